```python
import functools
import jax
import jax.numpy as jnp
from jax import lax
import numpy as np

D_MODEL = 2048
BATCH = 4
SEQ = 2048
DEPTH = 1
DEC_BATCH = 32
DEC_SEQ = 4
PAST_LEN = 16384
PAGE_SIZE = 128

DV_G = 128
H_G = D_MODEL // 2 // DV_G
DK_G = DV_G // 2
GLA_RANK = 16
GLA_GATE_NORM = 16.0
GLA_CHUNK = 16
D_HF = 128
H_F = D_MODEL // 2 // D_HF
Q_BLOCK = 128
FOX_GATE_BIAS = 7.0
N_GROUPS = 4
E_PER_GROUP = 8
N_EXPERTS = N_GROUPS * E_PER_GROUP
TOP_K = 2
D_FF_E = D_MODEL // 4
EPS = 1e-6
D_IN = 2 * H_G * DK_G + 2 * H_G * DV_G + GLA_RANK + 3 * H_F * D_HF + H_F

kernel_name = 'hymba_gla_fox_hier_moe_adaln_step'


def _rmsnorm(x, w):
    xf = x.astype(jnp.float32)
    y = xf * lax.rsqrt(jnp.mean(xf * xf, axis=-1, keepdims=True) + EPS)
    return (y * w).astype(x.dtype)


def _rms_heads(x):
    xf = x.astype(jnp.float32)
    return (xf * lax.rsqrt(jnp.mean(xf * xf, axis=-1, keepdims=True) + EPS)).astype(x.dtype)


def _adaln(c, w_ada, b_ada):
    m = jax.nn.silu(c) @ w_ada + b_ada
    return jnp.split(m[:, None, :], 6, axis=-1)


def _split_points():
    sizes = (H_G * DK_G, H_G * DK_G, H_G * DV_G, H_G * DV_G, GLA_RANK,
             H_F * D_HF, H_F * D_HF, H_F * D_HF, H_F)
    points, acc = [], 0
    for s in sizes[:-1]:
        acc += s
        points.append(acc)
    return points


def _project(h, w_in, w_gk2, b_gk, b_fgate):
    B, T, _ = h.shape
    p = h @ w_in
    qg, kg, vg, gg, alr, qf, kf, vf, fl = jnp.split(p, _split_points(), axis=-1)
    gk = jax.nn.log_sigmoid((alr @ w_gk2 + b_gk).astype(jnp.float32)) / GLA_GATE_NORM
    logf = jax.nn.log_sigmoid((fl + b_fgate).astype(jnp.float32))
    return (qg.reshape(B, T, H_G, DK_G), kg.reshape(B, T, H_G, DK_G), vg.reshape(B, T, H_G, DV_G),
            gg, gk.reshape(B, T, H_G, DK_G),
            qf.reshape(B, T, H_F, D_HF), kf.reshape(B, T, H_F, D_HF), vf.reshape(B, T, H_F, D_HF), logf)


def _gla(q, k, v, gk, s0):
    B, T = q.shape[:2]
    C = GLA_CHUNK if T % GLA_CHUNK == 0 else T
    n = T // C

    def chunks(a):
        return a.astype(jnp.float32).reshape(B, n, C, a.shape[2], a.shape[3]).transpose(1, 0, 3, 2, 4)

    qc = chunks(q) * (DK_G ** -0.5)
    kc, vc, gc = chunks(k), chunks(v), chunks(gk)
    b = jnp.cumsum(gc, axis=3)
    b_end = b[:, :, :, -1:, :]
    causal = jnp.tril(jnp.ones((C, C), dtype=bool))
    rel = b[:, :, :, :, None, :] - b[:, :, :, None, :, :]
    rel = jnp.exp(jnp.where(causal[:, :, None], rel, -jnp.inf))
    scores = jnp.einsum('nbhtk,nbhsk,nbhtsk->nbhts', qc, kc, rel)
    o_intra = jnp.einsum('nbhts,nbhsv->nbhtv', scores, vc)
    upd = jnp.einsum('nbhck,nbhcv->nbhkv', kc * jnp.exp(b_end - b), vc)
    dec = jnp.exp(b_end[:, :, :, 0, :])

    def step(s, inp):
        d, u = inp
        return d[..., None] * s + u, s

    s_fin, s_prev = lax.scan(step, s0.astype(jnp.float32), (dec, upd))
    o_inter = jnp.einsum('nbhtk,nbhkv->nbhtv', qc * jnp.exp(b), s_prev)
    o = (o_intra + o_inter).transpose(1, 0, 3, 2, 4).reshape(B, T, H_G, DV_G)
    return o.astype(q.dtype), s_fin


def _fox_prompt(q, k, v, logf):
    B, S = q.shape[:2]
    nb = S // Q_BLOCK
    F = jnp.cumsum(logf, axis=1).transpose(0, 2, 1)
    qb = q.reshape(B, nb, Q_BLOCK, H_F, D_HF).transpose(1, 0, 2, 3, 4)
    Fq = F.reshape(B, H_F, nb, Q_BLOCK).transpose(2, 0, 1, 3)
    kpos = jnp.arange(S)

    def block(inp):
        i, qi, fi = inp
        s = jnp.einsum('bthd,bshd->bhts', qi, k, preferred_element_type=jnp.float32) * (D_HF ** -0.5)
        s = s + fi[..., :, None] - F[:, :, None, :]
        qpos = i * Q_BLOCK + jnp.arange(Q_BLOCK)
        s = jnp.where(kpos[None, :] <= qpos[:, None], s, -jnp.inf)
        p = jax.nn.softmax(s, axis=-1)
        return jnp.einsum('bhts,bshd->bthd', p.astype(v.dtype), v)

    o = lax.map(block, (jnp.arange(nb), qb, Fq))
    return o.transpose(1, 0, 2, 3, 4).reshape(B, S, H_F, D_HF)


def _fox_sample(q, k, v, logf, cache_k, cache_v, cache_logf, page_table, layer):
    DB, T = q.shape[:2]
    n_pages = page_table.shape[1]
    scale = D_HF ** -0.5
    past = cache_logf[layer, page_table].astype(jnp.float32).reshape(DB, n_pages * PAGE_SIZE, H_F)
    suffix = lax.cumsum(past, axis=1, reverse=True) - past
    suffix = suffix.reshape(DB, n_pages, PAGE_SIZE, H_F).transpose(1, 0, 3, 2)
    Fn = jnp.cumsum(logf, axis=1).transpose(0, 2, 1)

    def page(inp):
        idx, suf = inp
        kp = cache_k[layer, idx]
        vp = cache_v[layer, idx]
        s = jnp.einsum('bthd,bshd->bhts', q, kp, preferred_element_type=jnp.float32) * scale
        s = s + Fn[..., :, None] + suf[:, :, None, :]
        m = jnp.max(s, axis=-1)
        e = jnp.exp(s - m[..., None])
        return m, jnp.sum(e, axis=-1), jnp.einsum('bhts,bshd->bhtd', e, vp.astype(jnp.float32))

    m_p, l_p, o_p = lax.map(page, (page_table.T, suffix))
    s = jnp.einsum('bthd,bshd->bhts', q, k, preferred_element_type=jnp.float32) * scale
    s = s + Fn[..., :, None] - Fn[..., None, :]
    s = jnp.where(jnp.tril(jnp.ones((T, T), dtype=bool)), s, -jnp.inf)
    m_s = jnp.max(s, axis=-1)
    e_s = jnp.exp(s - m_s[..., None])
    l_s = jnp.sum(e_s, axis=-1)
    o_s = jnp.einsum('bhts,bshd->bhtd', e_s, v.astype(jnp.float32))
    m_all = jnp.maximum(jnp.max(m_p, axis=0), m_s)
    w_p = jnp.exp(m_p - m_all)
    w_s = jnp.exp(m_s - m_all)
    num = jnp.einsum('nbht,nbhtd->bhtd', w_p, o_p) + w_s[..., None] * o_s
    den = jnp.sum(w_p * l_p, axis=0) + w_s * l_s
    return (num / den[..., None]).transpose(0, 2, 1, 3).astype(q.dtype)


def _merge(o_gla, gg, o_fox, gla_onorm_w, fox_onorm_w, w_out):
    B, T = o_gla.shape[:2]
    og = _rms_heads(o_gla).reshape(B, T, H_G * DV_G) * gla_onorm_w * jax.nn.silu(gg)
    of = _rms_heads(o_fox).reshape(B, T, H_F * D_HF) * fox_onorm_w
    return jnp.concatenate([og, of], axis=-1) @ w_out


def _hier_moe(h, w_rg, b_rg, w_re, b_re, w_gate_e, w_up_e, w_down_e):
    B, T, D = h.shape
    ht = h.reshape(B * T, D)
    gl = (ht @ w_rg + b_rg).astype(jnp.float32)
    g_sel = jnp.argmax(gl, axis=-1)
    p_g = jnp.take_along_axis(jax.nn.softmax(gl, axis=-1), g_sel[:, None], axis=-1)
    el = (ht @ w_re + b_re).astype(jnp.float32).reshape(-1, N_GROUPS, E_PER_GROUP)
    el_sel = jnp.take_along_axis(el, g_sel[:, None, None], axis=1)[:, 0]
    top_v, top_i = lax.top_k(el_sel, TOP_K)
    w_top = jax.nn.softmax(top_v, axis=-1) * p_g
    e_idx = g_sel[:, None] * E_PER_GROUP + top_i
    comb = jnp.sum(jax.nn.one_hot(e_idx, N_EXPERTS, dtype=jnp.float32) * w_top[..., None], axis=1)
    a = jax.nn.silu(jnp.einsum('nd,edf->nef', ht, w_gate_e)) * jnp.einsum('nd,edf->nef', ht, w_up_e)
    a = a * comb[..., None].astype(a.dtype)
    return jnp.einsum('nef,efd->nd', a, w_down_e).reshape(B, T, D)


def _layer(x, c, s0, fox_fn, norm1_w, norm2_w, w_ada, b_ada, w_in, w_gk2, b_gk, b_fgate,
           gla_onorm_w, fox_onorm_w, w_out, w_rg, b_rg, w_re, b_re, w_gate_e, w_up_e, w_down_e):
    sh1, sc1, g1, sh2, sc2, g2 = _adaln(c, w_ada, b_ada)
    h = _rmsnorm(x, norm1_w) * (1.0 + sc1) + sh1
    qg, kg, vg, gg, gk, qf, kf, vf, logf = _project(h, w_in, w_gk2, b_gk, b_fgate)
    o_gla, s_new = _gla(qg, kg, vg, gk, s0)
    o_fox = fox_fn(qf, kf, vf, logf)
    x = x + g1 * _merge(o_gla, gg, o_fox, gla_onorm_w, fox_onorm_w, w_out)
    h2 = _rmsnorm(x, norm2_w) * (1.0 + sc2) + sh2
    x = x + g2 * _hier_moe(h2, w_rg, b_rg, w_re, b_re, w_gate_e, w_up_e, w_down_e)
    return x, kf, vf, logf, s_new


def setup_inputs(seed: int = 0) -> dict:
    key = jax.random.key(seed)
    keys = iter(jax.random.split(key, 40))
    f32 = jnp.float32

    def nrm(shape, scale):
        return jax.random.normal(next(keys), shape, f32) * scale

    n_pages = PAST_LEN // PAGE_SIZE
    n_used = DEC_BATCH * n_pages
    n_pool = n_used + n_used // 4
    page_table = jax.random.permutation(next(keys), n_pool)[:n_used].reshape(DEC_BATCH, n_pages).astype(jnp.int32)
    dm = D_MODEL ** -0.5
    return {
        'x_prompt': nrm((BATCH, SEQ, D_MODEL), 1.0),
        'x_sample': nrm((DEC_BATCH, DEC_SEQ, D_MODEL), 1.0),
        'cache_k': nrm((DEPTH, n_pool, PAGE_SIZE, H_F, D_HF), 1.0),
        'cache_v': nrm((DEPTH, n_pool, PAGE_SIZE, H_F, D_HF), 1.0),
        'cache_logf': jax.nn.log_sigmoid(FOX_GATE_BIAS + nrm((DEPTH, n_pool, PAGE_SIZE, H_F), 1.0)),
        'state_gla': nrm((DEPTH, DEC_BATCH, H_G, DK_G, DV_G), 1.0),
        'page_table': page_table,
        'c_prompt': nrm((BATCH, D_MODEL), 1.0),
        'c_sample': nrm((DEC_BATCH, D_MODEL), 1.0),
        'norm1_w': 1.0 + nrm((DEPTH, D_MODEL), 0.02),
        'norm2_w': 1.0 + nrm((DEPTH, D_MODEL), 0.02),
        'w_ada': nrm((DEPTH, D_MODEL, 6 * D_MODEL), 0.3 * dm),
        'b_ada': nrm((DEPTH, 6 * D_MODEL), 0.02),
        'w_in': nrm((DEPTH, D_MODEL, D_IN), dm),
        'w_gk2': nrm((DEPTH, GLA_RANK, H_G * DK_G), GLA_RANK ** -0.5),
        'b_gk': nrm((DEPTH, H_G * DK_G), 0.1),
        'b_fgate': FOX_GATE_BIAS + nrm((DEPTH, H_F), 0.5),
        'gla_onorm_w': 1.0 + nrm((DEPTH, H_G * DV_G), 0.02),
        'fox_onorm_w': 1.0 + nrm((DEPTH, H_F * D_HF), 0.02),
        'w_out': nrm((DEPTH, D_MODEL, D_MODEL), dm),
        'w_rg': nrm((DEPTH, D_MODEL, N_GROUPS), dm),
        'b_rg': nrm((DEPTH, N_GROUPS), 0.01),
        'w_re': nrm((DEPTH, D_MODEL, N_EXPERTS), dm),
        'b_re': nrm((DEPTH, N_EXPERTS), 0.01),
        'w_gate_e': nrm((DEPTH, N_EXPERTS, D_MODEL, D_FF_E), dm),
        'w_up_e': nrm((DEPTH, N_EXPERTS, D_MODEL, D_FF_E), dm),
        'w_down_e': nrm((DEPTH, N_EXPERTS, D_FF_E, D_MODEL), D_FF_E ** -0.5),
        'final_norm_w': 1.0 + nrm((D_MODEL,), 0.02),
    }


def reference(x_prompt, x_sample, cache_k, cache_v, cache_logf, state_gla, page_table, c_prompt, c_sample,
              norm1_w, norm2_w, w_ada, b_ada, w_in, w_gk2, b_gk, b_fgate, gla_onorm_w, fox_onorm_w, w_out,
              w_rg, b_rg, w_re, b_re, w_gate_e, w_up_e, w_down_e, final_norm_w):
    x_p, x_s = x_prompt, x_sample
    kp_l, vp_l, fp_l, sp_l = [], [], [], []
    ks_l, vs_l, fs_l, ss_l = [], [], [], []
    for l in range(DEPTH):
        wl = (norm1_w[l], norm2_w[l], w_ada[l], b_ada[l], w_in[l], w_gk2[l], b_gk[l], b_fgate[l],
              gla_onorm_w[l], fox_onorm_w[l], w_out[l], w_rg[l], b_rg[l], w_re[l], b_re[l],
              w_gate_e[l], w_up_e[l], w_down_e[l])
        s0 = jnp.zeros((x_p.shape[0], H_G, DK_G, DV_G), x_p.dtype)
        x_p, kp, vp, fp, sp = _layer(x_p, c_prompt, s0, _fox_prompt, *wl)
        fox_s = functools.partial(_fox_sample, cache_k=cache_k, cache_v=cache_v, cache_logf=cache_logf,
                                  page_table=page_table, layer=l)
        x_s, ks, vs, fs, ss = _layer(x_s, c_sample, state_gla[l], fox_s, *wl)
        kp_l.append(kp)
        vp_l.append(vp)
        fp_l.append(fp)
        sp_l.append(sp)
        ks_l.append(ks)
        vs_l.append(vs)
        fs_l.append(fs)
        ss_l.append(ss)
    y_prompt = _rmsnorm(x_p, final_norm_w)
    y_sample = _rmsnorm(x_s, final_norm_w)
    k_prompt = jnp.stack(kp_l)
    v_prompt = jnp.stack(vp_l)
    logf_prompt = jnp.stack(fp_l)
    gla_prompt = jnp.stack(sp_l)
    k_sample = jnp.stack(ks_l)
    v_sample = jnp.stack(vs_l)
    logf_sample = jnp.stack(fs_l)
    gla_sample = jnp.stack(ss_l)
    return (y_prompt, y_sample, k_prompt, v_prompt, logf_prompt, gla_prompt, k_sample, v_sample, logf_sample, gla_sample)
```

```python
import functools

import jax
import jax.numpy as jnp
from jax import lax
from jax.experimental import pallas as pl
from jax.experimental.pallas import tpu as pltpu

F32 = jnp.float32
BF16 = jnp.bfloat16
I32 = jnp.int32
EPS = 1e-6
NEG_INF = float("-inf")

LANES = 128
GLA_CHUNK = 16
GLA_GATE_NORM = 16.0
N_GROUPS = 4
E_PER_GROUP = 8
VMEM_LIMIT = 56 * 1024 * 1024


def _cparams(sem):
    return pltpu.CompilerParams(dimension_semantics=sem, vmem_limit_bytes=VMEM_LIMIT)


def _log_sigmoid(z):
    return jnp.minimum(z, 0.0) - jnp.log1p(jnp.exp(-jnp.abs(z)))


def _silu(z):
    return z * jax.nn.sigmoid(z)


def _split3(a):
    hi = a.astype(BF16)
    r = a - hi.astype(F32)
    mid = r.astype(BF16)
    lo = (r - mid.astype(F32)).astype(BF16)
    return hi, mid, lo


def _dot_nn(a, b):
    return jnp.dot(a, b, preferred_element_type=F32)


def _dot_nt(a, b):
    return lax.dot_general(a, b, (((1,), (1,)), ((), ())), preferred_element_type=F32)


def _dot_f32(a, b):
    ah, al, _ = _split3(a)
    bh, bl, _ = _split3(b)
    return _dot_nn(ah, bh) + _dot_nn(al, bh) + _dot_nn(ah, bl)


def _dot_exact_rhs01(a, ones_bf16):
    hi, mid, lo = _split3(a)
    return _dot_nn(hi, ones_bf16) + _dot_nn(mid, ones_bf16) + _dot_nn(lo, ones_bf16)


def _dot_exact_lhs01(ones_bf16, b):
    hi, mid, lo = _split3(b)
    return _dot_nn(ones_bf16, hi) + _dot_nn(ones_bf16, mid) + _dot_nn(ones_bf16, lo)


def _adaln_body(c_ref, w_ref, b_ref, o_ref):
    s = _silu(c_ref[...]).astype(BF16)
    o_ref[...] = _dot_nn(s, w_ref[...].astype(BF16)) + b_ref[...]


def _adaln(c_all, w_ada, b_ada, tn=1024):
    rows, d = c_all.shape
    n6 = w_ada.shape[1]
    return pl.pallas_call(
        _adaln_body,
        grid=(n6 // tn,),
        in_specs=[pl.BlockSpec((rows, d), lambda j: (0, 0)),
                  pl.BlockSpec((d, tn), lambda j: (0, j)),
                  pl.BlockSpec((1, tn), lambda j: (0, j))],
        out_specs=pl.BlockSpec((rows, tn), lambda j: (0, j)),
        out_shape=jax.ShapeDtypeStruct((rows, n6), F32),
        compiler_params=_cparams(("arbitrary",)),
        name="adaln",
    )(c_all, w_ada, b_ada.reshape(1, n6))


def _inproj_body(x_ref, sh_ref, sc_ref, nw_ref, w_ref, ws_ref, bs_ref,
                 gla_ref, qf_ref, kf_ref, vf_ref, kfb_ref, vfb_ref, small_ref, h_scr, *, q_scale):
    j = pl.program_id(1)

    @pl.when(j == 0)
    def _():
        x = x_ref[...]
        y = x * lax.rsqrt(jnp.mean(x * x, axis=-1, keepdims=True) + EPS) * nw_ref[...]
        hb = (y * (1.0 + sc_ref[...]) + sh_ref[...]).astype(BF16)
        h_scr[...] = hb
        sm = _dot_nn(hb, ws_ref[...])
        lane = lax.broadcasted_iota(I32, sm.shape, 1)
        small_ref[...] = jnp.where((lane >= 16) & (lane < 24), _log_sigmoid(sm + bs_ref[...]), sm)

    acc = _dot_nn(h_scr[...], w_ref[...])

    @pl.when(j < 3)
    def _():
        gla_ref[...] = acc

    @pl.when(j == 3)
    def _():
        qf_ref[...] = (acc * q_scale).astype(BF16)

    @pl.when(j == 4)
    def _():
        kf_ref[...] = acc
        kfb_ref[...] = acc.astype(BF16)

    @pl.when(j == 5)
    def _():
        vf_ref[...] = acc
        vfb_ref[...] = acc.astype(BF16)


def _inproj(x, sh, sc, norm_w, w_main, w_small, b_small, tm, rows_per_mod, q_scale):
    n, d = x.shape
    tn = 1024
    per_token = rows_per_mod == 1
    if per_token:
        mod_spec = pl.BlockSpec((tm, d), lambda i, j: (i, 0))
    else:
        tiles_per_mod = rows_per_mod // tm
        sh = sh.reshape(sh.shape[0], 1, d)
        sc = sc.reshape(sc.shape[0], 1, d)
        mod_spec = pl.BlockSpec((None, 1, d), lambda i, j: (i // tiles_per_mod, 0, 0))
    row = lambda i, j: (i, 0)
    out_shapes = (jax.ShapeDtypeStruct((n, 3 * tn), F32),
                  jax.ShapeDtypeStruct((n, tn), BF16),
                  jax.ShapeDtypeStruct((n, tn), F32),
                  jax.ShapeDtypeStruct((n, tn), F32),
                  jax.ShapeDtypeStruct((n, tn), BF16),
                  jax.ShapeDtypeStruct((n, tn), BF16),
                  jax.ShapeDtypeStruct((n, LANES), F32))
    return pl.pallas_call(
        functools.partial(_inproj_body, q_scale=q_scale),
        grid=(n // tm, 6),
        in_specs=[pl.BlockSpec((tm, d), row), mod_spec, mod_spec,
                  pl.BlockSpec((1, d), lambda i, j: (0, 0)),
                  pl.BlockSpec((d, tn), lambda i, j: (0, j)),
                  pl.BlockSpec((d, LANES), lambda i, j: (0, 0)),
                  pl.BlockSpec((1, LANES), lambda i, j: (0, 0))],
        out_specs=(pl.BlockSpec((tm, tn), lambda i, j: (i, jnp.minimum(j, 2))),
                   pl.BlockSpec((tm, tn), row), pl.BlockSpec((tm, tn), row), pl.BlockSpec((tm, tn), row),
                   pl.BlockSpec((tm, tn), row), pl.BlockSpec((tm, tn), row),
                   pl.BlockSpec((tm, LANES), row)),
        out_shape=out_shapes,
        scratch_shapes=[pltpu.VMEM((tm, d), BF16)],
        compiler_params=_cparams(("arbitrary", "arbitrary")),
        name="inproj",
    )(x, sh, sc, norm_w.reshape(1, d), w_main, w_small, b_small)


def _gla_body(q_ref, k_ref, v_ref, sm_ref, w2_ref, b2_ref, s0_ref, o_ref, sout_ref, st_scr, *, tt, dk, dv, t_valid):
    t = pl.program_id(2)
    zero_blk = jnp.zeros((dv, dk), F32)

    @pl.when(t == 0)
    def _():
        top = jnp.concatenate([s0_ref[0].T, zero_blk], axis=1)
        bot = jnp.concatenate([zero_blk, s0_ref[1].T], axis=1)
        st_scr[...] = jnp.concatenate([top, bot], axis=0)

    gk = _log_sigmoid(_dot_f32(sm_ref[...], w2_ref[...]) + b2_ref[...]) / GLA_GATE_NORM
    if t_valid < tt:
        gk = jnp.where(lax.broadcasted_iota(I32, (tt, 1), 0) < t_valid, gk, 0.0)
    r_i = lax.broadcasted_iota(I32, (tt, tt), 0)
    c_i = lax.broadcasted_iota(I32, (tt, tt), 1)
    tri = ((r_i // GLA_CHUNK == c_i // GLA_CHUNK) & (c_i <= r_i)).astype(BF16)
    b = _dot_exact_lhs01(tri, gk)

    q = q_ref[...] * (dk ** -0.5)
    k = k_ref[...]
    v = v_ref[...]
    pos = lax.broadcasted_iota(I32, (tt, 1), 0) % GLA_CHUNK

    o0 = jnp.zeros((tt, dv), F32)
    o1 = jnp.zeros((tt, dv), F32)
    for d in range(GLA_CHUNK):
        if d == 0:
            k_d, b_d, v_d = k, b, v
        else:
            k_d = pltpu.roll(k, d, 0)
            b_d = pltpu.roll(b, d, 0)
            v_d = pltpu.roll(v, d, 0)
        w = q * k_d * jnp.exp(jnp.where(pos >= d, b - b_d, NEG_INF))
        a0 = jnp.sum(w[:, :dk], axis=1, keepdims=True)
        a1 = jnp.sum(w[:, dk:], axis=1, keepdims=True)
        o0 = o0 + a0 * v_d[:, :dv]
        o1 = o1 + a1 * v_d[:, dv:]
    o_intra = jnp.concatenate([o0, o1], axis=1)

    rr = lax.broadcasted_iota(I32, (2 * dv, 2 * dk), 0) // dv
    cc = lax.broadcasted_iota(I32, (2 * dv, 2 * dk), 1) // dk
    diag = (rr == cc).astype(F32)
    st = st_scr[...]
    outs = []
    for c in range(tt // GLA_CHUNK):
        sl = slice(c * GLA_CHUNK, (c + 1) * GLA_CHUNK)
        bc = b[sl]
        bend = bc[GLA_CHUNK - 1:GLA_CHUNK]
        qe = (q[sl] * jnp.exp(bc)).astype(BF16)
        outs.append(o_intra[sl] + _dot_nt(qe, st.astype(BF16)))
        kdec = (k[sl] * jnp.exp(bend - bc)).astype(BF16)
        upd_t = _dot_nn(v[sl].T.astype(BF16), kdec)
        st = st * jnp.exp(bend) + upd_t * diag
    o_ref[...] = jnp.concatenate(outs, axis=0)
    st_scr[...] = st

    @pl.when(t == pl.num_programs(2) - 1)
    def _():
        sout_ref[0] = st[:dv, :dk].T
        sout_ref[1] = st[dv:, dk:].T


def _gla(gla_in, small, w_gk2_pad, b_gk, s0, batch, seq, tt, n_heads, dk, dv, t_valid=None):
    n = batch * seq
    nt = seq // tt
    pairs = n_heads // 2
    row = lambda b, p, t: b * nt + t
    return pl.pallas_call(
        functools.partial(_gla_body, tt=tt, dk=dk, dv=dv, t_valid=tt if t_valid is None else t_valid),
        grid=(batch, pairs, nt),
        in_specs=[pl.BlockSpec((tt, 2 * dk), lambda b, p, t: (row(b, p, t), p)),
                  pl.BlockSpec((tt, 2 * dk), lambda b, p, t: (row(b, p, t), pairs + p)),
                  pl.BlockSpec((tt, 2 * dv), lambda b, p, t: (row(b, p, t), pairs + p)),
                  pl.BlockSpec((tt, LANES), lambda b, p, t: (row(b, p, t), 0)),
                  pl.BlockSpec((LANES, 2 * dk), lambda b, p, t: (0, p)),
                  pl.BlockSpec((1, 2 * dk), lambda b, p, t: (0, p)),
                  pl.BlockSpec((None, 2, dk, dv), lambda b, p, t: (b, p, 0, 0))],
        out_specs=(pl.BlockSpec((tt, 2 * dv), lambda b, p, t: (row(b, p, t), p)),
                   pl.BlockSpec((None, 2, dk, dv), lambda b, p, t: (b, p, 0, 0))),
        out_shape=(jax.ShapeDtypeStruct((n, n_heads * dv), F32),
                   jax.ShapeDtypeStruct((batch, n_heads, dk, dv), F32)),
        scratch_shapes=[pltpu.VMEM((2 * dv, 2 * dk), F32)],
        compiler_params=_cparams(("arbitrary", "arbitrary", "arbitrary")),
        name="gla",
    )(gla_in, gla_in, gla_in, small, w_gk2_pad, b_gk, s0)


def _cumsum_lanes_body(x_ref, o_ref, *, blk):
    s = x_ref.shape[-1]
    r_i = lax.broadcasted_iota(I32, (blk, blk), 0)
    c_i = lax.broadcasted_iota(I32, (blk, blk), 1)
    upper = (r_i <= c_i).astype(BF16)
    carry = jnp.zeros((x_ref.shape[0], 1), F32)
    for i in range(s // blk):
        part = _dot_exact_rhs01(x_ref[:, i * blk:(i + 1) * blk], upper) + carry
        o_ref[:, i * blk:(i + 1) * blk] = part
        carry = part[:, blk - 1:blk]


def _cumsum_lanes(x):
    b, h, s = x.shape
    return pl.pallas_call(
        functools.partial(_cumsum_lanes_body, blk=LANES),
        grid=(b,),
        in_specs=[pl.BlockSpec((None, h, s), lambda i: (i, 0, 0))],
        out_specs=pl.BlockSpec((None, h, s), lambda i: (i, 0, 0)),
        out_shape=jax.ShapeDtypeStruct((b, h, s), F32),
        compiler_params=_cparams(("arbitrary",)),
        name="logf_cumsum",
    )(x)


def _fox_prompt_body(q_ref, k_ref, v_ref, fk_ref, o_ref, m_scr, l_scr, acc_scr, *, n_heads, dh):
    qi = pl.program_id(1)
    ki = pl.program_id(2)
    tq = q_ref.shape[0]
    tk = k_ref.shape[0]

    @pl.when(ki == 0)
    def _():
        m_scr[...] = jnp.full(m_scr.shape, NEG_INF, F32)
        l_scr[...] = jnp.zeros(l_scr.shape, F32)
        acc_scr[...] = jnp.zeros(acc_scr.shape, F32)

    def step(masked):
        if masked:
            keep = lax.broadcasted_iota(I32, (tq, tk), 1) <= lax.broadcasted_iota(I32, (tq, tk), 0)
        for h in range(n_heads):
            hs = slice(h * dh, (h + 1) * dh)
            s = _dot_nt(q_ref[:, hs], k_ref[:, hs]) - fk_ref[h:h + 1, :]
            if masked:
                s = jnp.where(keep, s, NEG_INF)
            m_old = m_scr[h]
            m_new = jnp.maximum(m_old, jnp.max(s, axis=-1, keepdims=True))
            alpha = jnp.exp(m_old - m_new)
            p = jnp.exp(s - m_new)
            l_scr[h] = alpha * l_scr[h] + jnp.sum(p, axis=-1, keepdims=True)
            acc_scr[:, hs] = alpha * acc_scr[:, hs] + _dot_nn(p.astype(BF16), v_ref[:, hs])
            m_scr[h] = m_new

    @pl.when(ki < qi)
    def _():
        step(False)

    @pl.when(ki == qi)
    def _():
        step(True)

    @pl.when(ki == pl.num_programs(2) - 1)
    def _():
        for h in range(n_heads):
            hs = slice(h * dh, (h + 1) * dh)
            o_ref[:, hs] = acc_scr[:, hs] / l_scr[h]


def _fox_prompt(qf, kfb, vfb, f_row, batch, seq, n_heads, dh, tq):
    nq = seq // tq
    width = n_heads * dh
    kv_map = lambda b, qi, ki: (b * nq + jnp.minimum(ki, qi), 0)
    return pl.pallas_call(
        functools.partial(_fox_prompt_body, n_heads=n_heads, dh=dh),
        grid=(batch, nq, nq),
        in_specs=[pl.BlockSpec((tq, width), lambda b, qi, ki: (b * nq + qi, 0)),
                  pl.BlockSpec((tq, width), kv_map),
                  pl.BlockSpec((tq, width), kv_map),
                  pl.BlockSpec((None, n_heads, tq), lambda b, qi, ki: (b, 0, jnp.minimum(ki, qi)))],
        out_specs=pl.BlockSpec((tq, width), lambda b, qi, ki: (b * nq + qi, 0)),
        out_shape=jax.ShapeDtypeStruct((batch * seq, width), F32),
        scratch_shapes=[pltpu.VMEM((n_heads, tq, 1), F32), pltpu.VMEM((n_heads, tq, 1), F32),
                        pltpu.VMEM((tq, width), F32)],
        compiler_params=_cparams(("arbitrary", "arbitrary", "arbitrary")),
        name="fox_prompt",
    )(qf, kfb, vfb, f_row)


def _fox_sample_body(pt_ref, q_ref, kn_ref, vn_ref, lfn_ref, *rest, n_heads, dh, pages_per_step, t_new):
    P = pages_per_step
    k_refs = rest[:P]
    v_refs = rest[P:2 * P]
    lf_refs = rest[2 * P:3 * P]
    o_ref, m_scr, l_scr, acc_scr, carry_scr = rest[3 * P:]
    j = pl.program_id(1)
    rows = q_ref.shape[0]
    page = k_refs[0].shape[0]

    @pl.when(j == 0)
    def _():
        lf = lfn_ref[...]
        fn = lf
        for sft in range(1, t_new):
            shifted = jnp.concatenate([jnp.zeros((n_heads, sft), F32), lf[:, :rows - sft]], axis=1)
            fn = fn + shifted
        r_i = lax.broadcasted_iota(I32, (rows, rows), 0)
        c_i = lax.broadcasted_iota(I32, (rows, rows), 1)
        keep = (c_i <= r_i) & (c_i < t_new)
        for h in range(n_heads):
            hs = slice(h * dh, (h + 1) * dh)
            s = _dot_nt(q_ref[:, hs], kn_ref[:, hs].astype(BF16)) - fn[h:h + 1, :]
            s = jnp.where(keep, s, NEG_INF)
            m = jnp.max(s, axis=-1, keepdims=True)
            p = jnp.exp(s - m)
            acc = jnp.zeros((rows, dh), F32)
            for sft in range(t_new):
                acc = acc + p[:, sft:sft + 1] * vn_ref[sft:sft + 1, hs]
            m_scr[h] = m
            l_scr[h] = jnp.sum(p, axis=-1, keepdims=True)
            acc_scr[:, hs] = acc
        carry_scr[...] = jnp.zeros(carry_scr.shape, F32)

    r_i = lax.broadcasted_iota(I32, (page, page), 0)
    c_i = lax.broadcasted_iota(I32, (page, page), 1)
    later = (r_i > c_i).astype(BF16)
    carry = carry_scr[...]
    sufs = []
    for i in range(P):
        lf_t = lf_refs[i][...].T
        sufs.append(_dot_exact_rhs01(lf_t, later) + carry)
        carry = carry + jnp.sum(lf_t, axis=-1, keepdims=True)
    carry_scr[...] = carry
    suf = jnp.concatenate(sufs, axis=1)

    for h in range(n_heads):
        hs = slice(h * dh, (h + 1) * dh)
        k_h = jnp.concatenate([k_refs[i][:, h, :] for i in range(P)], axis=0).astype(BF16)
        v_h = jnp.concatenate([v_refs[i][:, h, :] for i in range(P)], axis=0).astype(BF16)
        s = _dot_nt(q_ref[:, hs], k_h) + suf[h:h + 1, :]
        m_old = m_scr[h]
        m_new = jnp.maximum(m_old, jnp.max(s, axis=-1, keepdims=True))
        alpha = jnp.exp(m_old - m_new)
        p = jnp.exp(s - m_new)
        l_scr[h] = alpha * l_scr[h] + jnp.sum(p, axis=-1, keepdims=True)
        acc_scr[:, hs] = alpha * acc_scr[:, hs] + _dot_nn(p.astype(BF16), v_h)
        m_scr[h] = m_new

    @pl.when(j == pl.num_programs(1) - 1)
    def _():
        for h in range(n_heads):
            hs = slice(h * dh, (h + 1) * dh)
            o_ref[:, hs] = acc_scr[:, hs] / l_scr[h]


def _fox_sample(page_table, q8, k8, v8, lf8_t, cache_k, cache_v, cache_logf, layer, n_heads, dh, t_new,
                pages_per_step):
    db, rows, width = q8.shape
    n_pages = page_table.shape[1]
    page = cache_k.shape[2]
    P = pages_per_step
    steps = n_pages // P

    def page_idx(b, j, pt, i):
        return pt[b, n_pages - 1 - (j * P + i)]

    def kv_spec(i):
        return pl.BlockSpec((None, None, page, n_heads, dh),
                            lambda b, j, pt: (layer, page_idx(b, j, pt, i), 0, 0, 0))

    def lf_spec(i):
        return pl.BlockSpec((None, None, page, n_heads),
                            lambda b, j, pt: (layer, page_idx(b, j, pt, i), 0, 0))

    new_spec = pl.BlockSpec((None, rows, width), lambda b, j, pt: (b, 0, 0))
    in_specs = ([new_spec, new_spec, new_spec, pl.BlockSpec((None, n_heads, rows), lambda b, j, pt: (b, 0, 0))]
                + [kv_spec(i) for i in range(P)] + [kv_spec(i) for i in range(P)]
                + [lf_spec(i) for i in range(P)])
    return pl.pallas_call(
        functools.partial(_fox_sample_body, n_heads=n_heads, dh=dh, pages_per_step=P, t_new=t_new),
        grid_spec=pltpu.PrefetchScalarGridSpec(
            num_scalar_prefetch=1,
            grid=(db, steps),
            in_specs=in_specs,
            out_specs=pl.BlockSpec((None, rows, width), lambda b, j, pt: (b, 0, 0)),
            scratch_shapes=[pltpu.VMEM((n_heads, rows, 1), F32), pltpu.VMEM((n_heads, rows, 1), F32),
                            pltpu.VMEM((rows, width), F32), pltpu.VMEM((n_heads, 1), F32)]),
        out_shape=jax.ShapeDtypeStruct((db, rows, width), F32),
        compiler_params=_cparams(("arbitrary", "arbitrary")),
        name="fox_sample",
    )(page_table, q8, k8, v8, lf8_t, *([cache_k] * P), *([cache_v] * P), *([cache_logf] * P))


def _rms_heads(x, n_heads, dh):
    outs = []
    for h in range(n_heads):
        xs = x[:, h * dh:(h + 1) * dh]
        outs.append(xs * lax.rsqrt(jnp.mean(xs * xs, axis=-1, keepdims=True) + EPS))
    return jnp.concatenate(outs, axis=1)


def _route(logits):
    lane = lax.broadcasted_iota(I32, logits.shape, 1)
    big = jnp.int32(LANES)
    gl = jnp.where(lane < N_GROUPS, logits, NEG_INF)
    gmax = jnp.max(gl, axis=-1, keepdims=True)
    g_sel = jnp.min(jnp.where(gl == gmax, lane, big), axis=-1, keepdims=True)
    p_g = 1.0 / jnp.sum(jnp.exp(gl - gmax), axis=-1, keepdims=True)
    lo = N_GROUPS + E_PER_GROUP * g_sel
    ev = jnp.where((lane >= lo) & (lane < lo + E_PER_GROUP), logits, NEG_INF)
    v1 = jnp.max(ev, axis=-1, keepdims=True)
    i1 = jnp.min(jnp.where(ev == v1, lane, big), axis=-1, keepdims=True)
    ev2 = jnp.where(lane == i1, NEG_INF, ev)
    v2 = jnp.max(ev2, axis=-1, keepdims=True)
    i2 = jnp.min(jnp.where(ev2 == v2, lane, big), axis=-1, keepdims=True)
    e21 = jnp.exp(v2 - v1)
    w1 = p_g / (1.0 + e21)
    w2 = p_g * e21 / (1.0 + e21)
    out = jnp.where(lane == 0, (i1 - N_GROUPS).astype(F32), 0.0)
    out = jnp.where(lane == 1, (i2 - N_GROUPS).astype(F32), out)
    out = jnp.where(lane == 2, w1, out)
    out = jnp.where(lane == 3, w2, out)
    return out


def _merge_body(x_ref, og_ref, gg_ref, of_ref, gw_ref, fw_ref, wo_ref, g1_ref, sh2_ref, sc2_ref, n2_ref,
                wr_hi_ref, wr_lo_ref, br_ref, x1_ref, h2_ref, route_ref, *, n_heads, dh):
    og = _rms_heads(og_ref[...], n_heads, dh) * gw_ref[...] * _silu(gg_ref[...])
    of = _rms_heads(of_ref[...], n_heads, dh) * fw_ref[...]
    merged = jnp.concatenate([og, of], axis=1).astype(BF16)
    x1 = x_ref[...] + g1_ref[...] * _dot_nn(merged, wo_ref[...])
    x1_ref[...] = x1
    y = x1 * lax.rsqrt(jnp.mean(x1 * x1, axis=-1, keepdims=True) + EPS) * n2_ref[...]
    h2 = y * (1.0 + sc2_ref[...]) + sh2_ref[...]
    h2_ref[...] = h2
    hh, hl, _ = _split3(h2)
    logits = (_dot_nn(hh, wr_hi_ref[...]) + _dot_nn(hl, wr_hi_ref[...]) + _dot_nn(hh, wr_lo_ref[...])
              + br_ref[...])
    route_ref[...] = _route(logits)


def _merge(x, o_gla, gla_in, o_fox, gla_w, fox_w, w_out_bf, g1, sh2, sc2, norm2_w, wr_hi, wr_lo, b_r,
           tm, rows_per_mod, n_heads, dh):
    n, d = x.shape
    half = n_heads * dh
    per_token = rows_per_mod == 1
    if per_token:
        mod_spec = pl.BlockSpec((tm, d), lambda i: (i, 0))
    else:
        tiles_per_mod = rows_per_mod // tm
        g1, sh2, sc2 = (a.reshape(a.shape[0], 1, d) for a in (g1, sh2, sc2))
        mod_spec = pl.BlockSpec((None, 1, d), lambda i: (i // tiles_per_mod, 0, 0))
    row = lambda i: (i, 0)
    fixed = lambda i: (0, 0)
    return pl.pallas_call(
        functools.partial(_merge_body, n_heads=n_heads, dh=dh),
        grid=(n // tm,),
        in_specs=[pl.BlockSpec((tm, d), row),
                  pl.BlockSpec((tm, half), row),
                  pl.BlockSpec((tm, half), lambda i: (i, 2)),
                  pl.BlockSpec((tm, half), row),
                  pl.BlockSpec((1, half), fixed), pl.BlockSpec((1, half), fixed),
                  pl.BlockSpec((d, d), fixed),
                  mod_spec, mod_spec, mod_spec,
                  pl.BlockSpec((1, d), fixed),
                  pl.BlockSpec((d, LANES), fixed), pl.BlockSpec((d, LANES), fixed),
                  pl.BlockSpec((1, LANES), fixed)],
        out_specs=(pl.BlockSpec((tm, d), row), pl.BlockSpec((tm, d), row), pl.BlockSpec((tm, LANES), row)),
        out_shape=(jax.ShapeDtypeStruct((n, d), F32), jax.ShapeDtypeStruct((n, d), F32),
                   jax.ShapeDtypeStruct((n, LANES), F32)),
        compiler_params=_cparams(("arbitrary",)),
        name="merge_outproj",
    )(x, o_gla, gla_in, o_fox, gla_w.reshape(1, half), fox_w.reshape(1, half), w_out_bf, g1, sh2, sc2,
      norm2_w.reshape(1, d), wr_hi, wr_lo, b_r)


def _experts_body(te_ref, src_ref, nt_ref, h_hbm, w_ref, wg_ref, wu_ref, wd_ref, y_ref,
                  xbuf, wg_bf, wu_bf, wd_bf, sem, *, tm):
    i = pl.program_id(0)
    active = i < nt_ref[0]

    @pl.when(active)
    def _():
        def issue(r, carry):
            src = src_ref[i * tm + r]
            pltpu.make_async_copy(h_hbm.at[pl.ds(src, 1), :], xbuf.at[pl.ds(r, 1), :], sem).start()
            return carry
        lax.fori_loop(0, tm, issue, 0)

        prev = te_ref[jnp.maximum(i - 1, 0)]

        @pl.when((i == 0) | (te_ref[i] != prev))
        def _():
            wg_bf[...] = wg_ref[...].astype(BF16)
            wu_bf[...] = wu_ref[...].astype(BF16)
            wd_bf[...] = wd_ref[...].astype(BF16)

        pltpu.make_async_copy(h_hbm.at[pl.ds(0, tm), :], xbuf, sem).wait()
        x = xbuf[...].astype(BF16)
        a = _silu(_dot_nn(x, wg_bf[...])) * _dot_nn(x, wu_bf[...])
        a = (a * w_ref[...]).astype(BF16)
        y_ref[...] = _dot_nn(a, wd_bf[...])

    @pl.when(jnp.logical_not(active))
    def _():
        y_ref[...] = jnp.zeros(y_ref.shape, F32)


def _experts(tile_expert, src_row, n_tiles, h2, w_sorted, w_gate, w_up, w_down, tm):
    n_slots = src_row.shape[0]
    max_tiles = n_slots // tm
    _, d, dff = w_gate.shape
    wmap = lambda i, te, src, nt: (te[i], 0, 0)
    return pl.pallas_call(
        functools.partial(_experts_body, tm=tm),
        grid_spec=pltpu.PrefetchScalarGridSpec(
            num_scalar_prefetch=3,
            grid=(max_tiles,),
            in_specs=[pl.BlockSpec(memory_space=pl.ANY),
                      pl.BlockSpec((tm, 1), lambda i, te, src, nt: (i, 0)),
                      pl.BlockSpec((None, d, dff), wmap),
                      pl.BlockSpec((None, d, dff), wmap),
                      pl.BlockSpec((None, dff, d), wmap)],
            out_specs=pl.BlockSpec((tm, d), lambda i, te, src, nt: (i, 0)),
            scratch_shapes=[pltpu.VMEM((tm, d), F32),
                            pltpu.VMEM((d, dff), BF16), pltpu.VMEM((d, dff), BF16), pltpu.VMEM((dff, d), BF16),
                            pltpu.SemaphoreType.DMA(())]),
        out_shape=jax.ShapeDtypeStruct((n_slots, d), F32),
        compiler_params=_cparams(("arbitrary",)),
        name="experts",
    )(tile_expert, src_row, n_tiles, h2, w_sorted, w_gate, w_up, w_down)


def _combine_body(pos_ref, y_hbm, x1_ref, g2_ref, fw_ref, o_ref, buf0, buf1, sem, *, tm, row0):
    i = pl.program_id(0)

    def issue(r, carry):
        tok = row0 + i * tm + r
        pltpu.make_async_copy(y_hbm.at[pl.ds(pos_ref[2 * tok], 1), :], buf0.at[pl.ds(r, 1), :], sem).start()
        pltpu.make_async_copy(y_hbm.at[pl.ds(pos_ref[2 * tok + 1], 1), :], buf1.at[pl.ds(r, 1), :], sem).start()
        return carry
    lax.fori_loop(0, tm, issue, 0)
    pltpu.make_async_copy(y_hbm.at[pl.ds(0, tm), :], buf0, sem).wait()
    pltpu.make_async_copy(y_hbm.at[pl.ds(0, tm), :], buf1, sem).wait()
    x2 = x1_ref[...] + g2_ref[...] * (buf0[...] + buf1[...])
    o_ref[...] = x2 * lax.rsqrt(jnp.mean(x2 * x2, axis=-1, keepdims=True) + EPS) * fw_ref[...]


def _combine(pos, y_sorted, x1, g2, final_w, row0, n_rows, tm, rows_per_mod):
    d = x1.shape[1]
    per_token = rows_per_mod == 1
    tile0 = row0 // tm
    if per_token:
        mod_spec = pl.BlockSpec((tm, d), lambda i, pos: (i, 0))
    else:
        tiles_per_mod = rows_per_mod // tm
        g2 = g2.reshape(g2.shape[0], 1, d)
        mod_spec = pl.BlockSpec((None, 1, d), lambda i, pos: (i // tiles_per_mod, 0, 0))
    return pl.pallas_call(
        functools.partial(_combine_body, tm=tm, row0=row0),
        grid_spec=pltpu.PrefetchScalarGridSpec(
            num_scalar_prefetch=1,
            grid=(n_rows // tm,),
            in_specs=[pl.BlockSpec(memory_space=pl.ANY),
                      pl.BlockSpec((tm, d), lambda i, pos: (tile0 + i, 0)),
                      mod_spec,
                      pl.BlockSpec((1, d), lambda i, pos: (0, 0))],
            out_specs=pl.BlockSpec((tm, d), lambda i, pos: (i, 0)),
            scratch_shapes=[pltpu.VMEM((tm, d), F32), pltpu.VMEM((tm, d), F32), pltpu.SemaphoreType.DMA(())]),
        out_shape=jax.ShapeDtypeStruct((n_rows, d), F32),
        compiler_params=_cparams(("arbitrary",)),
        name="combine_norm",
    )(pos, y_sorted, x1, g2, final_w.reshape(1, d))


def _plan(route, n_experts, tm):
    n = route.shape[0]
    e_flat = route[:, 0:2].astype(I32).reshape(-1)
    w_flat = route[:, 2:4].reshape(-1)
    onehot = (e_flat[:, None] == jnp.arange(n_experts, dtype=I32)[None, :]).astype(I32)
    csum = jnp.cumsum(onehot, axis=0)
    rank = jnp.take_along_axis(csum, e_flat[:, None], axis=1)[:, 0] - 1
    counts = csum[-1]
    tiles_e = (counts + tm - 1) // tm
    tile_end = jnp.cumsum(tiles_e)
    tile_off = tile_end - tiles_e
    n_tiles = tile_end[-1]
    max_tiles = (2 * n) // tm + n_experts
    dest = tile_off[e_flat] * tm + rank
    token = jnp.arange(2 * n, dtype=I32) // 2
    src_row = jnp.zeros((max_tiles * tm,), I32).at[dest].set(token)
    w_sorted = jnp.zeros((max_tiles * tm,), F32).at[dest].set(w_flat)
    tile_expert = jnp.minimum(jnp.searchsorted(tile_end, jnp.arange(max_tiles, dtype=I32), side="right"),
                              n_experts - 1).astype(I32)
    return tile_expert, src_row, n_tiles.reshape(1).astype(I32), w_sorted.reshape(-1, 1), dest.astype(I32)


def kernel(x_prompt, x_sample, cache_k, cache_v, cache_logf, state_gla, page_table, c_prompt, c_sample,
           norm1_w, norm2_w, w_ada, b_ada, w_in, w_gk2, b_gk, b_fgate, gla_onorm_w, fox_onorm_w, w_out,
           w_rg, b_rg, w_re, b_re, w_gate_e, w_up_e, w_down_e, final_norm_w):
    depth = w_in.shape[0]
    assert depth == 1, "single-layer trunk"
    batch, seq, d = x_prompt.shape
    db, t_new, _ = x_sample.shape
    n_hg, dk, dv = state_gla.shape[2:]
    n_hf, dh = cache_k.shape[3:]
    rank = w_gk2.shape[1]
    n_experts = w_gate_e.shape[1]
    wq = n_hg * dk
    wv = n_hg * dv
    wf = n_hf * dh
    assert wv == 1024 and wf == 1024 and 2 * wq == 1024 and rank == 16 and n_hf == 8
    n_p = batch * seq
    n_s = db * t_new
    layer = 0

    wi = w_in[layer]
    o_alr = 2 * wq + 2 * wv
    o_fox = o_alr + rank
    o_fl = o_fox + 3 * wf
    w_main = jnp.concatenate([wi[:, :o_alr], wi[:, o_fox:o_fl]], axis=1).astype(BF16)
    w_small = jnp.concatenate([wi[:, o_alr:o_fox], wi[:, o_fl:], jnp.zeros((d, LANES - rank - n_hf), F32)],
                              axis=1).astype(BF16)
    b_small = jnp.zeros((1, LANES), F32).at[0, rank:rank + n_hf].set(b_fgate[layer])
    w_gk2_pad = jnp.zeros((LANES, wq), F32).at[:rank].set(w_gk2[layer])
    b_gk2 = b_gk[layer].reshape(1, wq)
    w_out_bf = w_out[layer].astype(BF16)
    w_r = jnp.concatenate([w_rg[layer], w_re[layer],
                           jnp.zeros((d, LANES - N_GROUPS - n_experts), F32)], axis=1)
    wr_hi = w_r.astype(BF16)
    wr_lo = (w_r - wr_hi.astype(F32)).astype(BF16)
    b_r = jnp.concatenate([b_rg[layer], b_re[layer], jnp.zeros((LANES - N_GROUPS - n_experts,), F32)]).reshape(1, LANES)

    n_c = batch + db
    c_all = jnp.concatenate([c_prompt, c_sample, jnp.zeros((-n_c % 8, d), F32)], axis=0)
    mod = _adaln(c_all, w_ada[layer], b_ada[layer])
    sh1, sc1, g1, sh2, sc2, g2 = (mod[:, i * d:(i + 1) * d] for i in range(6))
    p_rows = slice(0, batch)
    rep = lambda a: jnp.repeat(a[batch:n_c], t_new, axis=0)

    q_scale = dh ** -0.5
    xp = x_prompt.reshape(n_p, d)
    xs = x_sample.reshape(n_s, d)

    tm_p = 512
    gla_p, qf_p, kf_p, vf_p, kfb_p, vfb_p, small_p = _inproj(
        xp, sh1[p_rows], sc1[p_rows], norm1_w[layer], w_main, w_small, b_small, tm_p, seq, q_scale)
    s0_p = jnp.zeros((batch, n_hg, dk, dv), F32)
    o_gla_p, gla_state_p = _gla(gla_p, small_p, w_gk2_pad, b_gk2, s0_p, batch, seq, 128, n_hg, dk, dv)
    logf_p = small_p[:, rank:rank + n_hf]
    f_row = _cumsum_lanes(logf_p.reshape(batch, seq, n_hf).transpose(0, 2, 1))
    o_fox_p = _fox_prompt(qf_p, kfb_p, vfb_p, f_row, batch, seq, n_hf, dh, 512)
    x1_p, h2_p, route_p = _merge(xp, o_gla_p, gla_p, o_fox_p, gla_onorm_w[layer], fox_onorm_w[layer], w_out_bf,
                                 g1[p_rows], sh2[p_rows], sc2[p_rows], norm2_w[layer], wr_hi, wr_lo, b_r,
                                 256, seq, n_hg, dv)

    gla_s, qf_s, kf_s, vf_s, _, _, small_s = _inproj(
        xs, rep(sh1), rep(sc1), norm1_w[layer], w_main, w_small, b_small, n_s, 1, q_scale)
    t_pad = GLA_CHUNK
    pad_t = lambda a: jnp.pad(a.reshape(db, t_new, -1), ((0, 0), (0, t_pad - t_new), (0, 0))).reshape(db * t_pad, -1)
    o_gla_s_pad, gla_state_s = _gla(pad_t(gla_s), pad_t(small_s), w_gk2_pad, b_gk2, state_gla[layer],
                                    db, t_pad, t_pad, n_hg, dk, dv, t_valid=t_new)
    o_gla_s = o_gla_s_pad.reshape(db, t_pad, wv)[:, :t_new].reshape(n_s, wv)

    rows8 = 8
    pad8 = lambda a: jnp.pad(a.reshape(db, t_new, -1), ((0, 0), (0, rows8 - t_new), (0, 0)))
    logf_s = small_s[:, rank:rank + n_hf]
    lf8_t = pad8(logf_s).transpose(0, 2, 1)
    o_fox_s8 = _fox_sample(page_table, pad8(qf_s), pad8(kf_s), pad8(vf_s), lf8_t, cache_k, cache_v, cache_logf,
                           layer, n_hf, dh, t_new, 4)
    o_fox_s = o_fox_s8[:, :t_new].reshape(n_s, wf)
    x1_s, h2_s, route_s = _merge(xs, o_gla_s, gla_s, o_fox_s, gla_onorm_w[layer], fox_onorm_w[layer], w_out_bf,
                                 rep(g1), rep(sh2), rep(sc2), norm2_w[layer], wr_hi, wr_lo, b_r,
                                 n_s, 1, n_hg, dv)

    tm_e = 256
    x1 = jnp.concatenate([x1_p, x1_s], axis=0)
    h2 = jnp.concatenate([h2_p, h2_s], axis=0)
    route = jnp.concatenate([route_p, route_s], axis=0)
    tile_expert, src_row, n_tiles, w_sorted, pos = _plan(route, n_experts, tm_e)
    y_sorted = _experts(tile_expert, src_row, n_tiles, h2, w_sorted, w_gate_e[layer], w_up_e[layer],
                        w_down_e[layer], tm_e)
    y_p = _combine(pos, y_sorted, x1, g2[p_rows], final_norm_w, 0, n_p, 256, seq)
    y_s = _combine(pos, y_sorted, x1, rep(g2), final_norm_w, n_p, n_s, n_s, 1)

    y_prompt = y_p.reshape(batch, seq, d)
    y_sample = y_s.reshape(db, t_new, d)
    k_prompt = kf_p.reshape(1, batch, seq, n_hf, dh)
    v_prompt = vf_p.reshape(1, batch, seq, n_hf, dh)
    logf_prompt = logf_p.reshape(1, batch, seq, n_hf)
    k_sample = kf_s.reshape(1, db, t_new, n_hf, dh)
    v_sample = vf_s.reshape(1, db, t_new, n_hf, dh)
    logf_sample = logf_s.reshape(1, db, t_new, n_hf)
    return (y_prompt, y_sample, k_prompt, v_prompt, logf_prompt, gla_state_p[None],
            k_sample, v_sample, logf_sample, gla_state_s[None])
```

```python
import functools

import jax
import jax.numpy as jnp
from jax import lax
from jax.experimental import pallas as pl
from jax.experimental.pallas import tpu as pltpu

F32 = jnp.float32
BF16 = jnp.bfloat16
I32 = jnp.int32
EPS = 1e-6
NEG_INF = float("-inf")

LANES = 128
GLA_CHUNK = 16
GLA_GATE_NORM = 16.0
N_GROUPS = 4
E_PER_GROUP = 8
VMEM_LIMIT = 56 * 1024 * 1024


def _cparams(sem):
    return pltpu.CompilerParams(dimension_semantics=sem, vmem_limit_bytes=VMEM_LIMIT)


def _log_sigmoid(z):
    return jnp.minimum(z, 0.0) - jnp.log1p(jnp.exp(-jnp.abs(z)))


def _silu(z):
    return z * jax.nn.sigmoid(z)


def _split3(a):
    hi = a.astype(BF16)
    r = a - hi.astype(F32)
    mid = r.astype(BF16)
    lo = (r - mid.astype(F32)).astype(BF16)
    return hi, mid, lo


def _dot_nn(a, b):
    return jnp.dot(a, b, preferred_element_type=F32)


def _dot_nt(a, b):
    return lax.dot_general(a, b, (((1,), (1,)), ((), ())), preferred_element_type=F32)


def _dot_f32(a, b):
    ah, al, _ = _split3(a)
    bh, bl, _ = _split3(b)
    return _dot_nn(ah, bh) + _dot_nn(al, bh) + _dot_nn(ah, bl)


def _dot_exact_rhs01(a, ones_bf16):
    hi, mid, lo = _split3(a)
    return _dot_nn(hi, ones_bf16) + _dot_nn(mid, ones_bf16) + _dot_nn(lo, ones_bf16)


def _dot_exact_lhs01(ones_bf16, b):
    hi, mid, lo = _split3(b)
    return _dot_nn(ones_bf16, hi) + _dot_nn(ones_bf16, mid) + _dot_nn(ones_bf16, lo)


def _adaln_body(c_ref, w_ref, b_ref, o_ref):
    s = _silu(c_ref[...]).astype(BF16)
    o_ref[...] = _dot_nn(s, w_ref[...].astype(BF16)) + b_ref[...]


def _adaln(c_all, w_ada, b_ada, tn=1024):
    rows, d = c_all.shape
    n6 = w_ada.shape[1]
    return pl.pallas_call(
        _adaln_body,
        grid=(n6 // tn,),
        in_specs=[pl.BlockSpec((rows, d), lambda j: (0, 0)),
                  pl.BlockSpec((d, tn), lambda j: (0, j)),
                  pl.BlockSpec((1, tn), lambda j: (0, j))],
        out_specs=pl.BlockSpec((rows, tn), lambda j: (0, j)),
        out_shape=jax.ShapeDtypeStruct((rows, n6), F32),
        compiler_params=_cparams(("arbitrary",)),
        name="adaln",
    )(c_all, w_ada, b_ada.reshape(1, n6))


def _inproj_body(x_ref, sh_ref, sc_ref, nw_ref, w_ref, ws_ref, bs_ref,
                 gla_ref, qf_ref, kf_ref, vf_ref, kfb_ref, vfb_ref, small_ref, h_scr, *, q_scale):
    j = pl.program_id(1)

    @pl.when(j == 0)
    def _():
        x = x_ref[...]
        y = x * lax.rsqrt(jnp.mean(x * x, axis=-1, keepdims=True) + EPS) * nw_ref[...]
        hb = (y * (1.0 + sc_ref[...]) + sh_ref[...]).astype(BF16)
        h_scr[...] = hb
        sm = _dot_nn(hb, ws_ref[...])
        lane = lax.broadcasted_iota(I32, sm.shape, 1)
        small_ref[...] = jnp.where((lane >= 16) & (lane < 24), _log_sigmoid(sm + bs_ref[...]), sm)

    acc = _dot_nn(h_scr[...], w_ref[...])

    @pl.when(j < 3)
    def _():
        gla_ref[...] = acc

    @pl.when(j == 3)
    def _():
        qf_ref[...] = (acc * q_scale).astype(BF16)

    @pl.when(j == 4)
    def _():
        kf_ref[...] = acc
        kfb_ref[...] = acc.astype(BF16)

    @pl.when(j == 5)
    def _():
        vf_ref[...] = acc
        vfb_ref[...] = acc.astype(BF16)


def _inproj(x, sh, sc, norm_w, w_main, w_small, b_small, tm, rows_per_mod, q_scale):
    n, d = x.shape
    tn = 1024
    per_token = rows_per_mod == 1
    if per_token:
        mod_spec = pl.BlockSpec((tm, d), lambda i, j: (i, 0))
    else:
        tiles_per_mod = rows_per_mod // tm
        sh = sh.reshape(sh.shape[0], 1, d)
        sc = sc.reshape(sc.shape[0], 1, d)
        mod_spec = pl.BlockSpec((None, 1, d), lambda i, j: (i // tiles_per_mod, 0, 0))
    row = lambda i, j: (i, 0)
    out_shapes = (jax.ShapeDtypeStruct((n, 3 * tn), F32),
                  jax.ShapeDtypeStruct((n, tn), BF16),
                  jax.ShapeDtypeStruct((n, tn), F32),
                  jax.ShapeDtypeStruct((n, tn), F32),
                  jax.ShapeDtypeStruct((n, tn), BF16),
                  jax.ShapeDtypeStruct((n, tn), BF16),
                  jax.ShapeDtypeStruct((n, LANES), F32))
    return pl.pallas_call(
        functools.partial(_inproj_body, q_scale=q_scale),
        grid=(n // tm, 6),
        in_specs=[pl.BlockSpec((tm, d), row), mod_spec, mod_spec,
                  pl.BlockSpec((1, d), lambda i, j: (0, 0)),
                  pl.BlockSpec((d, tn), lambda i, j: (0, j)),
                  pl.BlockSpec((d, LANES), lambda i, j: (0, 0)),
                  pl.BlockSpec((1, LANES), lambda i, j: (0, 0))],
        out_specs=(pl.BlockSpec((tm, tn), lambda i, j: (i, jnp.minimum(j, 2))),
                   pl.BlockSpec((tm, tn), row), pl.BlockSpec((tm, tn), row), pl.BlockSpec((tm, tn), row),
                   pl.BlockSpec((tm, tn), row), pl.BlockSpec((tm, tn), row),
                   pl.BlockSpec((tm, LANES), row)),
        out_shape=out_shapes,
        scratch_shapes=[pltpu.VMEM((tm, d), BF16)],
        compiler_params=_cparams(("arbitrary", "arbitrary")),
        name="inproj",
    )(x, sh, sc, norm_w.reshape(1, d), w_main, w_small, b_small)


def _level_ref(b, half, rows_i):
    tt, n = b.shape
    if half >= 4:
        size = 2 * half
        blocks = [jnp.broadcast_to(b[g * size + half - 1:g * size + half, :], (size, n)) for g in range(tt // size)]
        return blocks[0] if len(blocks) == 1 else jnp.concatenate(blocks, axis=0)
    if half == 2:
        lo = jnp.concatenate([jnp.broadcast_to(b[8 * g + 1:8 * g + 2, :], (8, n)) for g in range(tt // 8)], axis=0)
        hi = jnp.concatenate([jnp.broadcast_to(b[8 * g + 5:8 * g + 6, :], (8, n)) for g in range(tt // 8)], axis=0)
        return jnp.where(rows_i % 8 < 4, lo, hi)
    return jnp.where(rows_i % 2 == 1, pltpu.roll(b, 1, 0), b)


def _gla_body(q_ref, k_ref, v_ref, sm_ref, w2_ref, b2_ref, s0_ref, o_ref, sout_ref, st_scr,
              *, tt, dk, dv, n_pairs, t_valid):
    t = pl.program_id(1)
    zero_blk = jnp.zeros((dv, dk), F32)

    @pl.when(t == 0)
    def _():
        for p in range(n_pairs):
            top = jnp.concatenate([s0_ref[2 * p].T, zero_blk], axis=1)
            bot = jnp.concatenate([zero_blk, s0_ref[2 * p + 1].T], axis=1)
            st_scr[p] = jnp.concatenate([top, bot], axis=0)

    gk = _log_sigmoid(_dot_f32(sm_ref[...], w2_ref[...]) + b2_ref[...]) / GLA_GATE_NORM
    rows_i = lax.broadcasted_iota(I32, (tt, 1), 0)
    if t_valid < tt:
        gk = jnp.where(rows_i < t_valid, gk, 0.0)
    r_i = lax.broadcasted_iota(I32, (tt, tt), 0)
    c_i = lax.broadcasted_iota(I32, (tt, tt), 1)
    b_all = _dot_exact_lhs01((c_i <= r_i).astype(BF16), gk)

    halves = []
    half = tt // 2
    while half >= 1:
        halves.append(half)
        half //= 2
    refs = [_level_ref(b_all, hf, rows_i) for hf in halves]
    valid = [(r_i // (2 * hf) == c_i // (2 * hf)) & (r_i % (2 * hf) >= hf) & (c_i % (2 * hf) < hf) for hf in halves]
    on_diag = r_i == c_i
    head0 = lax.broadcasted_iota(I32, (tt, 2 * dk), 1) < dk
    rr = lax.broadcasted_iota(I32, (2 * dv, 2 * dk), 0) // dv
    cc = lax.broadcasted_iota(I32, (2 * dv, 2 * dk), 1) // dk
    diag = (rr == cc).astype(F32)

    def head_scores(qx, kx):
        stacked = jnp.concatenate([jnp.where(head0, qx, 0.0), jnp.where(head0, 0.0, qx)], axis=0).astype(BF16)
        return _dot_nt(stacked, kx.astype(BF16))

    for p in range(n_pairs):
        cs = slice(p * 2 * dk, (p + 1) * 2 * dk)
        vs = slice(p * 2 * dv, (p + 1) * 2 * dv)
        q = q_ref[:, cs] * (dk ** -0.5)
        k = k_ref[:, cs]
        v = v_ref[:, vs]
        b = b_all[:, cs]
        sc = head_scores(q, k)
        a0 = jnp.where(on_diag, sc[:tt], 0.0)
        a1 = jnp.where(on_diag, sc[tt:], 0.0)
        for ref_all, ok in zip(refs, valid):
            ref = ref_all[:, cs]
            sc = head_scores(q * jnp.exp(jnp.minimum(b - ref, 0.0)), k * jnp.exp(jnp.minimum(ref - b, 0.0)))
            a0 = jnp.where(ok, sc[:tt], a0)
            a1 = jnp.where(ok, sc[tt:], a1)
        vb = v.astype(BF16)
        o_intra = jnp.concatenate([_dot_nn(a0.astype(BF16), vb[:, :dv]), _dot_nn(a1.astype(BF16), vb[:, dv:])], axis=1)

        st = st_scr[p]
        bend = b[tt - 1:tt]
        o_inter = _dot_nt((q * jnp.exp(b)).astype(BF16), st.astype(BF16))
        kdec = (k * jnp.exp(bend - b)).astype(BF16)
        st = st * jnp.exp(bend) + _dot_nn(v.T.astype(BF16), kdec) * diag
        st_scr[p] = st
        o_ref[:, vs] = o_intra + o_inter

    @pl.when(t == pl.num_programs(1) - 1)
    def _():
        for p in range(n_pairs):
            st = st_scr[p]
            sout_ref[2 * p] = st[:dv, :dk].T
            sout_ref[2 * p + 1] = st[dv:, dk:].T


def _gla(gla_in, small, w_gk2_pad, b_gk, s0, batch, seq, tt, n_heads, dk, dv, t_valid=None):
    n = batch * seq
    nt = seq // tt
    wq = n_heads * dk
    wv = n_heads * dv
    assert wv == 2 * wq
    row = lambda b, t: b * nt + t
    return pl.pallas_call(
        functools.partial(_gla_body, tt=tt, dk=dk, dv=dv, n_pairs=n_heads // 2,
                          t_valid=tt if t_valid is None else t_valid),
        grid=(batch, nt),
        in_specs=[pl.BlockSpec((tt, wq), lambda b, t: (row(b, t), 0)),
                  pl.BlockSpec((tt, wq), lambda b, t: (row(b, t), 1)),
                  pl.BlockSpec((tt, wv), lambda b, t: (row(b, t), 1)),
                  pl.BlockSpec((tt, LANES), lambda b, t: (row(b, t), 0)),
                  pl.BlockSpec((LANES, wq), lambda b, t: (0, 0)),
                  pl.BlockSpec((1, wq), lambda b, t: (0, 0)),
                  pl.BlockSpec((None, n_heads, dk, dv), lambda b, t: (b, 0, 0, 0))],
        out_specs=(pl.BlockSpec((tt, wv), lambda b, t: (row(b, t), 0)),
                   pl.BlockSpec((None, n_heads, dk, dv), lambda b, t: (b, 0, 0, 0))),
        out_shape=(jax.ShapeDtypeStruct((n, wv), F32),
                   jax.ShapeDtypeStruct((batch, n_heads, dk, dv), F32)),
        scratch_shapes=[pltpu.VMEM((n_heads // 2, 2 * dv, 2 * dk), F32)],
        compiler_params=_cparams(("arbitrary", "arbitrary")),
        name="gla",
    )(gla_in, gla_in, gla_in, small, w_gk2_pad, b_gk, s0)


def _cumsum_lanes_body(x_ref, o_ref, *, blk):
    s = x_ref.shape[-1]
    r_i = lax.broadcasted_iota(I32, (blk, blk), 0)
    c_i = lax.broadcasted_iota(I32, (blk, blk), 1)
    upper = (r_i <= c_i).astype(BF16)
    carry = jnp.zeros((x_ref.shape[0], 1), F32)
    for i in range(s // blk):
        part = _dot_exact_rhs01(x_ref[:, i * blk:(i + 1) * blk], upper) + carry
        o_ref[:, i * blk:(i + 1) * blk] = part
        carry = part[:, blk - 1:blk]


def _cumsum_lanes(x):
    b, h, s = x.shape
    return pl.pallas_call(
        functools.partial(_cumsum_lanes_body, blk=LANES),
        grid=(b,),
        in_specs=[pl.BlockSpec((None, h, s), lambda i: (i, 0, 0))],
        out_specs=pl.BlockSpec((None, h, s), lambda i: (i, 0, 0)),
        out_shape=jax.ShapeDtypeStruct((b, h, s), F32),
        compiler_params=_cparams(("arbitrary",)),
        name="logf_cumsum",
    )(x)


def _fox_prompt_body(q_ref, k_ref, v_ref, fk_ref, o_ref, m_scr, l_scr, acc_scr, *, n_heads, dh):
    qi = pl.program_id(1)
    ki = pl.program_id(2)
    tq = q_ref.shape[0]
    tk = k_ref.shape[0]

    @pl.when(ki == 0)
    def _():
        m_scr[...] = jnp.full(m_scr.shape, NEG_INF, F32)
        l_scr[...] = jnp.zeros(l_scr.shape, F32)
        acc_scr[...] = jnp.zeros(acc_scr.shape, F32)

    def step(masked):
        if masked:
            keep = lax.broadcasted_iota(I32, (tq, tk), 1) <= lax.broadcasted_iota(I32, (tq, tk), 0)
        for h in range(n_heads):
            hs = slice(h * dh, (h + 1) * dh)
            s = _dot_nt(q_ref[:, hs], k_ref[:, hs]) - fk_ref[h:h + 1, :]
            if masked:
                s = jnp.where(keep, s, NEG_INF)
            m_old = m_scr[h]
            m_new = jnp.maximum(m_old, jnp.max(s, axis=-1, keepdims=True))
            alpha = jnp.exp(m_old - m_new)
            p = jnp.exp(s - m_new)
            l_scr[h] = alpha * l_scr[h] + jnp.sum(p, axis=-1, keepdims=True)
            acc_scr[:, hs] = alpha * acc_scr[:, hs] + _dot_nn(p.astype(BF16), v_ref[:, hs])
            m_scr[h] = m_new

    @pl.when(ki < qi)
    def _():
        step(False)

    @pl.when(ki == qi)
    def _():
        step(True)

    @pl.when(ki == pl.num_programs(2) - 1)
    def _():
        for h in range(n_heads):
            hs = slice(h * dh, (h + 1) * dh)
            o_ref[:, hs] = acc_scr[:, hs] / l_scr[h]


def _fox_prompt(qf, kfb, vfb, f_row, batch, seq, n_heads, dh, tq):
    nq = seq // tq
    width = n_heads * dh
    kv_map = lambda b, qi, ki: (b * nq + jnp.minimum(ki, qi), 0)
    return pl.pallas_call(
        functools.partial(_fox_prompt_body, n_heads=n_heads, dh=dh),
        grid=(batch, nq, nq),
        in_specs=[pl.BlockSpec((tq, width), lambda b, qi, ki: (b * nq + qi, 0)),
                  pl.BlockSpec((tq, width), kv_map),
                  pl.BlockSpec((tq, width), kv_map),
                  pl.BlockSpec((None, n_heads, tq), lambda b, qi, ki: (b, 0, jnp.minimum(ki, qi)))],
        out_specs=pl.BlockSpec((tq, width), lambda b, qi, ki: (b * nq + qi, 0)),
        out_shape=jax.ShapeDtypeStruct((batch * seq, width), F32),
        scratch_shapes=[pltpu.VMEM((n_heads, tq, 1), F32), pltpu.VMEM((n_heads, tq, 1), F32),
                        pltpu.VMEM((tq, width), F32)],
        compiler_params=_cparams(("arbitrary", "arbitrary", "arbitrary")),
        name="fox_prompt",
    )(qf, kfb, vfb, f_row)


def _fox_sample_body(pt_ref, q_ref, kn_ref, vn_ref, lfn_ref, *rest, n_heads, pages_per_step):
    P = pages_per_step
    k_refs = rest[:P]
    v_refs = rest[P:2 * P]
    lf_refs = rest[2 * P:3 * P]
    o_ref, m_scr, l_scr, acc_scr, carry_scr = rest[3 * P:]
    j = pl.program_id(1)
    rows = q_ref.shape[0]
    tiles = lf_refs[0].shape[0]
    q = q_ref[...]
    lane = lax.broadcasted_iota(I32, (rows, LANES), 1)
    rowi = lax.broadcasted_iota(I32, (rows, LANES), 0)
    same_head = (lane % n_heads) == (rowi % n_heads)
    l_i = lax.broadcasted_iota(I32, (LANES, LANES), 0)
    l_j = lax.broadcasted_iota(I32, (LANES, LANES), 1)
    head_eq = (l_i % n_heads) == (l_j % n_heads)

    @pl.when(j == 0)
    def _():
        fn = _dot_exact_rhs01(lfn_ref[...], (head_eq & (l_i <= l_j)).astype(BF16))[0:1, :]
        s = _dot_nt(q, kn_ref[...]) - fn
        keep = same_head & (lane // n_heads <= rowi // n_heads) & (lane < rows)
        s = jnp.where(keep, s, NEG_INF)
        m = jnp.max(s, axis=-1, keepdims=True)
        p = jnp.exp(s - m)
        m_scr[...] = m
        l_scr[...] = jnp.sum(p, axis=-1, keepdims=True)
        acc_scr[...] = _dot_nn(p.astype(BF16), vn_ref[...])
        carry_scr[...] = jnp.zeros(carry_scr.shape, F32)

    x = jnp.concatenate([lf_refs[i][...] for i in range(P)], axis=0)
    within = _dot_exact_rhs01(x, (head_eq & (l_i > l_j)).astype(BF16))
    tot = _dot_exact_rhs01(x, head_eq.astype(BF16))
    nr = P * tiles
    r_i = lax.broadcasted_iota(I32, (nr, nr), 0)
    c_i = lax.broadcasted_iota(I32, (nr, nr), 1)
    later_rows = ((c_i // tiles < r_i // tiles) | ((c_i // tiles == r_i // tiles) & (c_i > r_i))).astype(BF16)
    carry = carry_scr[...]
    suf = within + _dot_exact_lhs01(later_rows, tot) + carry
    carry_scr[...] = carry + jnp.sum(tot, axis=0, keepdims=True)

    k2 = jnp.concatenate([k_refs[i][...].reshape(-1, LANES) for i in range(P)], axis=0).astype(BF16)
    v2 = jnp.concatenate([v_refs[i][...].reshape(-1, LANES) for i in range(P)], axis=0).astype(BF16)
    s = _dot_nt(q, k2)
    blocks = [jnp.where(same_head, s[:, c * LANES:(c + 1) * LANES] + suf[c:c + 1, :], NEG_INF) for c in range(nr)]
    blk_max = blocks[0]
    for blk in blocks[1:]:
        blk_max = jnp.maximum(blk_max, blk)
    m_old = m_scr[...]
    m_new = jnp.maximum(m_old, jnp.max(blk_max, axis=-1, keepdims=True))
    alpha = jnp.exp(m_old - m_new)
    probs = [jnp.exp(blk - m_new) for blk in blocks]
    psum = probs[0]
    for pb in probs[1:]:
        psum = psum + pb
    l_scr[...] = alpha * l_scr[...] + jnp.sum(psum, axis=-1, keepdims=True)
    p_all = jnp.concatenate([pb.astype(BF16) for pb in probs], axis=1)
    acc_scr[...] = alpha * acc_scr[...] + _dot_nn(p_all, v2)
    m_scr[...] = m_new

    @pl.when(j == pl.num_programs(1) - 1)
    def _():
        o_ref[...] = acc_scr[...] / l_scr[...]


def _fox_sample_call(page_table, qf, kf, vf, logf, cache_k, cache_v, cache_logf, layer, db, t_new, n_heads, dh,
                     pages_per_step):
    assert dh == LANES
    rows = t_new * n_heads
    n_pool, page = cache_k.shape[1:3]
    n_pages = page_table.shape[1]
    tiles = page * n_heads // LANES
    P = pages_per_step
    steps = n_pages // P
    pad_rows = lambda a: jnp.pad(a.reshape(db, rows, dh), ((0, 0), (0, LANES - rows), (0, 0))).astype(BF16)
    q2 = qf.reshape(db, rows, dh).astype(BF16)
    lfn = jnp.pad(logf.reshape(db, 1, rows), ((0, 0), (0, 7), (0, LANES - rows)))
    lf_pages = cache_logf.reshape(cache_logf.shape[0], n_pool, tiles, LANES)

    def page_idx(b, j, pt, i):
        return pt[b, n_pages - 1 - (j * P + i)]

    def kv_spec(i):
        return pl.BlockSpec((None, None, page, n_heads, dh),
                            lambda b, j, pt: (layer, page_idx(b, j, pt, i), 0, 0, 0))

    def lf_spec(i):
        return pl.BlockSpec((None, None, tiles, LANES), lambda b, j, pt: (layer, page_idx(b, j, pt, i), 0, 0))

    per_seq = lambda r: pl.BlockSpec((None, r, LANES), lambda b, j, pt: (b, 0, 0))
    in_specs = ([per_seq(rows), per_seq(LANES), per_seq(LANES), per_seq(8)]
                + [kv_spec(i) for i in range(P)] + [kv_spec(i) for i in range(P)]
                + [lf_spec(i) for i in range(P)])
    out = pl.pallas_call(
        functools.partial(_fox_sample_body, n_heads=n_heads, pages_per_step=P),
        grid_spec=pltpu.PrefetchScalarGridSpec(
            num_scalar_prefetch=1,
            grid=(db, steps),
            in_specs=in_specs,
            out_specs=per_seq(rows),
            scratch_shapes=[pltpu.VMEM((rows, 1), F32), pltpu.VMEM((rows, 1), F32),
                            pltpu.VMEM((rows, LANES), F32), pltpu.VMEM((1, LANES), F32)]),
        out_shape=jax.ShapeDtypeStruct((db, rows, LANES), F32),
        compiler_params=_cparams(("arbitrary", "arbitrary")),
        name="fox_sample",
    )(page_table, q2, pad_rows(kf), pad_rows(vf), lfn, *([cache_k] * P), *([cache_v] * P), *([lf_pages] * P))
    return out.reshape(db * t_new, n_heads * dh)


def _rms_heads(x, n_heads, dh):
    outs = []
    for h in range(n_heads):
        xs = x[:, h * dh:(h + 1) * dh]
        outs.append(xs * lax.rsqrt(jnp.mean(xs * xs, axis=-1, keepdims=True) + EPS))
    return jnp.concatenate(outs, axis=1)


def _route(logits):
    lane = lax.broadcasted_iota(I32, logits.shape, 1)
    big = jnp.int32(LANES)
    gl = jnp.where(lane < N_GROUPS, logits, NEG_INF)
    gmax = jnp.max(gl, axis=-1, keepdims=True)
    g_sel = jnp.min(jnp.where(gl == gmax, lane, big), axis=-1, keepdims=True)
    p_g = 1.0 / jnp.sum(jnp.exp(gl - gmax), axis=-1, keepdims=True)
    lo = N_GROUPS + E_PER_GROUP * g_sel
    ev = jnp.where((lane >= lo) & (lane < lo + E_PER_GROUP), logits, NEG_INF)
    v1 = jnp.max(ev, axis=-1, keepdims=True)
    i1 = jnp.min(jnp.where(ev == v1, lane, big), axis=-1, keepdims=True)
    ev2 = jnp.where(lane == i1, NEG_INF, ev)
    v2 = jnp.max(ev2, axis=-1, keepdims=True)
    i2 = jnp.min(jnp.where(ev2 == v2, lane, big), axis=-1, keepdims=True)
    e21 = jnp.exp(v2 - v1)
    w1 = p_g / (1.0 + e21)
    w2 = p_g * e21 / (1.0 + e21)
    out = jnp.where(lane == 0, (i1 - N_GROUPS).astype(F32), 0.0)
    out = jnp.where(lane == 1, (i2 - N_GROUPS).astype(F32), out)
    out = jnp.where(lane == 2, w1, out)
    out = jnp.where(lane == 3, w2, out)
    return out


def _merge_body(x_ref, og_ref, gg_ref, of_ref, gw_ref, fw_ref, wo_ref, g1_ref, sh2_ref, sc2_ref, n2_ref,
                wr_hi_ref, wr_lo_ref, br_ref, x1_ref, h2_ref, route_ref, *, n_heads, dh):
    og = _rms_heads(og_ref[...], n_heads, dh) * gw_ref[...] * _silu(gg_ref[...])
    of = _rms_heads(of_ref[...], n_heads, dh) * fw_ref[...]
    merged = jnp.concatenate([og, of], axis=1).astype(BF16)
    x1 = x_ref[...] + g1_ref[...] * _dot_nn(merged, wo_ref[...])
    x1_ref[...] = x1
    y = x1 * lax.rsqrt(jnp.mean(x1 * x1, axis=-1, keepdims=True) + EPS) * n2_ref[...]
    h2 = y * (1.0 + sc2_ref[...]) + sh2_ref[...]
    h2_ref[...] = h2
    hh, hl, _ = _split3(h2)
    logits = (_dot_nn(hh, wr_hi_ref[...]) + _dot_nn(hl, wr_hi_ref[...]) + _dot_nn(hh, wr_lo_ref[...])
              + br_ref[...])
    route_ref[...] = _route(logits)


def _merge(x, o_gla, gla_in, o_fox, gla_w, fox_w, w_out_bf, g1, sh2, sc2, norm2_w, wr_hi, wr_lo, b_r,
           tm, rows_per_mod, n_heads, dh):
    n, d = x.shape
    half = n_heads * dh
    per_token = rows_per_mod == 1
    if per_token:
        mod_spec = pl.BlockSpec((tm, d), lambda i: (i, 0))
    else:
        tiles_per_mod = rows_per_mod // tm
        g1, sh2, sc2 = (a.reshape(a.shape[0], 1, d) for a in (g1, sh2, sc2))
        mod_spec = pl.BlockSpec((None, 1, d), lambda i: (i // tiles_per_mod, 0, 0))
    row = lambda i: (i, 0)
    fixed = lambda i: (0, 0)
    return pl.pallas_call(
        functools.partial(_merge_body, n_heads=n_heads, dh=dh),
        grid=(n // tm,),
        in_specs=[pl.BlockSpec((tm, d), row),
                  pl.BlockSpec((tm, half), row),
                  pl.BlockSpec((tm, half), lambda i: (i, 2)),
                  pl.BlockSpec((tm, half), row),
                  pl.BlockSpec((1, half), fixed), pl.BlockSpec((1, half), fixed),
                  pl.BlockSpec((d, d), fixed),
                  mod_spec, mod_spec, mod_spec,
                  pl.BlockSpec((1, d), fixed),
                  pl.BlockSpec((d, LANES), fixed), pl.BlockSpec((d, LANES), fixed),
                  pl.BlockSpec((1, LANES), fixed)],
        out_specs=(pl.BlockSpec((tm, d), row), pl.BlockSpec((tm, d), row), pl.BlockSpec((tm, LANES), row)),
        out_shape=(jax.ShapeDtypeStruct((n, d), F32), jax.ShapeDtypeStruct((n, d), F32),
                   jax.ShapeDtypeStruct((n, LANES), F32)),
        compiler_params=_cparams(("arbitrary",)),
        name="merge_outproj",
    )(x, o_gla, gla_in, o_fox, gla_w.reshape(1, half), fox_w.reshape(1, half), w_out_bf, g1, sh2, sc2,
      norm2_w.reshape(1, d), wr_hi, wr_lo, b_r)


GATHER_UNROLL = 8


def _experts_body(te_ref, src_ref, nt_ref, h_hbm, w_ref, wg_ref, wu_ref, wd_ref, y_ref,
                  xbuf, wg_bf, wu_bf, wd_bf, sem, *, tm):
    i = pl.program_id(0)
    n_active = nt_ref[0]

    def gather(tile, slot):
        def issue(g, carry):
            for u in range(GATHER_UNROLL):
                r = g * GATHER_UNROLL + u
                src = src_ref[tile * tm + r]
                pltpu.make_async_copy(h_hbm.at[pl.ds(src, 1), :], xbuf.at[slot, pl.ds(r, 1), :],
                                      sem.at[slot]).start()
            return carry
        lax.fori_loop(0, tm // GATHER_UNROLL, issue, 0)

    @pl.when((i == 0) & (n_active > 0))
    def _():
        gather(0, 0)

    @pl.when(i + 1 < n_active)
    def _():
        gather(i + 1, (i + 1) % 2)

    @pl.when(i < n_active)
    def _():
        slot = i % 2
        prev = te_ref[jnp.maximum(i - 1, 0)]

        @pl.when((i == 0) | (te_ref[i] != prev))
        def _():
            wg_bf[...] = wg_ref[...].astype(BF16)
            wu_bf[...] = wu_ref[...].astype(BF16)
            wd_bf[...] = wd_ref[...].astype(BF16)

        pltpu.make_async_copy(h_hbm.at[pl.ds(0, tm), :], xbuf.at[slot], sem.at[slot]).wait()
        x = xbuf[slot].astype(BF16)
        a = _silu(_dot_nn(x, wg_bf[...])) * _dot_nn(x, wu_bf[...])
        a = (a * w_ref[...]).astype(BF16)
        y_ref[...] = _dot_nn(a, wd_bf[...])

    @pl.when(i >= n_active)
    def _():
        y_ref[...] = jnp.zeros(y_ref.shape, F32)


def _experts(tile_expert, src_row, n_tiles, h2, w_sorted, w_gate, w_up, w_down, tm):
    n_slots = src_row.shape[0]
    max_tiles = n_slots // tm
    _, d, dff = w_gate.shape
    wmap = lambda i, te, src, nt: (te[i], 0, 0)
    return pl.pallas_call(
        functools.partial(_experts_body, tm=tm),
        grid_spec=pltpu.PrefetchScalarGridSpec(
            num_scalar_prefetch=3,
            grid=(max_tiles,),
            in_specs=[pl.BlockSpec(memory_space=pl.ANY),
                      pl.BlockSpec((tm, 1), lambda i, te, src, nt: (i, 0)),
                      pl.BlockSpec((None, d, dff), wmap),
                      pl.BlockSpec((None, d, dff), wmap),
                      pl.BlockSpec((None, dff, d), wmap)],
            out_specs=pl.BlockSpec((tm, d), lambda i, te, src, nt: (i, 0)),
            scratch_shapes=[pltpu.VMEM((2, tm, d), F32),
                            pltpu.VMEM((d, dff), BF16), pltpu.VMEM((d, dff), BF16), pltpu.VMEM((dff, d), BF16),
                            pltpu.SemaphoreType.DMA((2,))]),
        out_shape=jax.ShapeDtypeStruct((n_slots, d), F32),
        compiler_params=_cparams(("arbitrary",)),
        name="experts",
    )(tile_expert, src_row, n_tiles, h2, w_sorted, w_gate, w_up, w_down)


def _combine_body(pos_ref, y_hbm, x1_ref, g2_ref, fw_ref, o_ref, buf, sem, *, tm, row0):
    i = pl.program_id(0)

    def gather(tile, slot):
        def issue(g, carry):
            for u in range(GATHER_UNROLL // 2):
                r = g * (GATHER_UNROLL // 2) + u
                tok = row0 + tile * tm + r
                for c in range(2):
                    pltpu.make_async_copy(y_hbm.at[pl.ds(pos_ref[2 * tok + c], 1), :],
                                          buf.at[slot, c, pl.ds(r, 1), :], sem.at[slot]).start()
            return carry
        lax.fori_loop(0, tm // (GATHER_UNROLL // 2), issue, 0)

    @pl.when(i == 0)
    def _():
        gather(0, 0)

    @pl.when(i + 1 < pl.num_programs(0))
    def _():
        gather(i + 1, (i + 1) % 2)

    slot = i % 2
    for c in range(2):
        pltpu.make_async_copy(y_hbm.at[pl.ds(0, tm), :], buf.at[slot, c], sem.at[slot]).wait()
    x2 = x1_ref[...] + g2_ref[...] * (buf[slot, 0] + buf[slot, 1])
    o_ref[...] = x2 * lax.rsqrt(jnp.mean(x2 * x2, axis=-1, keepdims=True) + EPS) * fw_ref[...]


def _combine(pos, y_sorted, x1, g2, final_w, row0, n_rows, tm, rows_per_mod):
    d = x1.shape[1]
    per_token = rows_per_mod == 1
    tile0 = row0 // tm
    if per_token:
        mod_spec = pl.BlockSpec((tm, d), lambda i, pos: (i, 0))
    else:
        tiles_per_mod = rows_per_mod // tm
        g2 = g2.reshape(g2.shape[0], 1, d)
        mod_spec = pl.BlockSpec((None, 1, d), lambda i, pos: (i // tiles_per_mod, 0, 0))
    return pl.pallas_call(
        functools.partial(_combine_body, tm=tm, row0=row0),
        grid_spec=pltpu.PrefetchScalarGridSpec(
            num_scalar_prefetch=1,
            grid=(n_rows // tm,),
            in_specs=[pl.BlockSpec(memory_space=pl.ANY),
                      pl.BlockSpec((tm, d), lambda i, pos: (tile0 + i, 0)),
                      mod_spec,
                      pl.BlockSpec((1, d), lambda i, pos: (0, 0))],
            out_specs=pl.BlockSpec((tm, d), lambda i, pos: (i, 0)),
            scratch_shapes=[pltpu.VMEM((2, 2, tm, d), F32), pltpu.SemaphoreType.DMA((2,))]),
        out_shape=jax.ShapeDtypeStruct((n_rows, d), F32),
        compiler_params=_cparams(("arbitrary",)),
        name="combine_norm",
    )(pos, y_sorted, x1, g2, final_w.reshape(1, d))


def _plan(route, n_experts, tm):
    n = route.shape[0]
    e_flat = route[:, 0:2].astype(I32).reshape(-1)
    w_flat = route[:, 2:4].reshape(-1)
    onehot = (e_flat[:, None] == jnp.arange(n_experts, dtype=I32)[None, :]).astype(I32)
    csum = jnp.cumsum(onehot, axis=0)
    rank = jnp.take_along_axis(csum, e_flat[:, None], axis=1)[:, 0] - 1
    counts = csum[-1]
    tiles_e = (counts + tm - 1) // tm
    tile_end = jnp.cumsum(tiles_e)
    tile_off = tile_end - tiles_e
    n_tiles = tile_end[-1]
    max_tiles = (2 * n) // tm + n_experts
    dest = tile_off[e_flat] * tm + rank
    token = jnp.arange(2 * n, dtype=I32) // 2
    src_row = jnp.zeros((max_tiles * tm,), I32).at[dest].set(token)
    w_sorted = jnp.zeros((max_tiles * tm,), F32).at[dest].set(w_flat)
    tile_ids = jnp.arange(max_tiles, dtype=I32)
    tile_expert = jnp.minimum(jnp.sum((tile_end[None, :] <= tile_ids[:, None]).astype(I32), axis=1), n_experts - 1)
    return tile_expert, src_row, n_tiles.reshape(1).astype(I32), w_sorted.reshape(-1, 1), dest.astype(I32)


def kernel(x_prompt, x_sample, cache_k, cache_v, cache_logf, state_gla, page_table, c_prompt, c_sample,
           norm1_w, norm2_w, w_ada, b_ada, w_in, w_gk2, b_gk, b_fgate, gla_onorm_w, fox_onorm_w, w_out,
           w_rg, b_rg, w_re, b_re, w_gate_e, w_up_e, w_down_e, final_norm_w):
    depth = w_in.shape[0]
    assert depth == 1, "single-layer trunk"
    batch, seq, d = x_prompt.shape
    db, t_new, _ = x_sample.shape
    n_hg, dk, dv = state_gla.shape[2:]
    n_hf, dh = cache_k.shape[3:]
    rank = w_gk2.shape[1]
    n_experts = w_gate_e.shape[1]
    wq = n_hg * dk
    wv = n_hg * dv
    wf = n_hf * dh
    assert wv == 1024 and wf == 1024 and 2 * wq == 1024 and rank == 16 and n_hf == 8
    n_p = batch * seq
    n_s = db * t_new
    layer = 0

    wi = w_in[layer]
    o_alr = 2 * wq + 2 * wv
    o_fox = o_alr + rank
    o_fl = o_fox + 3 * wf
    w_main = jnp.concatenate([wi[:, :o_alr], wi[:, o_fox:o_fl]], axis=1).astype(BF16)
    w_small = jnp.concatenate([wi[:, o_alr:o_fox], wi[:, o_fl:], jnp.zeros((d, LANES - rank - n_hf), F32)],
                              axis=1).astype(BF16)
    b_small = jnp.zeros((1, LANES), F32).at[0, rank:rank + n_hf].set(b_fgate[layer])
    w_gk2_pad = jnp.zeros((LANES, wq), F32).at[:rank].set(w_gk2[layer])
    b_gk2 = b_gk[layer].reshape(1, wq)
    w_out_bf = w_out[layer].astype(BF16)
    w_r = jnp.concatenate([w_rg[layer], w_re[layer],
                           jnp.zeros((d, LANES - N_GROUPS - n_experts), F32)], axis=1)
    wr_hi = w_r.astype(BF16)
    wr_lo = (w_r - wr_hi.astype(F32)).astype(BF16)
    b_r = jnp.concatenate([b_rg[layer], b_re[layer], jnp.zeros((LANES - N_GROUPS - n_experts,), F32)]).reshape(1, LANES)

    n_c = batch + db
    c_all = jnp.concatenate([c_prompt, c_sample, jnp.zeros((-n_c % 8, d), F32)], axis=0)
    mod = _adaln(c_all, w_ada[layer], b_ada[layer])
    sh1, sc1, g1, sh2, sc2, g2 = (mod[:, i * d:(i + 1) * d] for i in range(6))
    p_rows = slice(0, batch)
    rep = lambda a: jnp.repeat(a[batch:n_c], t_new, axis=0)

    q_scale = dh ** -0.5
    xp = x_prompt.reshape(n_p, d)
    xs = x_sample.reshape(n_s, d)

    tm_p = 512
    gla_p, qf_p, kf_p, vf_p, kfb_p, vfb_p, small_p = _inproj(
        xp, sh1[p_rows], sc1[p_rows], norm1_w[layer], w_main, w_small, b_small, tm_p, seq, q_scale)
    s0_p = jnp.zeros((batch, n_hg, dk, dv), F32)
    o_gla_p, gla_state_p = _gla(gla_p, small_p, w_gk2_pad, b_gk2, s0_p, batch, seq, 128, n_hg, dk, dv)
    logf_p = small_p[:, rank:rank + n_hf]
    f_row = _cumsum_lanes(logf_p.reshape(batch, seq, n_hf).transpose(0, 2, 1))
    o_fox_p = _fox_prompt(qf_p, kfb_p, vfb_p, f_row, batch, seq, n_hf, dh, 512)
    x1_p, h2_p, route_p = _merge(xp, o_gla_p, gla_p, o_fox_p, gla_onorm_w[layer], fox_onorm_w[layer], w_out_bf,
                                 g1[p_rows], sh2[p_rows], sc2[p_rows], norm2_w[layer], wr_hi, wr_lo, b_r,
                                 256, seq, n_hg, dv)

    gla_s, qf_s, kf_s, vf_s, _, _, small_s = _inproj(
        xs, rep(sh1), rep(sc1), norm1_w[layer], w_main, w_small, b_small, n_s, 1, q_scale)
    t_pad = GLA_CHUNK
    pad_t = lambda a: jnp.pad(a.reshape(db, t_new, -1), ((0, 0), (0, t_pad - t_new), (0, 0))).reshape(db * t_pad, -1)
    o_gla_s_pad, gla_state_s = _gla(pad_t(gla_s), pad_t(small_s), w_gk2_pad, b_gk2, state_gla[layer],
                                    db, t_pad, t_pad, n_hg, dk, dv, t_valid=t_new)
    o_gla_s = o_gla_s_pad.reshape(db, t_pad, wv)[:, :t_new].reshape(n_s, wv)

    logf_s = small_s[:, rank:rank + n_hf]
    o_fox_s = _fox_sample_call(page_table, qf_s, kf_s, vf_s, logf_s, cache_k, cache_v, cache_logf,
                               layer, db, t_new, n_hf, dh, 8)
    x1_s, h2_s, route_s = _merge(xs, o_gla_s, gla_s, o_fox_s, gla_onorm_w[layer], fox_onorm_w[layer], w_out_bf,
                                 rep(g1), rep(sh2), rep(sc2), norm2_w[layer], wr_hi, wr_lo, b_r,
                                 n_s, 1, n_hg, dv)

    tm_e = 256
    x1 = jnp.concatenate([x1_p, x1_s], axis=0)
    h2 = jnp.concatenate([h2_p, h2_s], axis=0)
    route = jnp.concatenate([route_p, route_s], axis=0)
    tile_expert, src_row, n_tiles, w_sorted, pos = _plan(route, n_experts, tm_e)
    y_sorted = _experts(tile_expert, src_row, n_tiles, h2, w_sorted, w_gate_e[layer], w_up_e[layer],
                        w_down_e[layer], tm_e)
    y_p = _combine(pos, y_sorted, x1, g2[p_rows], final_norm_w, 0, n_p, 256, seq)
    y_s = _combine(pos, y_sorted, x1, rep(g2), final_norm_w, n_p, n_s, n_s, 1)

    y_prompt = y_p.reshape(batch, seq, d)
    y_sample = y_s.reshape(db, t_new, d)
    k_prompt = kf_p.reshape(1, batch, seq, n_hf, dh)
    v_prompt = vf_p.reshape(1, batch, seq, n_hf, dh)
    logf_prompt = logf_p.reshape(1, batch, seq, n_hf)
    k_sample = kf_s.reshape(1, db, t_new, n_hf, dh)
    v_sample = vf_s.reshape(1, db, t_new, n_hf, dh)
    logf_sample = logf_s.reshape(1, db, t_new, n_hf)
    return (y_prompt, y_sample, k_prompt, v_prompt, logf_prompt, gla_state_p[None],
            k_sample, v_sample, logf_sample, gla_state_s[None])
```

```python
import functools

import jax
import jax.numpy as jnp
from jax import lax
from jax.experimental import pallas as pl
from jax.experimental.pallas import tpu as pltpu

F32 = jnp.float32
BF16 = jnp.bfloat16
I32 = jnp.int32
EPS = 1e-6
NEG_INF = float("-inf")

LANES = 128
GLA_CHUNK = 16
GLA_GATE_NORM = 16.0
N_GROUPS = 4
E_PER_GROUP = 8
VMEM_LIMIT = 56 * 1024 * 1024


def _cparams(sem):
    return pltpu.CompilerParams(dimension_semantics=sem, vmem_limit_bytes=VMEM_LIMIT)


def _log_sigmoid(z):
    return jnp.minimum(z, 0.0) - jnp.log1p(jnp.exp(-jnp.abs(z)))


def _silu(z):
    return z * jax.nn.sigmoid(z)


def _split3(a):
    hi = a.astype(BF16)
    r = a - hi.astype(F32)
    mid = r.astype(BF16)
    lo = (r - mid.astype(F32)).astype(BF16)
    return hi, mid, lo


def _dot_nn(a, b):
    return jnp.dot(a, b, preferred_element_type=F32)


def _dot_nt(a, b):
    return lax.dot_general(a, b, (((1,), (1,)), ((), ())), preferred_element_type=F32)


def _dot_f32(a, b):
    ah, al, _ = _split3(a)
    bh, bl, _ = _split3(b)
    return _dot_nn(ah, bh) + _dot_nn(al, bh) + _dot_nn(ah, bl)


def _dot_exact_rhs01(a, ones_bf16):
    hi, mid, lo = _split3(a)
    return _dot_nn(hi, ones_bf16) + _dot_nn(mid, ones_bf16) + _dot_nn(lo, ones_bf16)


def _dot_exact_lhs01(ones_bf16, b):
    hi, mid, lo = _split3(b)
    return _dot_nn(ones_bf16, hi) + _dot_nn(ones_bf16, mid) + _dot_nn(ones_bf16, lo)


def _adaln_body(c_ref, w_ref, b_ref, o_ref):
    s = _silu(c_ref[...]).astype(BF16)
    o_ref[...] = _dot_nn(s, w_ref[...].astype(BF16)) + b_ref[...]


def _adaln(c_all, w_ada, b_ada, tn=1024):
    rows, d = c_all.shape
    n6 = w_ada.shape[1]
    return pl.pallas_call(
        _adaln_body,
        grid=(n6 // tn,),
        in_specs=[pl.BlockSpec((rows, d), lambda j: (0, 0)),
                  pl.BlockSpec((d, tn), lambda j: (0, j)),
                  pl.BlockSpec((1, tn), lambda j: (0, j))],
        out_specs=pl.BlockSpec((rows, tn), lambda j: (0, j)),
        out_shape=jax.ShapeDtypeStruct((rows, n6), F32),
        compiler_params=_cparams(("arbitrary",)),
        name="adaln",
    )(c_all, w_ada, b_ada.reshape(1, n6))


def _inproj_body(x_ref, sh_ref, sc_ref, nw_ref, w_ref, ws_ref, bs_ref,
                 gla_ref, qf_ref, kf_ref, vf_ref, kfb_ref, vfb_ref, small_ref, h_scr, *, q_scale):
    j = pl.program_id(1)

    @pl.when(j == 0)
    def _():
        x = x_ref[...]
        y = x * lax.rsqrt(jnp.mean(x * x, axis=-1, keepdims=True) + EPS) * nw_ref[...]
        hb = (y * (1.0 + sc_ref[...]) + sh_ref[...]).astype(BF16)
        h_scr[...] = hb
        sm = _dot_nn(hb, ws_ref[...])
        lane = lax.broadcasted_iota(I32, sm.shape, 1)
        small_ref[...] = jnp.where((lane >= 16) & (lane < 24), _log_sigmoid(sm + bs_ref[...]), sm)

    acc = _dot_nn(h_scr[...], w_ref[...])

    @pl.when(j < 3)
    def _():
        gla_ref[...] = acc

    @pl.when(j == 3)
    def _():
        qf_ref[...] = (acc * q_scale).astype(BF16)

    @pl.when(j == 4)
    def _():
        kf_ref[...] = acc
        kfb_ref[...] = acc.astype(BF16)

    @pl.when(j == 5)
    def _():
        vf_ref[...] = acc
        vfb_ref[...] = acc.astype(BF16)


def _inproj(x, sh, sc, norm_w, w_main, w_small, b_small, tm, rows_per_mod, q_scale):
    n, d = x.shape
    tn = 1024
    per_token = rows_per_mod == 1
    if per_token:
        mod_spec = pl.BlockSpec((tm, d), lambda i, j: (i, 0))
    else:
        tiles_per_mod = rows_per_mod // tm
        sh = sh.reshape(sh.shape[0], 1, d)
        sc = sc.reshape(sc.shape[0], 1, d)
        mod_spec = pl.BlockSpec((None, 1, d), lambda i, j: (i // tiles_per_mod, 0, 0))
    row = lambda i, j: (i, 0)
    out_shapes = (jax.ShapeDtypeStruct((n, 3 * tn), F32),
                  jax.ShapeDtypeStruct((n, tn), BF16),
                  jax.ShapeDtypeStruct((n, tn), F32),
                  jax.ShapeDtypeStruct((n, tn), F32),
                  jax.ShapeDtypeStruct((n, tn), BF16),
                  jax.ShapeDtypeStruct((n, tn), BF16),
                  jax.ShapeDtypeStruct((n, LANES), F32))
    return pl.pallas_call(
        functools.partial(_inproj_body, q_scale=q_scale),
        grid=(n // tm, 6),
        in_specs=[pl.BlockSpec((tm, d), row), mod_spec, mod_spec,
                  pl.BlockSpec((1, d), lambda i, j: (0, 0)),
                  pl.BlockSpec((d, tn), lambda i, j: (0, j)),
                  pl.BlockSpec((d, LANES), lambda i, j: (0, 0)),
                  pl.BlockSpec((1, LANES), lambda i, j: (0, 0))],
        out_specs=(pl.BlockSpec((tm, tn), lambda i, j: (i, jnp.minimum(j, 2))),
                   pl.BlockSpec((tm, tn), row), pl.BlockSpec((tm, tn), row), pl.BlockSpec((tm, tn), row),
                   pl.BlockSpec((tm, tn), row), pl.BlockSpec((tm, tn), row),
                   pl.BlockSpec((tm, LANES), row)),
        out_shape=out_shapes,
        scratch_shapes=[pltpu.VMEM((tm, d), BF16)],
        compiler_params=_cparams(("arbitrary", "arbitrary")),
        name="inproj",
    )(x, sh, sc, norm_w.reshape(1, d), w_main, w_small, b_small)


def _level_ref(b, half, rows_i):
    tt, n = b.shape
    if half >= 4:
        size = 2 * half
        blocks = [jnp.broadcast_to(b[g * size + half - 1:g * size + half, :], (size, n)) for g in range(tt // size)]
        return blocks[0] if len(blocks) == 1 else jnp.concatenate(blocks, axis=0)
    if half == 2:
        lo = jnp.concatenate([jnp.broadcast_to(b[8 * g + 1:8 * g + 2, :], (8, n)) for g in range(tt // 8)], axis=0)
        hi = jnp.concatenate([jnp.broadcast_to(b[8 * g + 5:8 * g + 6, :], (8, n)) for g in range(tt // 8)], axis=0)
        return jnp.where(rows_i % 8 < 4, lo, hi)
    return jnp.where(rows_i % 2 == 1, pltpu.roll(b, 1, 0), b)


def _gla_body(q_ref, k_ref, v_ref, sm_ref, w2_ref, b2_ref, s0_ref, o_ref, sout_ref, st_scr,
              *, tt, dk, dv, n_pairs, t_valid):
    t = pl.program_id(1)
    zero_blk = jnp.zeros((dv, dk), F32)

    @pl.when(t == 0)
    def _():
        for p in range(n_pairs):
            top = jnp.concatenate([s0_ref[2 * p].T, zero_blk], axis=1)
            bot = jnp.concatenate([zero_blk, s0_ref[2 * p + 1].T], axis=1)
            st_scr[p] = jnp.concatenate([top, bot], axis=0)

    gk = _log_sigmoid(_dot_f32(sm_ref[...], w2_ref[...]) + b2_ref[...]) / GLA_GATE_NORM
    rows_i = lax.broadcasted_iota(I32, (tt, 1), 0)
    if t_valid < tt:
        gk = jnp.where(rows_i < t_valid, gk, 0.0)
    r_i = lax.broadcasted_iota(I32, (tt, tt), 0)
    c_i = lax.broadcasted_iota(I32, (tt, tt), 1)
    b_all = _dot_exact_lhs01((c_i <= r_i).astype(BF16), gk)

    halves = []
    half = tt // 2
    while half >= 1:
        halves.append(half)
        half //= 2
    refs = [_level_ref(b_all, hf, rows_i) for hf in halves]
    valid = [(r_i // (2 * hf) == c_i // (2 * hf)) & (r_i % (2 * hf) >= hf) & (c_i % (2 * hf) < hf) for hf in halves]
    on_diag = r_i == c_i
    head0 = lax.broadcasted_iota(I32, (tt, 2 * dk), 1) < dk
    rr = lax.broadcasted_iota(I32, (2 * dv, 2 * dk), 0) // dv
    cc = lax.broadcasted_iota(I32, (2 * dv, 2 * dk), 1) // dk
    diag = (rr == cc).astype(F32)

    def head_scores(qx, kx):
        stacked = jnp.concatenate([jnp.where(head0, qx, 0.0), jnp.where(head0, 0.0, qx)], axis=0).astype(BF16)
        return _dot_nt(stacked, kx.astype(BF16))

    for p in range(n_pairs):
        cs = slice(p * 2 * dk, (p + 1) * 2 * dk)
        vs = slice(p * 2 * dv, (p + 1) * 2 * dv)
        q = q_ref[:, cs] * (dk ** -0.5)
        k = k_ref[:, cs]
        v = v_ref[:, vs]
        b = b_all[:, cs]
        sc = head_scores(q, k)
        a0 = jnp.where(on_diag, sc[:tt], 0.0)
        a1 = jnp.where(on_diag, sc[tt:], 0.0)
        for ref_all, ok in zip(refs, valid):
            ref = ref_all[:, cs]
            sc = head_scores(q * jnp.exp(jnp.minimum(b - ref, 0.0)), k * jnp.exp(jnp.minimum(ref - b, 0.0)))
            a0 = jnp.where(ok, sc[:tt], a0)
            a1 = jnp.where(ok, sc[tt:], a1)
        vb = v.astype(BF16)
        o_intra = jnp.concatenate([_dot_nn(a0.astype(BF16), vb[:, :dv]), _dot_nn(a1.astype(BF16), vb[:, dv:])], axis=1)

        st = st_scr[p]
        bend = b[tt - 1:tt]
        o_inter = _dot_nt((q * jnp.exp(b)).astype(BF16), st.astype(BF16))
        kdec = (k * jnp.exp(bend - b)).astype(BF16)
        st = st * jnp.exp(bend) + _dot_nn(v.T.astype(BF16), kdec) * diag
        st_scr[p] = st
        o_ref[:, vs] = o_intra + o_inter

    @pl.when(t == pl.num_programs(1) - 1)
    def _():
        for p in range(n_pairs):
            st = st_scr[p]
            sout_ref[2 * p] = st[:dv, :dk].T
            sout_ref[2 * p + 1] = st[dv:, dk:].T


def _gla(gla_in, small, w_gk2_pad, b_gk, s0, batch, seq, tt, n_heads, dk, dv, t_valid=None):
    n = batch * seq
    nt = seq // tt
    wq = n_heads * dk
    wv = n_heads * dv
    assert wv == 2 * wq
    row = lambda b, t: b * nt + t
    return pl.pallas_call(
        functools.partial(_gla_body, tt=tt, dk=dk, dv=dv, n_pairs=n_heads // 2,
                          t_valid=tt if t_valid is None else t_valid),
        grid=(batch, nt),
        in_specs=[pl.BlockSpec((tt, wq), lambda b, t: (row(b, t), 0)),
                  pl.BlockSpec((tt, wq), lambda b, t: (row(b, t), 1)),
                  pl.BlockSpec((tt, wv), lambda b, t: (row(b, t), 1)),
                  pl.BlockSpec((tt, LANES), lambda b, t: (row(b, t), 0)),
                  pl.BlockSpec((LANES, wq), lambda b, t: (0, 0)),
                  pl.BlockSpec((1, wq), lambda b, t: (0, 0)),
                  pl.BlockSpec((None, n_heads, dk, dv), lambda b, t: (b, 0, 0, 0))],
        out_specs=(pl.BlockSpec((tt, wv), lambda b, t: (row(b, t), 0)),
                   pl.BlockSpec((None, n_heads, dk, dv), lambda b, t: (b, 0, 0, 0))),
        out_shape=(jax.ShapeDtypeStruct((n, wv), F32),
                   jax.ShapeDtypeStruct((batch, n_heads, dk, dv), F32)),
        scratch_shapes=[pltpu.VMEM((n_heads // 2, 2 * dv, 2 * dk), F32)],
        compiler_params=_cparams(("arbitrary", "arbitrary")),
        name="gla",
    )(gla_in, gla_in, gla_in, small, w_gk2_pad, b_gk, s0)


LOG2E = 1.4426950408889634
BIAS_PARTS = 3


def _key_bias_body(sm_ref, o_ref, *, lane0, n_heads, blk):
    s = sm_ref.shape[0]
    r_i = lax.broadcasted_iota(I32, (blk, blk), 0)
    c_i = lax.broadcasted_iota(I32, (blk, blk), 1)
    lower = (c_i <= r_i).astype(BF16)
    head = r_i - lane0
    sel = [((head >= 0) & (head < n_heads) & (c_i == BIAS_PARTS * head + part)).astype(BF16)
           for part in range(BIAS_PARTS)]
    carry = jnp.zeros((1, sm_ref.shape[1]), F32)
    for i in range(s // blk):
        f = _dot_exact_lhs01(lower, sm_ref[i * blk:(i + 1) * blk, :]) + carry
        carry = f[blk - 1:blk, :]
        pieces = _split3(f * (-LOG2E))
        placed = sum(_dot_nn(piece, sel_p) for piece, sel_p in zip(pieces, sel))
        o_ref[i * blk:(i + 1) * blk, :] = placed.astype(BF16)


def _key_bias(small, batch, seq, lane0, n_heads):
    return pl.pallas_call(
        functools.partial(_key_bias_body, lane0=lane0, n_heads=n_heads, blk=LANES),
        grid=(batch,),
        in_specs=[pl.BlockSpec((seq, LANES), lambda i: (i, 0))],
        out_specs=pl.BlockSpec((seq, LANES), lambda i: (i, 0)),
        out_shape=jax.ShapeDtypeStruct((batch * seq, LANES), BF16),
        compiler_params=_cparams(("arbitrary",)),
        name="fox_key_bias",
    )(small)


def _fox_prompt_body(q_ref, k_ref, v_ref, kb_ref, o_ref, m_scr, l_scr, acc_scr, *, n_heads, dh):
    qi = pl.program_id(1)
    ki = pl.program_id(2)
    tq = q_ref.shape[0]
    tk = k_ref.shape[0]

    @pl.when(ki == 0)
    def _():
        m_scr[...] = jnp.full(m_scr.shape, NEG_INF, F32)
        l_scr[...] = jnp.zeros(l_scr.shape, F32)
        acc_scr[...] = jnp.zeros(acc_scr.shape, F32)

    def step(masked):
        lane = lax.broadcasted_iota(I32, (tq, LANES), 1)
        kb = kb_ref[...]
        if masked:
            keep = lax.broadcasted_iota(I32, (tk, tq), 0) <= lax.broadcasted_iota(I32, (tk, tq), 1)
        for h in range(n_heads):
            hs = slice(h * dh, (h + 1) * dh)
            ones = ((lane >= BIAS_PARTS * h) & (lane < BIAS_PARTS * (h + 1))).astype(BF16)
            q_aug = jnp.concatenate([q_ref[:, hs], ones], axis=1)
            k_aug = jnp.concatenate([k_ref[:, hs], kb], axis=1)
            s_t = _dot_nt(k_aug, q_aug)
            if masked:
                s_t = jnp.where(keep, s_t, NEG_INF)
            m_old = m_scr[h]
            m_new = jnp.maximum(m_old, jnp.max(s_t, axis=0, keepdims=True))
            alpha = jnp.exp2(m_old - m_new)
            p_t = jnp.exp2(s_t - m_new)
            l_scr[h] = alpha * l_scr[h] + jnp.sum(p_t, axis=0, keepdims=True)
            acc_scr[h] = alpha * acc_scr[h] + _dot_nn(v_ref[:, hs].T, p_t.astype(BF16))
            m_scr[h] = m_new

    @pl.when(ki < qi)
    def _():
        step(False)

    @pl.when(ki == qi)
    def _():
        step(True)

    @pl.when(ki == pl.num_programs(2) - 1)
    def _():
        for h in range(n_heads):
            o_ref[:, h * dh:(h + 1) * dh] = (acc_scr[h] / l_scr[h]).T


def _fox_prompt(qf, kfb, vfb, key_bias, batch, seq, n_heads, dh, tq):
    nq = seq // tq
    width = n_heads * dh
    kv_map = lambda b, qi, ki: (b * nq + jnp.minimum(ki, qi), 0)
    return pl.pallas_call(
        functools.partial(_fox_prompt_body, n_heads=n_heads, dh=dh),
        grid=(batch, nq, nq),
        in_specs=[pl.BlockSpec((tq, width), lambda b, qi, ki: (b * nq + qi, 0)),
                  pl.BlockSpec((tq, width), kv_map),
                  pl.BlockSpec((tq, width), kv_map),
                  pl.BlockSpec((tq, LANES), kv_map)],
        out_specs=pl.BlockSpec((tq, width), lambda b, qi, ki: (b * nq + qi, 0)),
        out_shape=jax.ShapeDtypeStruct((batch * seq, width), F32),
        scratch_shapes=[pltpu.VMEM((n_heads, 1, tq), F32), pltpu.VMEM((n_heads, 1, tq), F32),
                        pltpu.VMEM((n_heads, dh, tq), F32)],
        compiler_params=_cparams(("arbitrary", "arbitrary", "arbitrary")),
        name="fox_prompt",
    )(qf, kfb, vfb, key_bias)


def _fox_sample_body(pt_ref, q_ref, kn_ref, vn_ref, lfn_ref, ck_hbm, cv_hbm, lf_hbm, o_ref,
                     kbuf, vbuf, lfbuf, sem, m_scr, l_scr, acc_scr, carry_scr,
                     *, n_heads, pages_per_step, layer, n_pages):
    P = pages_per_step
    b = pl.program_id(0)
    j = pl.program_id(1)
    steps = pl.num_programs(1)
    rows = q_ref.shape[0]
    tiles = lfbuf.shape[2]

    def fetch(seq, stp, slot):
        for i in range(P):
            pg = pt_ref[seq, n_pages - 1 - (stp * P + i)]
            pltpu.make_async_copy(ck_hbm.at[layer, pg], kbuf.at[slot, i], sem.at[slot]).start()
            pltpu.make_async_copy(cv_hbm.at[layer, pg], vbuf.at[slot, i], sem.at[slot]).start()
            pltpu.make_async_copy(lf_hbm.at[layer, pg], lfbuf.at[slot, i], sem.at[slot]).start()

    step = b * steps + j
    slot = step % 2

    @pl.when(step == 0)
    def _():
        fetch(0, 0, 0)

    @pl.when(step + 1 < pl.num_programs(0) * steps)
    def _():
        wrap = j + 1 == steps
        fetch(jnp.where(wrap, b + 1, b), jnp.where(wrap, 0, j + 1), 1 - slot)

    q = q_ref[...]
    lane = lax.broadcasted_iota(I32, (rows, LANES), 1)
    rowi = lax.broadcasted_iota(I32, (rows, LANES), 0)
    same_head = (lane % n_heads) == (rowi % n_heads)
    l_i = lax.broadcasted_iota(I32, (LANES, LANES), 0)
    l_j = lax.broadcasted_iota(I32, (LANES, LANES), 1)
    head_eq = (l_i % n_heads) == (l_j % n_heads)

    @pl.when(j == 0)
    def _():
        fn = _dot_exact_rhs01(lfn_ref[...], (head_eq & (l_i <= l_j)).astype(BF16))[0:1, :]
        s = _dot_nt(q, kn_ref[...]) - fn
        keep = same_head & (lane // n_heads <= rowi // n_heads) & (lane < rows)
        s = jnp.where(keep, s, NEG_INF)
        m = jnp.max(s, axis=-1, keepdims=True)
        p = jnp.exp(s - m)
        m_scr[...] = m
        l_scr[...] = jnp.sum(p, axis=-1, keepdims=True)
        acc_scr[...] = _dot_nn(p.astype(BF16), vn_ref[...])
        carry_scr[...] = jnp.zeros(carry_scr.shape, F32)

    pltpu.make_async_copy(ck_hbm.at[layer, pl.ds(0, P)], kbuf.at[slot], sem.at[slot]).wait()
    pltpu.make_async_copy(cv_hbm.at[layer, pl.ds(0, P)], vbuf.at[slot], sem.at[slot]).wait()
    pltpu.make_async_copy(lf_hbm.at[layer, pl.ds(0, P)], lfbuf.at[slot], sem.at[slot]).wait()

    x = lfbuf[slot].reshape(P * tiles, LANES)
    within = _dot_exact_rhs01(x, (head_eq & (l_i > l_j)).astype(BF16))
    tot = _dot_exact_rhs01(x, head_eq.astype(BF16))
    nr = P * tiles
    r_i = lax.broadcasted_iota(I32, (nr, nr), 0)
    c_i = lax.broadcasted_iota(I32, (nr, nr), 1)
    later_rows = ((c_i // tiles < r_i // tiles) | ((c_i // tiles == r_i // tiles) & (c_i > r_i))).astype(BF16)
    carry = carry_scr[...]
    suf = within + _dot_exact_lhs01(later_rows, tot) + carry
    carry_scr[...] = carry + jnp.sum(tot, axis=0, keepdims=True)

    k2 = kbuf[slot].reshape(-1, LANES).astype(BF16)
    v2 = vbuf[slot].reshape(-1, LANES).astype(BF16)
    s = _dot_nt(q, k2)
    blocks = [jnp.where(same_head, s[:, c * LANES:(c + 1) * LANES] + suf[c:c + 1, :], NEG_INF) for c in range(nr)]
    blk_max = blocks[0]
    for blk in blocks[1:]:
        blk_max = jnp.maximum(blk_max, blk)
    m_old = m_scr[...]
    m_new = jnp.maximum(m_old, jnp.max(blk_max, axis=-1, keepdims=True))
    alpha = jnp.exp(m_old - m_new)
    probs = [jnp.exp(blk - m_new) for blk in blocks]
    psum = probs[0]
    for pb in probs[1:]:
        psum = psum + pb
    l_scr[...] = alpha * l_scr[...] + jnp.sum(psum, axis=-1, keepdims=True)
    p_all = jnp.concatenate([pb.astype(BF16) for pb in probs], axis=1)
    acc_scr[...] = alpha * acc_scr[...] + _dot_nn(p_all, v2)
    m_scr[...] = m_new

    @pl.when(j == pl.num_programs(1) - 1)
    def _():
        o_ref[...] = acc_scr[...] / l_scr[...]


def _fox_sample_call(page_table, qf, kf, vf, logf, cache_k, cache_v, cache_logf, layer, db, t_new, n_heads, dh,
                     pages_per_step):
    assert dh == LANES
    rows = t_new * n_heads
    n_pool, page = cache_k.shape[1:3]
    n_pages = page_table.shape[1]
    tiles = page * n_heads // LANES
    P = pages_per_step
    steps = n_pages // P
    pad_rows = lambda a: jnp.pad(a.reshape(db, rows, dh), ((0, 0), (0, LANES - rows), (0, 0))).astype(BF16)
    q2 = qf.reshape(db, rows, dh).astype(BF16)
    lfn = jnp.pad(logf.reshape(db, 1, rows), ((0, 0), (0, 7), (0, LANES - rows)))
    lf_pages = cache_logf.reshape(cache_logf.shape[0], n_pool, tiles, LANES)

    per_seq = lambda r: pl.BlockSpec((None, r, LANES), lambda b, j, pt: (b, 0, 0))
    hbm = pl.BlockSpec(memory_space=pl.ANY)
    out = pl.pallas_call(
        functools.partial(_fox_sample_body, n_heads=n_heads, pages_per_step=P, layer=layer, n_pages=n_pages),
        grid_spec=pltpu.PrefetchScalarGridSpec(
            num_scalar_prefetch=1,
            grid=(db, steps),
            in_specs=[per_seq(rows), per_seq(LANES), per_seq(LANES), per_seq(8), hbm, hbm, hbm],
            out_specs=per_seq(rows),
            scratch_shapes=[pltpu.VMEM((2, P, page, n_heads, dh), F32), pltpu.VMEM((2, P, page, n_heads, dh), F32),
                            pltpu.VMEM((2, P, tiles, LANES), F32), pltpu.SemaphoreType.DMA((2,)),
                            pltpu.VMEM((rows, 1), F32), pltpu.VMEM((rows, 1), F32),
                            pltpu.VMEM((rows, LANES), F32), pltpu.VMEM((1, LANES), F32)]),
        out_shape=jax.ShapeDtypeStruct((db, rows, LANES), F32),
        compiler_params=_cparams(("arbitrary", "arbitrary")),
        name="fox_sample",
    )(page_table, q2, pad_rows(kf), pad_rows(vf), lfn, cache_k, cache_v, lf_pages)
    return out.reshape(db * t_new, n_heads * dh)


def _rms_heads(x, n_heads, dh):
    outs = []
    for h in range(n_heads):
        xs = x[:, h * dh:(h + 1) * dh]
        outs.append(xs * lax.rsqrt(jnp.mean(xs * xs, axis=-1, keepdims=True) + EPS))
    return jnp.concatenate(outs, axis=1)


def _route(logits):
    lane = lax.broadcasted_iota(I32, logits.shape, 1)
    big = jnp.int32(LANES)
    gl = jnp.where(lane < N_GROUPS, logits, NEG_INF)
    gmax = jnp.max(gl, axis=-1, keepdims=True)
    g_sel = jnp.min(jnp.where(gl == gmax, lane, big), axis=-1, keepdims=True)
    p_g = 1.0 / jnp.sum(jnp.exp(gl - gmax), axis=-1, keepdims=True)
    lo = N_GROUPS + E_PER_GROUP * g_sel
    ev = jnp.where((lane >= lo) & (lane < lo + E_PER_GROUP), logits, NEG_INF)
    v1 = jnp.max(ev, axis=-1, keepdims=True)
    i1 = jnp.min(jnp.where(ev == v1, lane, big), axis=-1, keepdims=True)
    ev2 = jnp.where(lane == i1, NEG_INF, ev)
    v2 = jnp.max(ev2, axis=-1, keepdims=True)
    i2 = jnp.min(jnp.where(ev2 == v2, lane, big), axis=-1, keepdims=True)
    e21 = jnp.exp(v2 - v1)
    w1 = p_g / (1.0 + e21)
    w2 = p_g * e21 / (1.0 + e21)
    out = jnp.where(lane == 0, (i1 - N_GROUPS).astype(F32), 0.0)
    out = jnp.where(lane == 1, (i2 - N_GROUPS).astype(F32), out)
    out = jnp.where(lane == 2, w1, out)
    out = jnp.where(lane == 3, w2, out)
    return out


def _store_slabs(ref, x):
    tm, d = x.shape
    per = d // LANES
    for s in range(per):
        ref[pl.ds(s, tm, stride=per), :] = x[:, s * LANES:(s + 1) * LANES]


def _load_slabs(ref, tm, per):
    return jnp.concatenate([ref[pl.ds(s, tm, stride=per), :] for s in range(per)], axis=1)


def _merge_body(x_ref, og_ref, gg_ref, of_ref, gw_ref, fw_ref, wo_ref, g1_ref, sh2_ref, sc2_ref, n2_ref,
                wr_hi_ref, wr_lo_ref, br_ref, x1_ref, h2_ref, route_ref, *, n_heads, dh):
    og = _rms_heads(og_ref[...], n_heads, dh) * gw_ref[...] * _silu(gg_ref[...])
    of = _rms_heads(of_ref[...], n_heads, dh) * fw_ref[...]
    merged = jnp.concatenate([og, of], axis=1).astype(BF16)
    x1 = x_ref[...] + g1_ref[...] * _dot_nn(merged, wo_ref[...])
    x1_ref[...] = x1
    y = x1 * lax.rsqrt(jnp.mean(x1 * x1, axis=-1, keepdims=True) + EPS) * n2_ref[...]
    h2 = y * (1.0 + sc2_ref[...]) + sh2_ref[...]
    _store_slabs(h2_ref, h2)
    hh, hl, _ = _split3(h2)
    logits = (_dot_nn(hh, wr_hi_ref[...]) + _dot_nn(hl, wr_hi_ref[...]) + _dot_nn(hh, wr_lo_ref[...])
              + br_ref[...])
    route_ref[...] = _route(logits)


def _merge(x, o_gla, gla_in, o_fox, gla_w, fox_w, w_out_bf, g1, sh2, sc2, norm2_w, wr_hi, wr_lo, b_r,
           tm, rows_per_mod, n_heads, dh):
    n, d = x.shape
    half = n_heads * dh
    per_token = rows_per_mod == 1
    if per_token:
        mod_spec = pl.BlockSpec((tm, d), lambda i: (i, 0))
    else:
        tiles_per_mod = rows_per_mod // tm
        g1, sh2, sc2 = (a.reshape(a.shape[0], 1, d) for a in (g1, sh2, sc2))
        mod_spec = pl.BlockSpec((None, 1, d), lambda i: (i // tiles_per_mod, 0, 0))
    row = lambda i: (i, 0)
    fixed = lambda i: (0, 0)
    return pl.pallas_call(
        functools.partial(_merge_body, n_heads=n_heads, dh=dh),
        grid=(n // tm,),
        in_specs=[pl.BlockSpec((tm, d), row),
                  pl.BlockSpec((tm, half), row),
                  pl.BlockSpec((tm, half), lambda i: (i, 2)),
                  pl.BlockSpec((tm, half), row),
                  pl.BlockSpec((1, half), fixed), pl.BlockSpec((1, half), fixed),
                  pl.BlockSpec((d, d), fixed),
                  mod_spec, mod_spec, mod_spec,
                  pl.BlockSpec((1, d), fixed),
                  pl.BlockSpec((d, LANES), fixed), pl.BlockSpec((d, LANES), fixed),
                  pl.BlockSpec((1, LANES), fixed)],
        out_specs=(pl.BlockSpec((tm, d), row), pl.BlockSpec((tm * (d // LANES), LANES), row),
                   pl.BlockSpec((tm, LANES), row)),
        out_shape=(jax.ShapeDtypeStruct((n, d), F32), jax.ShapeDtypeStruct((n * (d // LANES), LANES), F32),
                   jax.ShapeDtypeStruct((n, LANES), F32)),
        compiler_params=_cparams(("arbitrary",)),
        name="merge_outproj",
    )(x, o_gla, gla_in, o_fox, gla_w.reshape(1, half), fox_w.reshape(1, half), w_out_bf, g1, sh2, sc2,
      norm2_w.reshape(1, d), wr_hi, wr_lo, b_r)


GATHER_UNROLL = 8


def _experts_body(te_ref, first_ref, tok_ref, nt_ref, h_hbm, wg_ref, wu_ref, wd_ref, y_ref,
                  xbuf, wg_bf, wu_bf, wd_bf, sem, *, tm, per):
    i = pl.program_id(0)
    n_active = nt_ref[0]

    def gather(tile, slot):
        first = first_ref[tile]

        def issue(g, carry):
            for u in range(GATHER_UNROLL):
                r = g * GATHER_UNROLL + u
                src = pl.multiple_of(tok_ref[first + r] * per, per)
                pltpu.make_async_copy(h_hbm.at[pl.ds(src, per), :], xbuf.at[slot, pl.ds(r * per, per), :],
                                      sem.at[slot]).start()
            return carry
        lax.fori_loop(0, tm // GATHER_UNROLL, issue, 0)

    @pl.when((i == 0) & (n_active > 0))
    def _():
        gather(0, 0)

    @pl.when(i + 1 < n_active)
    def _():
        gather(i + 1, (i + 1) % 2)

    @pl.when(i < n_active)
    def _():
        slot = i % 2
        prev = te_ref[jnp.maximum(i - 1, 0)]

        @pl.when((i == 0) | (te_ref[i] != prev))
        def _():
            wg_bf[...] = wg_ref[...].astype(BF16)
            wu_bf[...] = wu_ref[...].astype(BF16)
            wd_bf[...] = wd_ref[...].astype(BF16)

        pltpu.make_async_copy(h_hbm.at[pl.ds(0, tm * per), :], xbuf.at[slot], sem.at[slot]).wait()
        x = _load_slabs(xbuf.at[slot], tm, per).astype(BF16)
        a = (_silu(_dot_nn(x, wg_bf[...])) * _dot_nn(x, wu_bf[...])).astype(BF16)
        _store_slabs(y_ref, _dot_nn(a, wd_bf[...]))

    @pl.when(i >= n_active)
    def _():
        y_ref[...] = jnp.zeros(y_ref.shape, F32)


def _experts(tile_expert, tile_first, sorted_tok, n_tiles, h2_slabs, w_gate, w_up, w_down, tm):
    max_tiles = tile_expert.shape[0]
    _, d, dff = w_gate.shape
    per = d // LANES
    wmap = lambda i, te, first, tok, nt: (te[i], 0, 0)
    return pl.pallas_call(
        functools.partial(_experts_body, tm=tm, per=per),
        grid_spec=pltpu.PrefetchScalarGridSpec(
            num_scalar_prefetch=4,
            grid=(max_tiles,),
            in_specs=[pl.BlockSpec(memory_space=pl.ANY),
                      pl.BlockSpec((None, d, dff), wmap),
                      pl.BlockSpec((None, d, dff), wmap),
                      pl.BlockSpec((None, dff, d), wmap)],
            out_specs=pl.BlockSpec((tm * per, LANES), lambda i, te, first, tok, nt: (i, 0)),
            scratch_shapes=[pltpu.VMEM((2, tm * per, LANES), F32),
                            pltpu.VMEM((d, dff), BF16), pltpu.VMEM((d, dff), BF16), pltpu.VMEM((dff, d), BF16),
                            pltpu.SemaphoreType.DMA((2,))]),
        out_shape=jax.ShapeDtypeStruct((max_tiles * tm * per, LANES), F32),
        compiler_params=_cparams(("arbitrary",)),
        name="experts",
    )(tile_expert, tile_first, sorted_tok, n_tiles, h2_slabs, w_gate, w_up, w_down)


def _combine_body(pos_ref, y_hbm, x1_ref, route_ref, g2_ref, fw_ref, o_ref, buf, sem, *, tm, row0, per):
    i = pl.program_id(0)

    def gather(tile, slot):
        def issue(g, carry):
            for u in range(GATHER_UNROLL // 2):
                r = g * (GATHER_UNROLL // 2) + u
                tok = row0 + tile * tm + r
                for c in range(2):
                    src = pl.multiple_of(pos_ref[2 * tok + c] * per, per)
                    pltpu.make_async_copy(y_hbm.at[pl.ds(src, per), :],
                                          buf.at[slot, c, pl.ds(r * per, per), :], sem.at[slot]).start()
            return carry
        lax.fori_loop(0, tm // (GATHER_UNROLL // 2), issue, 0)

    @pl.when(i == 0)
    def _():
        gather(0, 0)

    @pl.when(i + 1 < pl.num_programs(0))
    def _():
        gather(i + 1, (i + 1) % 2)

    slot = i % 2
    for c in range(2):
        pltpu.make_async_copy(y_hbm.at[pl.ds(0, tm * per), :], buf.at[slot, c], sem.at[slot]).wait()
    route = route_ref[...]
    moe = (route[:, 2:3] * _load_slabs(buf.at[slot, 0], tm, per)
           + route[:, 3:4] * _load_slabs(buf.at[slot, 1], tm, per))
    x2 = x1_ref[...] + g2_ref[...] * moe
    o_ref[...] = x2 * lax.rsqrt(jnp.mean(x2 * x2, axis=-1, keepdims=True) + EPS) * fw_ref[...]


def _combine(pos, y_slabs, x1, route, g2, final_w, row0, n_rows, tm, rows_per_mod):
    d = x1.shape[1]
    per = d // LANES
    per_token = rows_per_mod == 1
    tile0 = row0 // tm
    if per_token:
        mod_spec = pl.BlockSpec((tm, d), lambda i, pos: (i, 0))
    else:
        tiles_per_mod = rows_per_mod // tm
        g2 = g2.reshape(g2.shape[0], 1, d)
        mod_spec = pl.BlockSpec((None, 1, d), lambda i, pos: (i // tiles_per_mod, 0, 0))
    return pl.pallas_call(
        functools.partial(_combine_body, tm=tm, row0=row0, per=per),
        grid_spec=pltpu.PrefetchScalarGridSpec(
            num_scalar_prefetch=1,
            grid=(n_rows // tm,),
            in_specs=[pl.BlockSpec(memory_space=pl.ANY),
                      pl.BlockSpec((tm, d), lambda i, pos: (tile0 + i, 0)),
                      pl.BlockSpec((tm, LANES), lambda i, pos: (tile0 + i, 0)),
                      mod_spec,
                      pl.BlockSpec((1, d), lambda i, pos: (0, 0))],
            out_specs=pl.BlockSpec((tm, d), lambda i, pos: (i, 0)),
            scratch_shapes=[pltpu.VMEM((2, 2, tm * per, LANES), F32), pltpu.SemaphoreType.DMA((2,))]),
        out_shape=jax.ShapeDtypeStruct((n_rows, d), F32),
        compiler_params=_cparams(("arbitrary",)),
        name="combine_norm",
    )(pos, y_slabs, x1, route, g2, final_w.reshape(1, d))


def _invert_body(where_ref, out_ref, *, n_pairs, n_out):
    def fill(g, carry):
        for u in range(GATHER_UNROLL):
            a = g * GATHER_UNROLL + u
            out_ref[where_ref[a]] = lax.shift_right_logical(a, 1)
        return carry
    lax.fori_loop(0, n_pairs // GATHER_UNROLL, fill, 0)

    def tail(t, carry):
        out_ref[n_pairs + t] = 0
        return carry
    lax.fori_loop(0, n_out - n_pairs, tail, 0)


def _sorted_tokens(where, n_out):
    n_pairs = where.shape[0]
    assert n_pairs % GATHER_UNROLL == 0
    smem = pl.BlockSpec(memory_space=pltpu.SMEM)
    return pl.pallas_call(
        functools.partial(_invert_body, n_pairs=n_pairs, n_out=n_out),
        in_specs=[smem], out_specs=smem,
        out_shape=jax.ShapeDtypeStruct((n_out,), I32),
        name="sorted_tokens",
    )(where)


def _plan(route, n_experts, tm):
    n = route.shape[0]
    e_flat = route[:, 0:2].astype(I32).reshape(-1)
    onehot = (e_flat[:, None] == jnp.arange(n_experts, dtype=I32)[None, :]).astype(I32)
    csum = jnp.cumsum(onehot, axis=0)
    counts = csum[-1]
    rank = jnp.sum(onehot * csum, axis=1) - 1
    tiles_e = (counts + tm - 1) // tm
    tile_end = jnp.cumsum(tiles_e)
    tile_off = tile_end - tiles_e
    cnt_off = jnp.cumsum(counts) - counts
    max_tiles = (2 * n) // tm + n_experts
    pos = jnp.sum(onehot * (tile_off * tm)[None, :], axis=1) + rank
    where = jnp.sum(onehot * cnt_off[None, :], axis=1) + rank
    tile_ids = jnp.arange(max_tiles, dtype=I32)
    tile_expert = jnp.minimum(jnp.sum((tile_end[None, :] <= tile_ids[:, None]).astype(I32), axis=1), n_experts - 1)
    is_e = (tile_expert[:, None] == jnp.arange(n_experts, dtype=I32)[None, :]).astype(I32)
    tile_first = jnp.sum(is_e * (cnt_off - tile_off * tm)[None, :], axis=1) + tile_ids * tm
    tile_first = jnp.clip(tile_first, 0, 2 * n)
    return (tile_expert, tile_first.astype(I32), tile_end[-1].reshape(1).astype(I32), pos.astype(I32),
            where.astype(I32))


def kernel(x_prompt, x_sample, cache_k, cache_v, cache_logf, state_gla, page_table, c_prompt, c_sample,
           norm1_w, norm2_w, w_ada, b_ada, w_in, w_gk2, b_gk, b_fgate, gla_onorm_w, fox_onorm_w, w_out,
           w_rg, b_rg, w_re, b_re, w_gate_e, w_up_e, w_down_e, final_norm_w):
    depth = w_in.shape[0]
    assert depth == 1, "single-layer trunk"
    batch, seq, d = x_prompt.shape
    db, t_new, _ = x_sample.shape
    n_hg, dk, dv = state_gla.shape[2:]
    n_hf, dh = cache_k.shape[3:]
    rank = w_gk2.shape[1]
    n_experts = w_gate_e.shape[1]
    wq = n_hg * dk
    wv = n_hg * dv
    wf = n_hf * dh
    assert wv == 1024 and wf == 1024 and 2 * wq == 1024 and rank == 16 and n_hf == 8
    n_p = batch * seq
    n_s = db * t_new
    layer = 0

    wi = w_in[layer]
    o_alr = 2 * wq + 2 * wv
    o_fox = o_alr + rank
    o_fl = o_fox + 3 * wf
    w_main = jnp.concatenate([wi[:, :o_alr], wi[:, o_fox:o_fl]], axis=1).astype(BF16)
    w_small = jnp.concatenate([wi[:, o_alr:o_fox], wi[:, o_fl:], jnp.zeros((d, LANES - rank - n_hf), F32)],
                              axis=1).astype(BF16)
    b_small = jnp.zeros((1, LANES), F32).at[0, rank:rank + n_hf].set(b_fgate[layer])
    w_gk2_pad = jnp.zeros((LANES, wq), F32).at[:rank].set(w_gk2[layer])
    b_gk2 = b_gk[layer].reshape(1, wq)
    w_out_bf = w_out[layer].astype(BF16)
    w_r = jnp.concatenate([w_rg[layer], w_re[layer],
                           jnp.zeros((d, LANES - N_GROUPS - n_experts), F32)], axis=1)
    wr_hi = w_r.astype(BF16)
    wr_lo = (w_r - wr_hi.astype(F32)).astype(BF16)
    b_r = jnp.concatenate([b_rg[layer], b_re[layer], jnp.zeros((LANES - N_GROUPS - n_experts,), F32)]).reshape(1, LANES)

    n_c = batch + db
    c_all = jnp.concatenate([c_prompt, c_sample, jnp.zeros((-n_c % 8, d), F32)], axis=0)
    mod = _adaln(c_all, w_ada[layer], b_ada[layer])
    sh1, sc1, g1, sh2, sc2, g2 = (mod[:, i * d:(i + 1) * d] for i in range(6))
    p_rows = slice(0, batch)
    rep = lambda a: jnp.repeat(a[batch:n_c], t_new, axis=0)

    q_scale = dh ** -0.5
    xp = x_prompt.reshape(n_p, d)
    xs = x_sample.reshape(n_s, d)

    tm_p = 512
    gla_p, qf_p, kf_p, vf_p, kfb_p, vfb_p, small_p = _inproj(
        xp, sh1[p_rows], sc1[p_rows], norm1_w[layer], w_main, w_small, b_small, tm_p, seq, q_scale * LOG2E)
    s0_p = jnp.zeros((batch, n_hg, dk, dv), F32)
    o_gla_p, gla_state_p = _gla(gla_p, small_p, w_gk2_pad, b_gk2, s0_p, batch, seq, 128, n_hg, dk, dv)
    logf_p = small_p[:, rank:rank + n_hf]
    key_bias = _key_bias(small_p, batch, seq, rank, n_hf)
    o_fox_p = _fox_prompt(qf_p, kfb_p, vfb_p, key_bias, batch, seq, n_hf, dh, 512)
    x1_p, h2_p, route_p = _merge(xp, o_gla_p, gla_p, o_fox_p, gla_onorm_w[layer], fox_onorm_w[layer], w_out_bf,
                                 g1[p_rows], sh2[p_rows], sc2[p_rows], norm2_w[layer], wr_hi, wr_lo, b_r,
                                 256, seq, n_hg, dv)

    gla_s, qf_s, kf_s, vf_s, _, _, small_s = _inproj(
        xs, rep(sh1), rep(sc1), norm1_w[layer], w_main, w_small, b_small, n_s, 1, q_scale)
    t_pad = GLA_CHUNK
    pad_t = lambda a: jnp.pad(a.reshape(db, t_new, -1), ((0, 0), (0, t_pad - t_new), (0, 0))).reshape(db * t_pad, -1)
    o_gla_s_pad, gla_state_s = _gla(pad_t(gla_s), pad_t(small_s), w_gk2_pad, b_gk2, state_gla[layer],
                                    db, t_pad, t_pad, n_hg, dk, dv, t_valid=t_new)
    o_gla_s = o_gla_s_pad.reshape(db, t_pad, wv)[:, :t_new].reshape(n_s, wv)

    logf_s = small_s[:, rank:rank + n_hf]
    o_fox_s = _fox_sample_call(page_table, qf_s, kf_s, vf_s, logf_s, cache_k, cache_v, cache_logf,
                               layer, db, t_new, n_hf, dh, 8)
    x1_s, h2_s, route_s = _merge(xs, o_gla_s, gla_s, o_fox_s, gla_onorm_w[layer], fox_onorm_w[layer], w_out_bf,
                                 rep(g1), rep(sh2), rep(sc2), norm2_w[layer], wr_hi, wr_lo, b_r,
                                 n_s, 1, n_hg, dv)

    tm_e = 256
    x1 = jnp.concatenate([x1_p, x1_s], axis=0)
    h2 = jnp.concatenate([h2_p, h2_s], axis=0)
    route = jnp.concatenate([route_p, route_s], axis=0)
    tile_expert, tile_first, n_tiles, pos, where = _plan(route, n_experts, tm_e)
    sorted_tok = _sorted_tokens(where, where.shape[0] + tm_e)
    y_slabs = _experts(tile_expert, tile_first, sorted_tok, n_tiles, h2, w_gate_e[layer], w_up_e[layer],
                       w_down_e[layer], tm_e)
    y_p = _combine(pos, y_slabs, x1, route, g2[p_rows], final_norm_w, 0, n_p, 256, seq)
    y_s = _combine(pos, y_slabs, x1, route, rep(g2), final_norm_w, n_p, n_s, n_s, 1)

    y_prompt = y_p.reshape(batch, seq, d)
    y_sample = y_s.reshape(db, t_new, d)
    k_prompt = kf_p.reshape(1, batch, seq, n_hf, dh)
    v_prompt = vf_p.reshape(1, batch, seq, n_hf, dh)
    logf_prompt = logf_p.reshape(1, batch, seq, n_hf)
    k_sample = kf_s.reshape(1, db, t_new, n_hf, dh)
    v_sample = vf_s.reshape(1, db, t_new, n_hf, dh)
    logf_sample = logf_s.reshape(1, db, t_new, n_hf)
    return (y_prompt, y_sample, k_prompt, v_prompt, logf_prompt, gla_state_p[None],
            k_sample, v_sample, logf_sample, gla_state_s[None])
```

```python
import functools

import jax
import jax.numpy as jnp
from jax import lax
from jax.experimental import pallas as pl
from jax.experimental.pallas import tpu as pltpu

F32 = jnp.float32
BF16 = jnp.bfloat16
I32 = jnp.int32
EPS = 1e-6
NEG_INF = float("-inf")

LANES = 128
GLA_CHUNK = 16
GLA_GATE_NORM = 16.0
N_GROUPS = 4
E_PER_GROUP = 8
VMEM_LIMIT = 56 * 1024 * 1024


def _cparams(sem):
    return pltpu.CompilerParams(dimension_semantics=sem, vmem_limit_bytes=VMEM_LIMIT)


def _log_sigmoid(z):
    return jnp.minimum(z, 0.0) - jnp.log1p(jnp.exp(-jnp.abs(z)))


def _silu(z):
    return z * jax.nn.sigmoid(z)


def _split3(a):
    hi = a.astype(BF16)
    r = a - hi.astype(F32)
    mid = r.astype(BF16)
    lo = (r - mid.astype(F32)).astype(BF16)
    return hi, mid, lo


def _dot_nn(a, b):
    return jnp.dot(a, b, preferred_element_type=F32)


def _dot_nt(a, b):
    return lax.dot_general(a, b, (((1,), (1,)), ((), ())), preferred_element_type=F32)


def _dot_f32(a, b):
    ah, al, _ = _split3(a)
    bh, bl, _ = _split3(b)
    return _dot_nn(ah, bh) + _dot_nn(al, bh) + _dot_nn(ah, bl)


def _dot_exact_rhs01(a, ones_bf16):
    hi, mid, lo = _split3(a)
    return _dot_nn(hi, ones_bf16) + _dot_nn(mid, ones_bf16) + _dot_nn(lo, ones_bf16)


def _dot_exact_lhs01(ones_bf16, b):
    hi, mid, lo = _split3(b)
    return _dot_nn(ones_bf16, hi) + _dot_nn(ones_bf16, mid) + _dot_nn(ones_bf16, lo)


def _adaln_body(c_ref, w_ref, b_ref, o_ref):
    s = _silu(c_ref[...]).astype(BF16)
    o_ref[...] = _dot_nn(s, w_ref[...].astype(BF16)) + b_ref[...]


def _adaln(c_all, w_ada, b_ada, tn=1024):
    rows, d = c_all.shape
    n6 = w_ada.shape[1]
    return pl.pallas_call(
        _adaln_body,
        grid=(n6 // tn,),
        in_specs=[pl.BlockSpec((rows, d), lambda j: (0, 0)),
                  pl.BlockSpec((d, tn), lambda j: (0, j)),
                  pl.BlockSpec((1, tn), lambda j: (0, j))],
        out_specs=pl.BlockSpec((rows, tn), lambda j: (0, j)),
        out_shape=jax.ShapeDtypeStruct((rows, n6), F32),
        compiler_params=_cparams(("arbitrary",)),
        name="adaln",
    )(c_all, w_ada, b_ada.reshape(1, n6))


def _inproj_body(x_ref, sh_ref, sc_ref, nw_ref, wa_ref, wf_ref, ws_ref, bs_ref,
                 gla_ref, qf_ref, kf_ref, vf_ref, kfb_ref, vfb_ref, small_ref, h_scr, *, q_scale, nb):
    j = pl.program_id(1)

    @pl.when(j == 0)
    def _():
        x = x_ref[...]
        y = x * lax.rsqrt(jnp.mean(x * x, axis=-1, keepdims=True) + EPS) * nw_ref[...]
        hb = (y * (1.0 + sc_ref[...]) + sh_ref[...]).astype(BF16)
        h_scr[...] = hb
        sm = _dot_nn(hb, ws_ref[...])
        lane = lax.broadcasted_iota(I32, sm.shape, 1)
        small_ref[...] = jnp.where((lane >= 16) & (lane < 24), _log_sigmoid(sm + bs_ref[...]), sm)

    @pl.when(j < 3 * nb)
    def _():
        gla_ref[...] = _dot_nn(h_scr[...], wa_ref[...])

    @pl.when((j >= 3 * nb) & (j < 4 * nb))
    def _():
        qf_ref[...] = (_dot_nn(h_scr[...], wf_ref[...]) * q_scale).astype(BF16)

    @pl.when((j >= 4 * nb) & (j < 5 * nb))
    def _():
        acc = _dot_nn(h_scr[...], wf_ref[...])
        kf_ref[...] = acc
        kfb_ref[...] = acc.astype(BF16)

    @pl.when(j >= 5 * nb)
    def _():
        acc = _dot_nn(h_scr[...], wf_ref[...])
        vf_ref[...] = acc
        vfb_ref[...] = acc.astype(BF16)


def _inproj(x, sh, sc, norm_w, w_gla, w_fox, w_small, b_small, tm, rows_per_mod, q_scale):
    n, d = x.shape
    wide = w_fox.shape[1] // 3
    tn = 512
    nb = wide // tn
    per_token = rows_per_mod == 1
    if per_token:
        mod_spec = pl.BlockSpec((tm, d), lambda i, j: (i, 0))
    else:
        tiles_per_mod = rows_per_mod // tm
        sh = sh.reshape(sh.shape[0], 1, d)
        sc = sc.reshape(sc.shape[0], 1, d)
        mod_spec = pl.BlockSpec((None, 1, d), lambda i, j: (i // tiles_per_mod, 0, 0))
    row = lambda i, j: (i, 0)
    group = lambda g: (lambda i, j: (i, jnp.clip(j - g * nb, 0, nb - 1)))
    out_shapes = (jax.ShapeDtypeStruct((n, 3 * wide), F32),
                  jax.ShapeDtypeStruct((n, wide), BF16),
                  jax.ShapeDtypeStruct((n, wide), F32),
                  jax.ShapeDtypeStruct((n, wide), F32),
                  jax.ShapeDtypeStruct((n, wide), BF16),
                  jax.ShapeDtypeStruct((n, wide), BF16),
                  jax.ShapeDtypeStruct((n, LANES), F32))
    return pl.pallas_call(
        functools.partial(_inproj_body, q_scale=q_scale, nb=nb),
        grid=(n // tm, 6 * nb),
        in_specs=[pl.BlockSpec((tm, d), row), mod_spec, mod_spec,
                  pl.BlockSpec((1, d), lambda i, j: (0, 0)),
                  pl.BlockSpec((d, tn), lambda i, j: (0, jnp.minimum(j, 3 * nb - 1))),
                  pl.BlockSpec((d, tn), lambda i, j: (0, jnp.maximum(j - 3 * nb, 0))),
                  pl.BlockSpec((d, LANES), lambda i, j: (0, 0)),
                  pl.BlockSpec((1, LANES), lambda i, j: (0, 0))],
        out_specs=(pl.BlockSpec((tm, tn), lambda i, j: (i, jnp.minimum(j, 3 * nb - 1))),
                   pl.BlockSpec((tm, tn), group(3)), pl.BlockSpec((tm, tn), group(4)),
                   pl.BlockSpec((tm, tn), group(5)), pl.BlockSpec((tm, tn), group(4)),
                   pl.BlockSpec((tm, tn), group(5)),
                   pl.BlockSpec((tm, LANES), row)),
        out_shape=out_shapes,
        scratch_shapes=[pltpu.VMEM((tm, d), BF16)],
        compiler_params=_cparams(("arbitrary", "arbitrary")),
        name="inproj",
    )(x, sh, sc, norm_w.reshape(1, d), w_gla, w_fox, w_small, b_small)


def _level_ref(b, half, rows_i):
    tt, n = b.shape
    if half >= 4:
        size = 2 * half
        blocks = [jnp.broadcast_to(b[g * size + half - 1:g * size + half, :], (size, n)) for g in range(tt // size)]
        return blocks[0] if len(blocks) == 1 else jnp.concatenate(blocks, axis=0)
    if half == 2:
        lo = jnp.concatenate([jnp.broadcast_to(b[8 * g + 1:8 * g + 2, :], (8, n)) for g in range(tt // 8)], axis=0)
        hi = jnp.concatenate([jnp.broadcast_to(b[8 * g + 5:8 * g + 6, :], (8, n)) for g in range(tt // 8)], axis=0)
        return jnp.where(rows_i % 8 < 4, lo, hi)
    return jnp.where(rows_i % 2 == 1, pltpu.roll(b, 1, 0), b)


def _gla_body(q_ref, k_ref, v_ref, sm_ref, w2_ref, b2_ref, s0_ref, o_ref, sout_ref, st_scr,
              *, tt, dk, dv, n_pairs, t_valid):
    t = pl.program_id(1)
    zero_blk = jnp.zeros((dv, dk), F32)

    @pl.when(t == 0)
    def _():
        for p in range(n_pairs):
            top = jnp.concatenate([s0_ref[2 * p].T, zero_blk], axis=1)
            bot = jnp.concatenate([zero_blk, s0_ref[2 * p + 1].T], axis=1)
            st_scr[p] = jnp.concatenate([top, bot], axis=0)

    gk = _log_sigmoid(_dot_f32(sm_ref[...], w2_ref[...]) + b2_ref[...]) / GLA_GATE_NORM
    rows_i = lax.broadcasted_iota(I32, (tt, 1), 0)
    if t_valid < tt:
        gk = jnp.where(rows_i < t_valid, gk, 0.0)
    r_i = lax.broadcasted_iota(I32, (tt, tt), 0)
    c_i = lax.broadcasted_iota(I32, (tt, tt), 1)
    b_all = _dot_exact_lhs01((c_i <= r_i).astype(BF16), gk)

    halves = []
    half = tt // 2
    while half >= 1:
        halves.append(half)
        half //= 2
    refs = [_level_ref(b_all, hf, rows_i) for hf in halves]
    valid = [(r_i // (2 * hf) == c_i // (2 * hf)) & (r_i % (2 * hf) >= hf) & (c_i % (2 * hf) < hf) for hf in halves]
    on_diag = r_i == c_i
    head0 = lax.broadcasted_iota(I32, (tt, 2 * dk), 1) < dk
    rr = lax.broadcasted_iota(I32, (2 * dv, 2 * dk), 0) // dv
    cc = lax.broadcasted_iota(I32, (2 * dv, 2 * dk), 1) // dk
    diag = (rr == cc).astype(F32)

    def head_scores(qx, kx):
        stacked = jnp.concatenate([jnp.where(head0, qx, 0.0), jnp.where(head0, 0.0, qx)], axis=0).astype(BF16)
        return _dot_nt(stacked, kx.astype(BF16))

    for p in range(n_pairs):
        cs = slice(p * 2 * dk, (p + 1) * 2 * dk)
        vs = slice(p * 2 * dv, (p + 1) * 2 * dv)
        q = q_ref[:, cs] * (dk ** -0.5)
        k = k_ref[:, cs]
        v = v_ref[:, vs]
        b = b_all[:, cs]
        sc = head_scores(q, k)
        a0 = jnp.where(on_diag, sc[:tt], 0.0)
        a1 = jnp.where(on_diag, sc[tt:], 0.0)
        for ref_all, ok in zip(refs, valid):
            ref = ref_all[:, cs]
            sc = head_scores(q * jnp.exp(jnp.minimum(b - ref, 0.0)), k * jnp.exp(jnp.minimum(ref - b, 0.0)))
            a0 = jnp.where(ok, sc[:tt], a0)
            a1 = jnp.where(ok, sc[tt:], a1)
        vb = v.astype(BF16)
        o_intra = jnp.concatenate([_dot_nn(a0.astype(BF16), vb[:, :dv]), _dot_nn(a1.astype(BF16), vb[:, dv:])], axis=1)

        st = st_scr[p]
        bend = b[tt - 1:tt]
        o_inter = _dot_nt((q * jnp.exp(b)).astype(BF16), st.astype(BF16))
        kdec = (k * jnp.exp(bend - b)).astype(BF16)
        st = st * jnp.exp(bend) + _dot_nn(v.T.astype(BF16), kdec) * diag
        st_scr[p] = st
        o_ref[:, vs] = o_intra + o_inter

    @pl.when(t == pl.num_programs(1) - 1)
    def _():
        for p in range(n_pairs):
            st = st_scr[p]
            sout_ref[2 * p] = st[:dv, :dk].T
            sout_ref[2 * p + 1] = st[dv:, dk:].T


def _gla(gla_in, small, w_gk2_pad, b_gk, s0, batch, seq, tt, n_heads, dk, dv, t_valid=None):
    n = batch * seq
    nt = seq // tt
    wq = n_heads * dk
    wv = n_heads * dv
    assert wv == 2 * wq
    row = lambda b, t: b * nt + t
    return pl.pallas_call(
        functools.partial(_gla_body, tt=tt, dk=dk, dv=dv, n_pairs=n_heads // 2,
                          t_valid=tt if t_valid is None else t_valid),
        grid=(batch, nt),
        in_specs=[pl.BlockSpec((tt, wq), lambda b, t: (row(b, t), 0)),
                  pl.BlockSpec((tt, wq), lambda b, t: (row(b, t), 1)),
                  pl.BlockSpec((tt, wv), lambda b, t: (row(b, t), 1)),
                  pl.BlockSpec((tt, LANES), lambda b, t: (row(b, t), 0)),
                  pl.BlockSpec((LANES, wq), lambda b, t: (0, 0)),
                  pl.BlockSpec((1, wq), lambda b, t: (0, 0)),
                  pl.BlockSpec((None, n_heads, dk, dv), lambda b, t: (b, 0, 0, 0))],
        out_specs=(pl.BlockSpec((tt, wv), lambda b, t: (row(b, t), 0)),
                   pl.BlockSpec((None, n_heads, dk, dv), lambda b, t: (b, 0, 0, 0))),
        out_shape=(jax.ShapeDtypeStruct((n, wv), F32),
                   jax.ShapeDtypeStruct((batch, n_heads, dk, dv), F32)),
        scratch_shapes=[pltpu.VMEM((n_heads // 2, 2 * dv, 2 * dk), F32)],
        compiler_params=_cparams(("arbitrary", "arbitrary")),
        name="gla",
    )(gla_in, gla_in, gla_in, small, w_gk2_pad, b_gk, s0)


LOG2E = 1.4426950408889634
BIAS_PARTS = 3


def _key_bias_body(sm_ref, o_ref, *, lane0, n_heads, blk):
    s = sm_ref.shape[0]
    r_i = lax.broadcasted_iota(I32, (blk, blk), 0)
    c_i = lax.broadcasted_iota(I32, (blk, blk), 1)
    lower = (c_i <= r_i).astype(BF16)
    head = r_i - lane0
    sel = [((head >= 0) & (head < n_heads) & (c_i == BIAS_PARTS * head + part)).astype(BF16)
           for part in range(BIAS_PARTS)]
    carry = jnp.zeros((1, sm_ref.shape[1]), F32)
    for i in range(s // blk):
        f = _dot_exact_lhs01(lower, sm_ref[i * blk:(i + 1) * blk, :]) + carry
        carry = f[blk - 1:blk, :]
        pieces = _split3(f * (-LOG2E))
        placed = sum(_dot_nn(piece, sel_p) for piece, sel_p in zip(pieces, sel))
        o_ref[i * blk:(i + 1) * blk, :] = placed.astype(BF16)


def _key_bias(small, batch, seq, lane0, n_heads):
    return pl.pallas_call(
        functools.partial(_key_bias_body, lane0=lane0, n_heads=n_heads, blk=LANES),
        grid=(batch,),
        in_specs=[pl.BlockSpec((seq, LANES), lambda i: (i, 0))],
        out_specs=pl.BlockSpec((seq, LANES), lambda i: (i, 0)),
        out_shape=jax.ShapeDtypeStruct((batch * seq, LANES), BF16),
        compiler_params=_cparams(("arbitrary",)),
        name="fox_key_bias",
    )(small)


def _fox_prompt_body(q_ref, k_ref, v_ref, kb_ref, o_ref, m_scr, l_scr, acc_scr, *, n_heads, dh):
    qi = pl.program_id(1)
    ki = pl.program_id(2)
    tq = q_ref.shape[0]
    tk = k_ref.shape[0]

    @pl.when(ki == 0)
    def _():
        m_scr[...] = jnp.full(m_scr.shape, NEG_INF, F32)
        l_scr[...] = jnp.zeros(l_scr.shape, F32)
        acc_scr[...] = jnp.zeros(acc_scr.shape, F32)

    def step(masked):
        lane = lax.broadcasted_iota(I32, (tq, LANES), 1)
        kb = kb_ref[...]
        if masked:
            keep = lax.broadcasted_iota(I32, (tk, tq), 0) <= lax.broadcasted_iota(I32, (tk, tq), 1)
        for h in range(n_heads):
            hs = slice(h * dh, (h + 1) * dh)
            ones = ((lane >= BIAS_PARTS * h) & (lane < BIAS_PARTS * (h + 1))).astype(BF16)
            q_aug = jnp.concatenate([q_ref[:, hs], ones], axis=1)
            k_aug = jnp.concatenate([k_ref[:, hs], kb], axis=1)
            s_t = _dot_nt(k_aug, q_aug)
            if masked:
                s_t = jnp.where(keep, s_t, NEG_INF)
            m_old = m_scr[h]
            m_new = jnp.maximum(m_old, jnp.max(s_t, axis=0, keepdims=True))
            alpha = jnp.exp2(m_old - m_new)
            p_t = jnp.exp2(s_t - m_new)
            l_scr[h] = alpha * l_scr[h] + jnp.sum(p_t, axis=0, keepdims=True)
            acc_scr[h] = alpha * acc_scr[h] + _dot_nn(v_ref[:, hs].T, p_t.astype(BF16))
            m_scr[h] = m_new

    @pl.when(ki < qi)
    def _():
        step(False)

    @pl.when(ki == qi)
    def _():
        step(True)

    @pl.when(ki == pl.num_programs(2) - 1)
    def _():
        for h in range(n_heads):
            o_ref[:, h * dh:(h + 1) * dh] = (acc_scr[h] / l_scr[h]).T


def _fox_prompt(qf, kfb, vfb, key_bias, batch, seq, n_heads, dh, tq):
    nq = seq // tq
    width = n_heads * dh
    kv_map = lambda b, qi, ki: (b * nq + jnp.minimum(ki, qi), 0)
    return pl.pallas_call(
        functools.partial(_fox_prompt_body, n_heads=n_heads, dh=dh),
        grid=(batch, nq, nq),
        in_specs=[pl.BlockSpec((tq, width), lambda b, qi, ki: (b * nq + qi, 0)),
                  pl.BlockSpec((tq, width), kv_map),
                  pl.BlockSpec((tq, width), kv_map),
                  pl.BlockSpec((tq, LANES), kv_map)],
        out_specs=pl.BlockSpec((tq, width), lambda b, qi, ki: (b * nq + qi, 0)),
        out_shape=jax.ShapeDtypeStruct((batch * seq, width), F32),
        scratch_shapes=[pltpu.VMEM((n_heads, 1, tq), F32), pltpu.VMEM((n_heads, 1, tq), F32),
                        pltpu.VMEM((n_heads, dh, tq), F32)],
        compiler_params=_cparams(("arbitrary", "arbitrary", "arbitrary")),
        name="fox_prompt",
    )(qf, kfb, vfb, key_bias)


def _fox_sample_body(pt_ref, q_ref, kn_ref, vn_ref, lfn_ref, ck_hbm, cv_hbm, lf_hbm, o_ref,
                     kbuf, vbuf, lfbuf, sem, m_scr, l_scr, acc_scr, carry_scr,
                     *, n_heads, pages_per_step, layer, n_pages):
    P = pages_per_step
    b = pl.program_id(0)
    j = pl.program_id(1)
    steps = pl.num_programs(1)
    rows = q_ref.shape[0]
    tiles = lfbuf.shape[2]

    def fetch(seq, stp, slot):
        for i in range(P):
            pg = pt_ref[seq, n_pages - 1 - (stp * P + i)]
            pltpu.make_async_copy(ck_hbm.at[layer, pg], kbuf.at[slot, i], sem.at[slot]).start()
            pltpu.make_async_copy(cv_hbm.at[layer, pg], vbuf.at[slot, i], sem.at[slot]).start(priority=1)
            pltpu.make_async_copy(lf_hbm.at[layer, pg], lfbuf.at[slot, i], sem.at[slot]).start()

    step = b * steps + j
    slot = step % 2

    @pl.when(step == 0)
    def _():
        fetch(0, 0, 0)

    @pl.when(step + 1 < pl.num_programs(0) * steps)
    def _():
        wrap = j + 1 == steps
        fetch(jnp.where(wrap, b + 1, b), jnp.where(wrap, 0, j + 1), 1 - slot)

    q = q_ref[...]
    lane = lax.broadcasted_iota(I32, (rows, LANES), 1)
    rowi = lax.broadcasted_iota(I32, (rows, LANES), 0)
    same_head = (lane % n_heads) == (rowi % n_heads)
    l_i = lax.broadcasted_iota(I32, (LANES, LANES), 0)
    l_j = lax.broadcasted_iota(I32, (LANES, LANES), 1)
    head_eq = (l_i % n_heads) == (l_j % n_heads)

    @pl.when(j == 0)
    def _():
        fn = _dot_exact_rhs01(lfn_ref[...], (head_eq & (l_i <= l_j)).astype(BF16))[0:1, :]
        s = _dot_nt(q, kn_ref[...]) - fn
        keep = same_head & (lane // n_heads <= rowi // n_heads) & (lane < rows)
        s = jnp.where(keep, s, NEG_INF)
        m = jnp.max(s, axis=-1, keepdims=True)
        p = jnp.exp(s - m)
        m_scr[...] = m
        l_scr[...] = jnp.sum(p, axis=-1, keepdims=True)
        acc_scr[...] = _dot_nn(p.astype(BF16), vn_ref[...])
        carry_scr[...] = jnp.zeros(carry_scr.shape, F32)

    pltpu.make_async_copy(ck_hbm.at[layer, pl.ds(0, P)], kbuf.at[slot], sem.at[slot]).wait()
    pltpu.make_async_copy(cv_hbm.at[layer, pl.ds(0, P)], vbuf.at[slot], sem.at[slot]).wait()
    pltpu.make_async_copy(lf_hbm.at[layer, pl.ds(0, P)], lfbuf.at[slot], sem.at[slot]).wait()

    x = lfbuf[slot].reshape(P * tiles, LANES)
    within = _dot_exact_rhs01(x, (head_eq & (l_i > l_j)).astype(BF16))
    tot = _dot_exact_rhs01(x, head_eq.astype(BF16))
    nr = P * tiles
    r_i = lax.broadcasted_iota(I32, (nr, nr), 0)
    c_i = lax.broadcasted_iota(I32, (nr, nr), 1)
    later_rows = ((c_i // tiles < r_i // tiles) | ((c_i // tiles == r_i // tiles) & (c_i > r_i))).astype(BF16)
    carry = carry_scr[...]
    suf = within + _dot_exact_lhs01(later_rows, tot) + carry
    carry_scr[...] = carry + jnp.sum(tot, axis=0, keepdims=True)

    k2 = kbuf[slot].reshape(-1, LANES).astype(BF16)
    v2 = vbuf[slot].reshape(-1, LANES).astype(BF16)
    s = _dot_nt(q, k2)
    blocks = [jnp.where(same_head, s[:, c * LANES:(c + 1) * LANES] + suf[c:c + 1, :], NEG_INF) for c in range(nr)]
    blk_max = blocks[0]
    for blk in blocks[1:]:
        blk_max = jnp.maximum(blk_max, blk)
    m_old = m_scr[...]
    m_new = jnp.maximum(m_old, jnp.max(blk_max, axis=-1, keepdims=True))
    alpha = jnp.exp(m_old - m_new)
    probs = [jnp.exp(blk - m_new) for blk in blocks]
    psum = probs[0]
    for pb in probs[1:]:
        psum = psum + pb
    l_scr[...] = alpha * l_scr[...] + jnp.sum(psum, axis=-1, keepdims=True)
    p_all = jnp.concatenate([pb.astype(BF16) for pb in probs], axis=1)
    acc_scr[...] = alpha * acc_scr[...] + _dot_nn(p_all, v2)
    m_scr[...] = m_new

    @pl.when(j == pl.num_programs(1) - 1)
    def _():
        o_ref[...] = acc_scr[...] / l_scr[...]


def _fox_sample_call(page_table, qf, kf, vf, logf, cache_k, cache_v, cache_logf, layer, db, t_new, n_heads, dh,
                     pages_per_step):
    assert dh == LANES
    rows = t_new * n_heads
    n_pool, page = cache_k.shape[1:3]
    n_pages = page_table.shape[1]
    tiles = page * n_heads // LANES
    P = pages_per_step
    steps = n_pages // P
    pad_rows = lambda a: jnp.pad(a.reshape(db, rows, dh), ((0, 0), (0, LANES - rows), (0, 0))).astype(BF16)
    q2 = qf.reshape(db, rows, dh).astype(BF16)
    lfn = jnp.pad(logf.reshape(db, 1, rows), ((0, 0), (0, 7), (0, LANES - rows)))
    lf_pages = cache_logf.reshape(cache_logf.shape[0], n_pool, tiles, LANES)

    per_seq = lambda r: pl.BlockSpec((None, r, LANES), lambda b, j, pt: (b, 0, 0))
    hbm = pl.BlockSpec(memory_space=pl.ANY)
    out = pl.pallas_call(
        functools.partial(_fox_sample_body, n_heads=n_heads, pages_per_step=P, layer=layer, n_pages=n_pages),
        grid_spec=pltpu.PrefetchScalarGridSpec(
            num_scalar_prefetch=1,
            grid=(db, steps),
            in_specs=[per_seq(rows), per_seq(LANES), per_seq(LANES), per_seq(8), hbm, hbm, hbm],
            out_specs=per_seq(rows),
            scratch_shapes=[pltpu.VMEM((2, P, page, n_heads, dh), F32), pltpu.VMEM((2, P, page, n_heads, dh), F32),
                            pltpu.VMEM((2, P, tiles, LANES), F32), pltpu.SemaphoreType.DMA((2,)),
                            pltpu.VMEM((rows, 1), F32), pltpu.VMEM((rows, 1), F32),
                            pltpu.VMEM((rows, LANES), F32), pltpu.VMEM((1, LANES), F32)]),
        out_shape=jax.ShapeDtypeStruct((db, rows, LANES), F32),
        compiler_params=_cparams(("arbitrary", "arbitrary")),
        name="fox_sample",
    )(page_table, q2, pad_rows(kf), pad_rows(vf), lfn, cache_k, cache_v, lf_pages)
    return out.reshape(db * t_new, n_heads * dh)


def _rms_heads(x, n_heads, dh):
    outs = []
    for h in range(n_heads):
        xs = x[:, h * dh:(h + 1) * dh]
        outs.append(xs * lax.rsqrt(jnp.mean(xs * xs, axis=-1, keepdims=True) + EPS))
    return jnp.concatenate(outs, axis=1)


def _route(logits):
    lane = lax.broadcasted_iota(I32, logits.shape, 1)
    big = jnp.int32(LANES)
    gl = jnp.where(lane < N_GROUPS, logits, NEG_INF)
    gmax = jnp.max(gl, axis=-1, keepdims=True)
    g_sel = jnp.min(jnp.where(gl == gmax, lane, big), axis=-1, keepdims=True)
    p_g = 1.0 / jnp.sum(jnp.exp(gl - gmax), axis=-1, keepdims=True)
    lo = N_GROUPS + E_PER_GROUP * g_sel
    ev = jnp.where((lane >= lo) & (lane < lo + E_PER_GROUP), logits, NEG_INF)
    v1 = jnp.max(ev, axis=-1, keepdims=True)
    i1 = jnp.min(jnp.where(ev == v1, lane, big), axis=-1, keepdims=True)
    ev2 = jnp.where(lane == i1, NEG_INF, ev)
    v2 = jnp.max(ev2, axis=-1, keepdims=True)
    i2 = jnp.min(jnp.where(ev2 == v2, lane, big), axis=-1, keepdims=True)
    e21 = jnp.exp(v2 - v1)
    w1 = p_g / (1.0 + e21)
    w2 = p_g * e21 / (1.0 + e21)
    out = jnp.where(lane == 0, (i1 - N_GROUPS).astype(F32), 0.0)
    out = jnp.where(lane == 1, (i2 - N_GROUPS).astype(F32), out)
    out = jnp.where(lane == 2, w1, out)
    out = jnp.where(lane == 3, w2, out)
    return out


def _store_slabs(ref, x):
    tm, d = x.shape
    per = d // LANES
    for s in range(per):
        ref[pl.ds(s, tm, stride=per), :] = x[:, s * LANES:(s + 1) * LANES]


def _load_slabs(ref, tm, per):
    return jnp.concatenate([ref[pl.ds(s, tm, stride=per), :] for s in range(per)], axis=1)


def _merge_body(x_ref, og_ref, gg_ref, of_ref, gw_ref, fw_ref, wo_ref, g1_ref, sh2_ref, sc2_ref, n2_ref,
                wr_hi_ref, wr_lo_ref, br_ref, *rest, n_heads, dh, n_tiles):
    x1_ref, h2_ref, route_ref = rest[-3:]
    i = pl.program_id(0)

    @pl.when(i < n_tiles)
    def _():
        og = _rms_heads(og_ref[...], n_heads, dh) * gw_ref[...] * _silu(gg_ref[...])
        of = _rms_heads(of_ref[...], n_heads, dh) * fw_ref[...]
        merged = jnp.concatenate([og, of], axis=1).astype(BF16)
        x1 = x_ref[...] + g1_ref[...] * _dot_nn(merged, wo_ref[...])
        x1_ref[...] = x1
        y = x1 * lax.rsqrt(jnp.mean(x1 * x1, axis=-1, keepdims=True) + EPS) * n2_ref[...]
        h2 = y * (1.0 + sc2_ref[...]) + sh2_ref[...]
        _store_slabs(h2_ref, h2)
        hh, hl, _ = _split3(h2)
        logits = (_dot_nn(hh, wr_hi_ref[...]) + _dot_nn(hl, wr_hi_ref[...]) + _dot_nn(hh, wr_lo_ref[...])
                  + br_ref[...])
        route_ref[...] = _route(logits)

    @pl.when(i >= n_tiles)
    def _():
        h2_ref[...] = jnp.zeros(h2_ref.shape, F32)


def _merge(x, o_gla, gla_in, o_fox, gla_w, fox_w, w_out_bf, g1, sh2, sc2, norm2_w, wr_hi, wr_lo, b_r,
           tm, rows_per_mod, n_heads, dh, slab_rows, row0, into=None):
    n, d = x.shape
    half = n_heads * dh
    blk0 = row0 // tm
    per = d // LANES
    n_tiles = n // tm
    first = into is None
    extra = -(-(slab_rows - n) // tm) if first else 0
    assert extra <= 1
    aliased = [] if first else [into]
    n_in = 14
    last = n_tiles - 1
    per_token = rows_per_mod == 1
    if per_token:
        mod_spec = pl.BlockSpec((tm, d), lambda i: (jnp.minimum(i, last), 0))
    else:
        tiles_per_mod = rows_per_mod // tm
        g1, sh2, sc2 = (a.reshape(a.shape[0], 1, d) for a in (g1, sh2, sc2))
        mod_spec = pl.BlockSpec((None, 1, d), lambda i: (jnp.minimum(i, last) // tiles_per_mod, 0, 0))
    row = lambda i: (jnp.minimum(i, last), 0)
    fixed = lambda i: (0, 0)
    return pl.pallas_call(
        functools.partial(_merge_body, n_heads=n_heads, dh=dh, n_tiles=n_tiles),
        grid=(n_tiles + extra,),
        in_specs=[pl.BlockSpec((tm, d), row),
                  pl.BlockSpec((tm, half), row),
                  pl.BlockSpec((tm, half), lambda i: (jnp.minimum(i, last), 2)),
                  pl.BlockSpec((tm, half), row),
                  pl.BlockSpec((1, half), fixed), pl.BlockSpec((1, half), fixed),
                  pl.BlockSpec((d, d), fixed),
                  mod_spec, mod_spec, mod_spec,
                  pl.BlockSpec((1, d), fixed),
                  pl.BlockSpec((d, LANES), fixed), pl.BlockSpec((d, LANES), fixed),
                  pl.BlockSpec((1, LANES), fixed)] + [pl.BlockSpec(memory_space=pl.ANY)] * len(aliased),
        out_specs=(pl.BlockSpec((tm, d), row), pl.BlockSpec((tm * per, LANES), lambda i: (blk0 + i, 0)),
                   pl.BlockSpec((tm, LANES), row)),
        out_shape=(jax.ShapeDtypeStruct((n, d), F32), jax.ShapeDtypeStruct((slab_rows * per, LANES), F32),
                   jax.ShapeDtypeStruct((n, LANES), F32)),
        input_output_aliases={n_in: 1} if aliased else {},
        compiler_params=_cparams(("arbitrary",)),
        name="merge_outproj",
    )(x, o_gla, gla_in, o_fox, gla_w.reshape(1, half), fox_w.reshape(1, half), w_out_bf, g1, sh2, sc2,
      norm2_w.reshape(1, d), wr_hi, wr_lo, b_r, *aliased)


GATHER_UNROLL = 8


def _experts_body(te_ref, first_ref, tok_ref, nt_ref, h_hbm, wg_ref, wu_ref, wd_ref, y_ref,
                  xbuf, wg_bf, wu_bf, wd_bf, sem, *, tm, per):
    i = pl.program_id(0)
    n_active = nt_ref[0]

    def gather(tile, slot):
        first = first_ref[tile]

        def issue(g, carry):
            for u in range(GATHER_UNROLL):
                r = g * GATHER_UNROLL + u
                src = pl.multiple_of(tok_ref[first + r] * per, per)
                pltpu.make_async_copy(h_hbm.at[pl.ds(src, per), :], xbuf.at[slot, pl.ds(r * per, per), :],
                                      sem.at[slot]).start(priority=1)
            return carry
        lax.fori_loop(0, tm // GATHER_UNROLL, issue, 0)

    @pl.when((i == 0) & (n_active > 0))
    def _():
        gather(0, 0)

    @pl.when(i + 1 < n_active)
    def _():
        gather(i + 1, (i + 1) % 2)

    @pl.when(i < n_active)
    def _():
        slot = i % 2
        prev = te_ref[jnp.maximum(i - 1, 0)]

        @pl.when((i == 0) | (te_ref[i] != prev))
        def _():
            wg_bf[...] = wg_ref[...].astype(BF16)
            wu_bf[...] = wu_ref[...].astype(BF16)
            wd_bf[...] = wd_ref[...].astype(BF16)

        pltpu.make_async_copy(h_hbm.at[pl.ds(0, tm * per), :], xbuf.at[slot], sem.at[slot]).wait()
        x = _load_slabs(xbuf.at[slot], tm, per).astype(BF16)
        a = (_silu(_dot_nn(x, wg_bf[...])) * _dot_nn(x, wu_bf[...])).astype(BF16)
        _store_slabs(y_ref, _dot_nn(a, wd_bf[...]))

    @pl.when(i >= n_active)
    def _():
        y_ref[...] = jnp.zeros(y_ref.shape, F32)


def _experts(tile_expert, tile_first, sorted_tok, n_tiles, h2_slabs, w_gate, w_up, w_down, tm):
    max_tiles = tile_expert.shape[0]
    _, d, dff = w_gate.shape
    per = d // LANES
    wmap = lambda i, te, first, tok, nt: (te[i], 0, 0)
    return pl.pallas_call(
        functools.partial(_experts_body, tm=tm, per=per),
        grid_spec=pltpu.PrefetchScalarGridSpec(
            num_scalar_prefetch=4,
            grid=(max_tiles,),
            in_specs=[pl.BlockSpec(memory_space=pl.ANY),
                      pl.BlockSpec((None, d, dff), wmap),
                      pl.BlockSpec((None, d, dff), wmap),
                      pl.BlockSpec((None, dff, d), wmap)],
            out_specs=pl.BlockSpec((tm * per, LANES), lambda i, te, first, tok, nt: (i, 0)),
            scratch_shapes=[pltpu.VMEM((2, tm * per, LANES), F32),
                            pltpu.VMEM((d, dff), BF16), pltpu.VMEM((d, dff), BF16), pltpu.VMEM((dff, d), BF16),
                            pltpu.SemaphoreType.DMA((2,))]),
        out_shape=jax.ShapeDtypeStruct((max_tiles * tm * per, LANES), F32),
        compiler_params=_cparams(("arbitrary",)),
        name="experts",
    )(tile_expert, tile_first, sorted_tok, n_tiles, h2_slabs, w_gate, w_up, w_down)


def _combine_body(pos_ref, y_hbm, x1_ref, route_ref, g2_ref, fw_ref, o_ref, buf, sem, *, tm, row0, per):
    i = pl.program_id(0)

    def gather(tile, slot):
        def issue(g, carry):
            for u in range(GATHER_UNROLL // 2):
                r = g * (GATHER_UNROLL // 2) + u
                tok = row0 + tile * tm + r
                for c in range(2):
                    src = pl.multiple_of(pos_ref[2 * tok + c] * per, per)
                    pltpu.make_async_copy(y_hbm.at[pl.ds(src, per), :],
                                          buf.at[slot, c, pl.ds(r * per, per), :], sem.at[slot]).start(priority=1)
            return carry
        lax.fori_loop(0, tm // (GATHER_UNROLL // 2), issue, 0)

    @pl.when(i == 0)
    def _():
        gather(0, 0)

    @pl.when(i + 1 < pl.num_programs(0))
    def _():
        gather(i + 1, (i + 1) % 2)

    slot = i % 2
    for c in range(2):
        pltpu.make_async_copy(y_hbm.at[pl.ds(0, tm * per), :], buf.at[slot, c], sem.at[slot]).wait()
    route = route_ref[...]
    moe = (route[:, 2:3] * _load_slabs(buf.at[slot, 0], tm, per)
           + route[:, 3:4] * _load_slabs(buf.at[slot, 1], tm, per))
    x2 = x1_ref[...] + g2_ref[...] * moe
    o_ref[...] = x2 * lax.rsqrt(jnp.mean(x2 * x2, axis=-1, keepdims=True) + EPS) * fw_ref[...]


def _combine(pos, y_slabs, x1, route, g2, final_w, row0, n_rows, tm, rows_per_mod):
    d = x1.shape[1]
    per = d // LANES
    per_token = rows_per_mod == 1
    if per_token:
        mod_spec = pl.BlockSpec((tm, d), lambda i, pos: (i, 0))
    else:
        tiles_per_mod = rows_per_mod // tm
        g2 = g2.reshape(g2.shape[0], 1, d)
        mod_spec = pl.BlockSpec((None, 1, d), lambda i, pos: (i // tiles_per_mod, 0, 0))
    return pl.pallas_call(
        functools.partial(_combine_body, tm=tm, row0=row0, per=per),
        grid_spec=pltpu.PrefetchScalarGridSpec(
            num_scalar_prefetch=1,
            grid=(n_rows // tm,),
            in_specs=[pl.BlockSpec(memory_space=pl.ANY),
                      pl.BlockSpec((tm, d), lambda i, pos: (i, 0)),
                      pl.BlockSpec((tm, LANES), lambda i, pos: (i, 0)),
                      mod_spec,
                      pl.BlockSpec((1, d), lambda i, pos: (0, 0))],
            out_specs=pl.BlockSpec((tm, d), lambda i, pos: (i, 0)),
            scratch_shapes=[pltpu.VMEM((2, 2, tm * per, LANES), F32), pltpu.SemaphoreType.DMA((2,))]),
        out_shape=jax.ShapeDtypeStruct((n_rows, d), F32),
        compiler_params=_cparams(("arbitrary",)),
        name="combine_norm",
    )(pos, y_slabs, x1, route, g2, final_w.reshape(1, d))


def _invert_body(where_ref, out_ref, *, n_pairs, n_out):
    def fill(g, carry):
        for u in range(GATHER_UNROLL):
            a = g * GATHER_UNROLL + u
            out_ref[where_ref[a]] = lax.shift_right_logical(a, 1)
        return carry
    lax.fori_loop(0, n_pairs // GATHER_UNROLL, fill, 0)

    def tail(t, carry):
        out_ref[n_pairs + t] = 0
        return carry
    lax.fori_loop(0, n_out - n_pairs, tail, 0)


def _sorted_tokens(where, n_out):
    n_pairs = where.shape[0]
    assert n_pairs % GATHER_UNROLL == 0
    smem = pl.BlockSpec(memory_space=pltpu.SMEM)
    return pl.pallas_call(
        functools.partial(_invert_body, n_pairs=n_pairs, n_out=n_out),
        in_specs=[smem], out_specs=smem,
        out_shape=jax.ShapeDtypeStruct((n_out,), I32),
        name="sorted_tokens",
    )(where)


def _plan(route, n_experts, tm):
    n = route.shape[0]
    e_flat = route[:, 0:2].astype(I32).reshape(-1)
    onehot = (e_flat[:, None] == jnp.arange(n_experts, dtype=I32)[None, :]).astype(I32)
    csum = jnp.cumsum(onehot, axis=0)
    counts = csum[-1]
    rank = jnp.sum(onehot * csum, axis=1) - 1
    tiles_e = (counts + tm - 1) // tm
    tile_end = jnp.cumsum(tiles_e)
    tile_off = tile_end - tiles_e
    cnt_off = jnp.cumsum(counts) - counts
    max_tiles = (2 * n) // tm + n_experts
    pos = jnp.sum(onehot * (tile_off * tm)[None, :], axis=1) + rank
    where = jnp.sum(onehot * cnt_off[None, :], axis=1) + rank
    tile_ids = jnp.arange(max_tiles, dtype=I32)
    live_ids = jnp.minimum(tile_ids, tile_end[-1] - 1)
    tile_expert = jnp.minimum(jnp.sum((tile_end[None, :] <= live_ids[:, None]).astype(I32), axis=1), n_experts - 1)
    is_e = (tile_expert[:, None] == jnp.arange(n_experts, dtype=I32)[None, :]).astype(I32)
    tile_first = jnp.sum(is_e * (cnt_off - tile_off * tm)[None, :], axis=1) + tile_ids * tm
    tile_first = jnp.clip(tile_first, 0, 2 * n)
    return (tile_expert, tile_first.astype(I32), tile_end[-1].reshape(1).astype(I32), pos.astype(I32),
            where.astype(I32))


def kernel(x_prompt, x_sample, cache_k, cache_v, cache_logf, state_gla, page_table, c_prompt, c_sample,
           norm1_w, norm2_w, w_ada, b_ada, w_in, w_gk2, b_gk, b_fgate, gla_onorm_w, fox_onorm_w, w_out,
           w_rg, b_rg, w_re, b_re, w_gate_e, w_up_e, w_down_e, final_norm_w):
    depth = w_in.shape[0]
    assert depth == 1, "single-layer trunk"
    batch, seq, d = x_prompt.shape
    db, t_new, _ = x_sample.shape
    n_hg, dk, dv = state_gla.shape[2:]
    n_hf, dh = cache_k.shape[3:]
    rank = w_gk2.shape[1]
    n_experts = w_gate_e.shape[1]
    wq = n_hg * dk
    wv = n_hg * dv
    wf = n_hf * dh
    assert wv == 1024 and wf == 1024 and 2 * wq == 1024 and rank == 16 and n_hf == 8
    n_p = batch * seq
    n_s = db * t_new
    layer = 0

    wi = w_in[layer]
    o_alr = 2 * wq + 2 * wv
    o_fox = o_alr + rank
    o_fl = o_fox + 3 * wf
    w_gla = wi[:, :o_alr].astype(BF16)
    w_fox = wi[:, o_fox:o_fl].astype(BF16)
    w_small = jnp.concatenate([wi[:, o_alr:o_fox], wi[:, o_fl:], jnp.zeros((d, LANES - rank - n_hf), F32)],
                              axis=1).astype(BF16)
    b_small = jnp.zeros((1, LANES), F32).at[0, rank:rank + n_hf].set(b_fgate[layer])
    w_gk2_pad = jnp.zeros((LANES, wq), F32).at[:rank].set(w_gk2[layer])
    b_gk2 = b_gk[layer].reshape(1, wq)
    w_out_bf = w_out[layer].astype(BF16)
    w_r = jnp.concatenate([w_rg[layer], w_re[layer],
                           jnp.zeros((d, LANES - N_GROUPS - n_experts), F32)], axis=1)
    wr_hi = w_r.astype(BF16)
    wr_lo = (w_r - wr_hi.astype(F32)).astype(BF16)
    b_r = jnp.concatenate([b_rg[layer], b_re[layer], jnp.zeros((LANES - N_GROUPS - n_experts,), F32)]).reshape(1, LANES)

    n_c = batch + db
    c_all = jnp.concatenate([c_prompt, c_sample, jnp.zeros((-n_c % 8, d), F32)], axis=0)
    mod = _adaln(c_all, w_ada[layer], b_ada[layer])
    sh1, sc1, g1, sh2, sc2, g2 = (mod[:, i * d:(i + 1) * d] for i in range(6))
    p_rows = slice(0, batch)
    rep = lambda a: jnp.repeat(a[batch:n_c], t_new, axis=0)

    q_scale = dh ** -0.5
    xp = x_prompt.reshape(n_p, d)
    xs = x_sample.reshape(n_s, d)

    tm_p = 1024
    gla_p, qf_p, kf_p, vf_p, kfb_p, vfb_p, small_p = _inproj(
        xp, sh1[p_rows], sc1[p_rows], norm1_w[layer], w_gla, w_fox, w_small, b_small, tm_p, seq, q_scale * LOG2E)
    s0_p = jnp.zeros((batch, n_hg, dk, dv), F32)
    o_gla_p, gla_state_p = _gla(gla_p, small_p, w_gk2_pad, b_gk2, s0_p, batch, seq, 128, n_hg, dk, dv)
    logf_p = small_p[:, rank:rank + n_hf]
    key_bias = _key_bias(small_p, batch, seq, rank, n_hf)
    o_fox_p = _fox_prompt(qf_p, kfb_p, vfb_p, key_bias, batch, seq, n_hf, dh, 512)
    n_all = n_p + n_s
    x1_p, h2_p, route_p = _merge(xp, o_gla_p, gla_p, o_fox_p, gla_onorm_w[layer], fox_onorm_w[layer], w_out_bf,
                                 g1[p_rows], sh2[p_rows], sc2[p_rows], norm2_w[layer], wr_hi, wr_lo, b_r,
                                 256, seq, n_hg, dv, n_all, 0)

    gla_s, qf_s, kf_s, vf_s, _, _, small_s = _inproj(
        xs, rep(sh1), rep(sc1), norm1_w[layer], w_gla, w_fox, w_small, b_small, n_s, 1, q_scale)
    t_pad = GLA_CHUNK
    pad_t = lambda a: jnp.pad(a.reshape(db, t_new, -1), ((0, 0), (0, t_pad - t_new), (0, 0))).reshape(db * t_pad, -1)
    o_gla_s_pad, gla_state_s = _gla(pad_t(gla_s), pad_t(small_s), w_gk2_pad, b_gk2, state_gla[layer],
                                    db, t_pad, t_pad, n_hg, dk, dv, t_valid=t_new)
    o_gla_s = o_gla_s_pad.reshape(db, t_pad, wv)[:, :t_new].reshape(n_s, wv)

    logf_s = small_s[:, rank:rank + n_hf]
    o_fox_s = _fox_sample_call(page_table, qf_s, kf_s, vf_s, logf_s, cache_k, cache_v, cache_logf,
                               layer, db, t_new, n_hf, dh, 16)
    x1_s, h2, route_s = _merge(xs, o_gla_s, gla_s, o_fox_s, gla_onorm_w[layer], fox_onorm_w[layer], w_out_bf,
                               rep(g1), rep(sh2), rep(sc2), norm2_w[layer], wr_hi, wr_lo, b_r,
                               n_s, 1, n_hg, dv, n_all, n_p, into=h2_p)

    tm_e = 256
    route = jnp.concatenate([route_p, route_s], axis=0)
    tile_expert, tile_first, n_tiles, pos, where = _plan(route, n_experts, tm_e)
    sorted_tok = _sorted_tokens(where, where.shape[0] + tm_e)
    y_slabs = _experts(tile_expert, tile_first, sorted_tok, n_tiles, h2, w_gate_e[layer], w_up_e[layer],
                       w_down_e[layer], tm_e)
    y_p = _combine(pos, y_slabs, x1_p, route_p, g2[p_rows], final_norm_w, 0, n_p, 256, seq)
    y_s = _combine(pos, y_slabs, x1_s, route_s, rep(g2), final_norm_w, n_p, n_s, n_s, 1)

    y_prompt = y_p.reshape(batch, seq, d)
    y_sample = y_s.reshape(db, t_new, d)
    k_prompt = kf_p.reshape(1, batch, seq, n_hf, dh)
    v_prompt = vf_p.reshape(1, batch, seq, n_hf, dh)
    logf_prompt = logf_p.reshape(1, batch, seq, n_hf)
    k_sample = kf_s.reshape(1, db, t_new, n_hf, dh)
    v_sample = vf_s.reshape(1, db, t_new, n_hf, dh)
    logf_sample = logf_s.reshape(1, db, t_new, n_hf)
    return (y_prompt, y_sample, k_prompt, v_prompt, logf_prompt, gla_state_p[None],
            k_sample, v_sample, logf_sample, gla_state_s[None])
```

```python
import functools

import jax
import jax.numpy as jnp
from jax import lax
from jax.experimental import pallas as pl
from jax.experimental.pallas import tpu as pltpu

F32 = jnp.float32
BF16 = jnp.bfloat16
I32 = jnp.int32
EPS = 1e-6
NEG_INF = float("-inf")

LANES = 128
GLA_CHUNK = 16
GLA_GATE_NORM = 16.0
N_GROUPS = 4
E_PER_GROUP = 8
VMEM_LIMIT = 56 * 1024 * 1024


def _cparams(sem):
    return pltpu.CompilerParams(dimension_semantics=sem, vmem_limit_bytes=VMEM_LIMIT)


def _log_sigmoid(z):
    return jnp.minimum(z, 0.0) - jnp.log1p(jnp.exp(-jnp.abs(z)))


def _silu(z):
    return z * jax.nn.sigmoid(z)


def _split3(a):
    hi = a.astype(BF16)
    r = a - hi.astype(F32)
    mid = r.astype(BF16)
    lo = (r - mid.astype(F32)).astype(BF16)
    return hi, mid, lo


def _dot_nn(a, b):
    return jnp.dot(a, b, preferred_element_type=F32)


def _dot_nt(a, b):
    return lax.dot_general(a, b, (((1,), (1,)), ((), ())), preferred_element_type=F32)


def _dot_f32(a, b):
    ah, al, _ = _split3(a)
    bh, bl, _ = _split3(b)
    return _dot_nn(ah, bh) + _dot_nn(al, bh) + _dot_nn(ah, bl)


def _dot_exact_rhs01(a, ones_bf16):
    hi, mid, lo = _split3(a)
    return _dot_nn(hi, ones_bf16) + _dot_nn(mid, ones_bf16) + _dot_nn(lo, ones_bf16)


def _dot_exact_lhs01(ones_bf16, b):
    hi, mid, lo = _split3(b)
    return _dot_nn(ones_bf16, hi) + _dot_nn(ones_bf16, mid) + _dot_nn(ones_bf16, lo)


def _adaln_body(c_ref, w_ref, b_ref, o_ref):
    s = _silu(c_ref[...]).astype(BF16)
    o_ref[...] = _dot_nn(s, w_ref[...].astype(BF16)) + b_ref[...]


def _adaln(c_all, w_ada, b_ada, tn=1024):
    rows, d = c_all.shape
    n6 = w_ada.shape[1]
    return pl.pallas_call(
        _adaln_body,
        grid=(n6 // tn,),
        in_specs=[pl.BlockSpec((rows, d), lambda j: (0, 0)),
                  pl.BlockSpec((d, tn), lambda j: (0, j)),
                  pl.BlockSpec((1, tn), lambda j: (0, j))],
        out_specs=pl.BlockSpec((rows, tn), lambda j: (0, j)),
        out_shape=jax.ShapeDtypeStruct((rows, n6), F32),
        compiler_params=_cparams(("arbitrary",)),
        name="adaln",
    )(c_all, w_ada, b_ada.reshape(1, n6))


def _inproj_body(x_ref, sh_ref, sc_ref, nw_ref, wa_ref, wf_ref, ws_ref, bs_ref,
                 gla_ref, qf_ref, kf_ref, vf_ref, kfb_ref, vfb_ref, small_ref, h_scr, *, q_scale, nb):
    j = pl.program_id(1)

    @pl.when(j == 0)
    def _():
        x = x_ref[...]
        y = x * lax.rsqrt(jnp.mean(x * x, axis=-1, keepdims=True) + EPS) * nw_ref[...]
        hb = (y * (1.0 + sc_ref[...]) + sh_ref[...]).astype(BF16)
        h_scr[...] = hb
        sm = _dot_nn(hb, ws_ref[...])
        lane = lax.broadcasted_iota(I32, sm.shape, 1)
        small_ref[...] = jnp.where((lane >= 16) & (lane < 24), _log_sigmoid(sm + bs_ref[...]), sm)

    @pl.when(j < 3 * nb)
    def _():
        gla_ref[...] = _dot_nn(h_scr[...], wa_ref[...])

    @pl.when((j >= 3 * nb) & (j < 4 * nb))
    def _():
        qf_ref[...] = (_dot_nn(h_scr[...], wf_ref[...]) * q_scale).astype(BF16)

    @pl.when((j >= 4 * nb) & (j < 5 * nb))
    def _():
        acc = _dot_nn(h_scr[...], wf_ref[...])
        kf_ref[...] = acc
        kfb_ref[...] = acc.astype(BF16)

    @pl.when(j >= 5 * nb)
    def _():
        acc = _dot_nn(h_scr[...], wf_ref[...])
        vf_ref[...] = acc
        vfb_ref[...] = acc.astype(BF16)


def _inproj(x, sh, sc, norm_w, w_gla, w_fox, w_small, b_small, tm, rows_per_mod, q_scale):
    n, d = x.shape
    wide = w_fox.shape[1] // 3
    tn = 1024
    nb = wide // tn
    per_token = rows_per_mod == 1
    if per_token:
        mod_spec = pl.BlockSpec((tm, d), lambda i, j: (i, 0))
    else:
        tiles_per_mod = rows_per_mod // tm
        sh = sh.reshape(sh.shape[0], 1, d)
        sc = sc.reshape(sc.shape[0], 1, d)
        mod_spec = pl.BlockSpec((None, 1, d), lambda i, j: (i // tiles_per_mod, 0, 0))
    row = lambda i, j: (i, 0)
    group = lambda g: (lambda i, j: (i, jnp.clip(j - g * nb, 0, nb - 1)))
    out_shapes = (jax.ShapeDtypeStruct((n, 3 * wide), F32),
                  jax.ShapeDtypeStruct((n, wide), BF16),
                  jax.ShapeDtypeStruct((n, wide), F32),
                  jax.ShapeDtypeStruct((n, wide), F32),
                  jax.ShapeDtypeStruct((n, wide), BF16),
                  jax.ShapeDtypeStruct((n, wide), BF16),
                  jax.ShapeDtypeStruct((n, LANES), F32))
    return pl.pallas_call(
        functools.partial(_inproj_body, q_scale=q_scale, nb=nb),
        grid=(n // tm, 6 * nb),
        in_specs=[pl.BlockSpec((tm, d), row), mod_spec, mod_spec,
                  pl.BlockSpec((1, d), lambda i, j: (0, 0)),
                  pl.BlockSpec((d, tn), lambda i, j: (0, jnp.minimum(j, 3 * nb - 1))),
                  pl.BlockSpec((d, tn), lambda i, j: (0, jnp.maximum(j - 3 * nb, 0))),
                  pl.BlockSpec((d, LANES), lambda i, j: (0, 0)),
                  pl.BlockSpec((1, LANES), lambda i, j: (0, 0))],
        out_specs=(pl.BlockSpec((tm, tn), lambda i, j: (i, jnp.minimum(j, 3 * nb - 1))),
                   pl.BlockSpec((tm, tn), group(3)), pl.BlockSpec((tm, tn), group(4)),
                   pl.BlockSpec((tm, tn), group(5)), pl.BlockSpec((tm, tn), group(4)),
                   pl.BlockSpec((tm, tn), group(5)),
                   pl.BlockSpec((tm, LANES), row)),
        out_shape=out_shapes,
        scratch_shapes=[pltpu.VMEM((tm, d), BF16)],
        compiler_params=_cparams(("arbitrary", "arbitrary")),
        name="inproj",
    )(x, sh, sc, norm_w.reshape(1, d), w_gla, w_fox, w_small, b_small)


def _level_ref(b, half, rows_i):
    tt, n = b.shape
    if half >= 4:
        size = 2 * half
        blocks = [jnp.broadcast_to(b[g * size + half - 1:g * size + half, :], (size, n)) for g in range(tt // size)]
        return blocks[0] if len(blocks) == 1 else jnp.concatenate(blocks, axis=0)
    if half == 2:
        lo = jnp.concatenate([jnp.broadcast_to(b[8 * g + 1:8 * g + 2, :], (8, n)) for g in range(tt // 8)], axis=0)
        hi = jnp.concatenate([jnp.broadcast_to(b[8 * g + 5:8 * g + 6, :], (8, n)) for g in range(tt // 8)], axis=0)
        return jnp.where(rows_i % 8 < 4, lo, hi)
    return jnp.where(rows_i % 2 == 1, pltpu.roll(b, 1, 0), b)


def _gla_body(q_ref, k_ref, v_ref, sm_ref, w2_ref, b2_ref, s0_ref, o_ref, sout_ref, st_scr,
              *, tt, dk, dv, n_pairs, t_valid):
    t = pl.program_id(1)
    zero_blk = jnp.zeros((dv, dk), F32)

    @pl.when(t == 0)
    def _():
        for p in range(n_pairs):
            top = jnp.concatenate([s0_ref[2 * p].T, zero_blk], axis=1)
            bot = jnp.concatenate([zero_blk, s0_ref[2 * p + 1].T], axis=1)
            st_scr[p] = jnp.concatenate([top, bot], axis=0)

    gk = _log_sigmoid(_dot_f32(sm_ref[...], w2_ref[...]) + b2_ref[...]) / GLA_GATE_NORM
    rows_i = lax.broadcasted_iota(I32, (tt, 1), 0)
    if t_valid < tt:
        gk = jnp.where(rows_i < t_valid, gk, 0.0)
    r_i = lax.broadcasted_iota(I32, (tt, tt), 0)
    c_i = lax.broadcasted_iota(I32, (tt, tt), 1)
    b_all = _dot_exact_lhs01((c_i <= r_i).astype(BF16), gk)

    halves = []
    half = tt // 2
    while half >= 1:
        halves.append(half)
        half //= 2
    refs = [_level_ref(b_all, hf, rows_i) for hf in halves]
    valid = [(r_i // (2 * hf) == c_i // (2 * hf)) & (r_i % (2 * hf) >= hf) & (c_i % (2 * hf) < hf) for hf in halves]
    on_diag = r_i == c_i
    head0 = lax.broadcasted_iota(I32, (tt, 2 * dk), 1) < dk
    rr = lax.broadcasted_iota(I32, (2 * dv, 2 * dk), 0) // dv
    cc = lax.broadcasted_iota(I32, (2 * dv, 2 * dk), 1) // dk
    diag = (rr == cc).astype(F32)

    def head_scores(qx, kx):
        stacked = jnp.concatenate([jnp.where(head0, qx, 0.0), jnp.where(head0, 0.0, qx)], axis=0).astype(BF16)
        return _dot_nt(stacked, kx.astype(BF16))

    for p in range(n_pairs):
        cs = slice(p * 2 * dk, (p + 1) * 2 * dk)
        vs = slice(p * 2 * dv, (p + 1) * 2 * dv)
        q = q_ref[:, cs] * (dk ** -0.5)
        k = k_ref[:, cs]
        v = v_ref[:, vs]
        b = b_all[:, cs]
        sc = head_scores(q, k)
        a0 = jnp.where(on_diag, sc[:tt], 0.0)
        a1 = jnp.where(on_diag, sc[tt:], 0.0)
        for ref_all, ok in zip(refs, valid):
            ref = ref_all[:, cs]
            sc = head_scores(q * jnp.exp(jnp.minimum(b - ref, 0.0)), k * jnp.exp(jnp.minimum(ref - b, 0.0)))
            a0 = jnp.where(ok, sc[:tt], a0)
            a1 = jnp.where(ok, sc[tt:], a1)
        vb = v.astype(BF16)
        o_intra = jnp.concatenate([_dot_nn(a0.astype(BF16), vb[:, :dv]), _dot_nn(a1.astype(BF16), vb[:, dv:])], axis=1)

        st = st_scr[p]
        bend = b[tt - 1:tt]
        o_inter = _dot_nt((q * jnp.exp(b)).astype(BF16), st.astype(BF16))
        kdec = (k * jnp.exp(bend - b)).astype(BF16)
        st = st * jnp.exp(bend) + _dot_nn(v.T.astype(BF16), kdec) * diag
        st_scr[p] = st
        o_ref[:, vs] = o_intra + o_inter

    @pl.when(t == pl.num_programs(1) - 1)
    def _():
        for p in range(n_pairs):
            st = st_scr[p]
            sout_ref[2 * p] = st[:dv, :dk].T
            sout_ref[2 * p + 1] = st[dv:, dk:].T


def _gla(gla_in, small, w_gk2_pad, b_gk, s0, batch, seq, tt, n_heads, dk, dv, t_valid=None):
    n = batch * seq
    nt = seq // tt
    wq = n_heads * dk
    wv = n_heads * dv
    assert wv == 2 * wq
    row = lambda b, t: b * nt + t
    return pl.pallas_call(
        functools.partial(_gla_body, tt=tt, dk=dk, dv=dv, n_pairs=n_heads // 2,
                          t_valid=tt if t_valid is None else t_valid),
        grid=(batch, nt),
        in_specs=[pl.BlockSpec((tt, wq), lambda b, t: (row(b, t), 0)),
                  pl.BlockSpec((tt, wq), lambda b, t: (row(b, t), 1)),
                  pl.BlockSpec((tt, wv), lambda b, t: (row(b, t), 1)),
                  pl.BlockSpec((tt, LANES), lambda b, t: (row(b, t), 0)),
                  pl.BlockSpec((LANES, wq), lambda b, t: (0, 0)),
                  pl.BlockSpec((1, wq), lambda b, t: (0, 0)),
                  pl.BlockSpec((None, n_heads, dk, dv), lambda b, t: (b, 0, 0, 0))],
        out_specs=(pl.BlockSpec((tt, wv), lambda b, t: (row(b, t), 0)),
                   pl.BlockSpec((None, n_heads, dk, dv), lambda b, t: (b, 0, 0, 0))),
        out_shape=(jax.ShapeDtypeStruct((n, wv), F32),
                   jax.ShapeDtypeStruct((batch, n_heads, dk, dv), F32)),
        scratch_shapes=[pltpu.VMEM((n_heads // 2, 2 * dv, 2 * dk), F32)],
        compiler_params=_cparams(("arbitrary", "arbitrary")),
        name="gla",
    )(gla_in, gla_in, gla_in, small, w_gk2_pad, b_gk, s0)


LOG2E = 1.4426950408889634
BIAS_PARTS = 3


def _key_bias_body(sm_ref, o_ref, *, lane0, n_heads, blk):
    s = sm_ref.shape[0]
    r_i = lax.broadcasted_iota(I32, (blk, blk), 0)
    c_i = lax.broadcasted_iota(I32, (blk, blk), 1)
    lower = (c_i <= r_i).astype(BF16)
    head = r_i - lane0
    sel = [((head >= 0) & (head < n_heads) & (c_i == BIAS_PARTS * head + part)).astype(BF16)
           for part in range(BIAS_PARTS)]
    carry = jnp.zeros((1, sm_ref.shape[1]), F32)
    for i in range(s // blk):
        f = _dot_exact_lhs01(lower, sm_ref[i * blk:(i + 1) * blk, :]) + carry
        carry = f[blk - 1:blk, :]
        pieces = _split3(f * (-LOG2E))
        placed = sum(_dot_nn(piece, sel_p) for piece, sel_p in zip(pieces, sel))
        o_ref[i * blk:(i + 1) * blk, :] = placed.astype(BF16)


def _key_bias(small, batch, seq, lane0, n_heads):
    return pl.pallas_call(
        functools.partial(_key_bias_body, lane0=lane0, n_heads=n_heads, blk=LANES),
        grid=(batch,),
        in_specs=[pl.BlockSpec((seq, LANES), lambda i: (i, 0))],
        out_specs=pl.BlockSpec((seq, LANES), lambda i: (i, 0)),
        out_shape=jax.ShapeDtypeStruct((batch * seq, LANES), BF16),
        compiler_params=_cparams(("arbitrary",)),
        name="fox_key_bias",
    )(small)


def _fox_prompt_body(q_ref, k_ref, v_ref, kb_ref, o_ref, m_scr, l_scr, acc_scr, *, n_heads, dh):
    qi = pl.program_id(1)
    ki = pl.program_id(2)
    tq = q_ref.shape[0]
    tk = k_ref.shape[0]

    @pl.when(ki == 0)
    def _():
        m_scr[...] = jnp.full(m_scr.shape, NEG_INF, F32)
        l_scr[...] = jnp.zeros(l_scr.shape, F32)
        acc_scr[...] = jnp.zeros(acc_scr.shape, F32)

    def step(masked):
        lane = lax.broadcasted_iota(I32, (tq, LANES), 1)
        kb = kb_ref[...]
        if masked:
            keep = lax.broadcasted_iota(I32, (tk, tq), 0) <= lax.broadcasted_iota(I32, (tk, tq), 1)
        for h in range(n_heads):
            hs = slice(h * dh, (h + 1) * dh)
            ones = ((lane >= BIAS_PARTS * h) & (lane < BIAS_PARTS * (h + 1))).astype(BF16)
            q_aug = jnp.concatenate([q_ref[:, hs], ones], axis=1)
            k_aug = jnp.concatenate([k_ref[:, hs], kb], axis=1)
            s_t = _dot_nt(k_aug, q_aug)
            if masked:
                s_t = jnp.where(keep, s_t, NEG_INF)
            m_old = m_scr[h]
            m_new = jnp.maximum(m_old, jnp.max(s_t, axis=0, keepdims=True))
            alpha = jnp.exp2(m_old - m_new)
            p_t = jnp.exp2(s_t - m_new)
            l_scr[h] = alpha * l_scr[h] + jnp.sum(p_t, axis=0, keepdims=True)
            acc_scr[h] = alpha * acc_scr[h] + _dot_nn(v_ref[:, hs].T, p_t.astype(BF16))
            m_scr[h] = m_new

    @pl.when(ki < qi)
    def _():
        step(False)

    @pl.when(ki == qi)
    def _():
        step(True)

    @pl.when(ki == pl.num_programs(2) - 1)
    def _():
        for h in range(n_heads):
            o_ref[:, h * dh:(h + 1) * dh] = (acc_scr[h] / l_scr[h]).T


def _fox_prompt(qf, kfb, vfb, key_bias, batch, seq, n_heads, dh, tq):
    nq = seq // tq
    width = n_heads * dh
    kv_map = lambda b, qi, ki: (b * nq + jnp.minimum(ki, qi), 0)
    return pl.pallas_call(
        functools.partial(_fox_prompt_body, n_heads=n_heads, dh=dh),
        grid=(batch, nq, nq),
        in_specs=[pl.BlockSpec((tq, width), lambda b, qi, ki: (b * nq + qi, 0)),
                  pl.BlockSpec((tq, width), kv_map),
                  pl.BlockSpec((tq, width), kv_map),
                  pl.BlockSpec((tq, LANES), kv_map)],
        out_specs=pl.BlockSpec((tq, width), lambda b, qi, ki: (b * nq + qi, 0)),
        out_shape=jax.ShapeDtypeStruct((batch * seq, width), F32),
        scratch_shapes=[pltpu.VMEM((n_heads, 1, tq), F32), pltpu.VMEM((n_heads, 1, tq), F32),
                        pltpu.VMEM((n_heads, dh, tq), F32)],
        compiler_params=_cparams(("arbitrary", "arbitrary", "arbitrary")),
        name="fox_prompt",
    )(qf, kfb, vfb, key_bias)


def _fox_sample_body(pt_ref, q_ref, kn_ref, vn_ref, lfn_ref, ck_hbm, cv_hbm, lf_hbm, o_ref,
                     kbuf, vbuf, lfbuf, sem, m_scr, l_scr, acc_scr, carry_scr,
                     *, n_heads, pages_per_step, layer, n_pages):
    P = pages_per_step
    b = pl.program_id(0)
    j = pl.program_id(1)
    steps = pl.num_programs(1)
    rows = q_ref.shape[0]
    tiles = lfbuf.shape[2]

    def fetch(seq, stp, slot):
        for i in range(P):
            pg = pt_ref[seq, n_pages - 1 - (stp * P + i)]
            pltpu.make_async_copy(ck_hbm.at[layer, pg], kbuf.at[slot, i], sem.at[slot]).start()
            pltpu.make_async_copy(cv_hbm.at[layer, pg], vbuf.at[slot, i], sem.at[slot]).start()
            pltpu.make_async_copy(lf_hbm.at[layer, pg], lfbuf.at[slot, i], sem.at[slot]).start()

    step = b * steps + j
    slot = step % 2

    @pl.when(step == 0)
    def _():
        fetch(0, 0, 0)

    @pl.when(step + 1 < pl.num_programs(0) * steps)
    def _():
        wrap = j + 1 == steps
        fetch(jnp.where(wrap, b + 1, b), jnp.where(wrap, 0, j + 1), 1 - slot)

    q = q_ref[...]
    lane = lax.broadcasted_iota(I32, (rows, LANES), 1)
    rowi = lax.broadcasted_iota(I32, (rows, LANES), 0)
    same_head = (lane % n_heads) == (rowi % n_heads)
    l_i = lax.broadcasted_iota(I32, (LANES, LANES), 0)
    l_j = lax.broadcasted_iota(I32, (LANES, LANES), 1)
    head_eq = (l_i % n_heads) == (l_j % n_heads)

    @pl.when(j == 0)
    def _():
        fn = _dot_exact_rhs01(lfn_ref[...], (head_eq & (l_i <= l_j)).astype(BF16))[0:1, :]
        s = _dot_nt(q, kn_ref[...]) - fn
        keep = same_head & (lane // n_heads <= rowi // n_heads) & (lane < rows)
        s = jnp.where(keep, s, NEG_INF)
        m = jnp.max(s, axis=-1, keepdims=True)
        p = jnp.exp(s - m)
        m_scr[...] = m
        l_scr[...] = jnp.sum(p, axis=-1, keepdims=True)
        acc_scr[...] = _dot_nn(p.astype(BF16), vn_ref[...])
        carry_scr[...] = jnp.zeros(carry_scr.shape, F32)

    pltpu.make_async_copy(ck_hbm.at[layer, pl.ds(0, P)], kbuf.at[slot], sem.at[slot]).wait()
    pltpu.make_async_copy(cv_hbm.at[layer, pl.ds(0, P)], vbuf.at[slot], sem.at[slot]).wait()
    pltpu.make_async_copy(lf_hbm.at[layer, pl.ds(0, P)], lfbuf.at[slot], sem.at[slot]).wait()

    x = lfbuf[slot].reshape(P * tiles, LANES)
    within = _dot_exact_rhs01(x, (head_eq & (l_i > l_j)).astype(BF16))
    tot = _dot_exact_rhs01(x, head_eq.astype(BF16))
    nr = P * tiles
    r_i = lax.broadcasted_iota(I32, (nr, nr), 0)
    c_i = lax.broadcasted_iota(I32, (nr, nr), 1)
    later_rows = ((c_i // tiles < r_i // tiles) | ((c_i // tiles == r_i // tiles) & (c_i > r_i))).astype(BF16)
    carry = carry_scr[...]
    suf = within + _dot_exact_lhs01(later_rows, tot) + carry
    carry_scr[...] = carry + jnp.sum(tot, axis=0, keepdims=True)

    k2 = kbuf[slot].reshape(-1, LANES).astype(BF16)
    v2 = vbuf[slot].reshape(-1, LANES).astype(BF16)
    s = _dot_nt(q, k2)
    blocks = [jnp.where(same_head, s[:, c * LANES:(c + 1) * LANES] + suf[c:c + 1, :], NEG_INF) for c in range(nr)]
    blk_max = blocks[0]
    for blk in blocks[1:]:
        blk_max = jnp.maximum(blk_max, blk)
    m_old = m_scr[...]
    m_new = jnp.maximum(m_old, jnp.max(blk_max, axis=-1, keepdims=True))
    alpha = jnp.exp(m_old - m_new)
    probs = [jnp.exp(blk - m_new) for blk in blocks]
    psum = probs[0]
    for pb in probs[1:]:
        psum = psum + pb
    l_scr[...] = alpha * l_scr[...] + jnp.sum(psum, axis=-1, keepdims=True)
    p_all = jnp.concatenate([pb.astype(BF16) for pb in probs], axis=1)
    acc_scr[...] = alpha * acc_scr[...] + _dot_nn(p_all, v2)
    m_scr[...] = m_new

    @pl.when(j == pl.num_programs(1) - 1)
    def _():
        o_ref[...] = acc_scr[...] / l_scr[...]


def _fox_sample_call(page_table, qf, kf, vf, logf, cache_k, cache_v, cache_logf, layer, db, t_new, n_heads, dh,
                     pages_per_step):
    assert dh == LANES
    rows = t_new * n_heads
    n_pool, page = cache_k.shape[1:3]
    n_pages = page_table.shape[1]
    tiles = page * n_heads // LANES
    P = pages_per_step
    steps = n_pages // P
    pad_rows = lambda a: jnp.pad(a.reshape(db, rows, dh), ((0, 0), (0, LANES - rows), (0, 0))).astype(BF16)
    q2 = qf.reshape(db, rows, dh).astype(BF16)
    lfn = jnp.pad(logf.reshape(db, 1, rows), ((0, 0), (0, 7), (0, LANES - rows)))
    lf_pages = cache_logf.reshape(cache_logf.shape[0], n_pool, tiles, LANES)

    per_seq = lambda r: pl.BlockSpec((None, r, LANES), lambda b, j, pt: (b, 0, 0))
    hbm = pl.BlockSpec(memory_space=pl.ANY)
    out = pl.pallas_call(
        functools.partial(_fox_sample_body, n_heads=n_heads, pages_per_step=P, layer=layer, n_pages=n_pages),
        grid_spec=pltpu.PrefetchScalarGridSpec(
            num_scalar_prefetch=1,
            grid=(db, steps),
            in_specs=[per_seq(rows), per_seq(LANES), per_seq(LANES), per_seq(8), hbm, hbm, hbm],
            out_specs=per_seq(rows),
            scratch_shapes=[pltpu.VMEM((2, P, page, n_heads, dh), F32), pltpu.VMEM((2, P, page, n_heads, dh), F32),
                            pltpu.VMEM((2, P, tiles, LANES), F32), pltpu.SemaphoreType.DMA((2,)),
                            pltpu.VMEM((rows, 1), F32), pltpu.VMEM((rows, 1), F32),
                            pltpu.VMEM((rows, LANES), F32), pltpu.VMEM((1, LANES), F32)]),
        out_shape=jax.ShapeDtypeStruct((db, rows, LANES), F32),
        compiler_params=_cparams(("arbitrary", "arbitrary")),
        name="fox_sample",
    )(page_table, q2, pad_rows(kf), pad_rows(vf), lfn, cache_k, cache_v, lf_pages)
    return out.reshape(db * t_new, n_heads * dh)


def _rms_heads(x, n_heads, dh):
    outs = []
    for h in range(n_heads):
        xs = x[:, h * dh:(h + 1) * dh]
        outs.append(xs * lax.rsqrt(jnp.mean(xs * xs, axis=-1, keepdims=True) + EPS))
    return jnp.concatenate(outs, axis=1)


def _route(logits):
    lane = lax.broadcasted_iota(I32, logits.shape, 1)
    big = jnp.int32(LANES)
    gl = jnp.where(lane < N_GROUPS, logits, NEG_INF)
    gmax = jnp.max(gl, axis=-1, keepdims=True)
    g_sel = jnp.min(jnp.where(gl == gmax, lane, big), axis=-1, keepdims=True)
    p_g = 1.0 / jnp.sum(jnp.exp(gl - gmax), axis=-1, keepdims=True)
    lo = N_GROUPS + E_PER_GROUP * g_sel
    ev = jnp.where((lane >= lo) & (lane < lo + E_PER_GROUP), logits, NEG_INF)
    v1 = jnp.max(ev, axis=-1, keepdims=True)
    i1 = jnp.min(jnp.where(ev == v1, lane, big), axis=-1, keepdims=True)
    ev2 = jnp.where(lane == i1, NEG_INF, ev)
    v2 = jnp.max(ev2, axis=-1, keepdims=True)
    i2 = jnp.min(jnp.where(ev2 == v2, lane, big), axis=-1, keepdims=True)
    e21 = jnp.exp(v2 - v1)
    w1 = p_g / (1.0 + e21)
    w2 = p_g * e21 / (1.0 + e21)
    out = jnp.where(lane == 0, (i1 - N_GROUPS).astype(F32), 0.0)
    out = jnp.where(lane == 1, (i2 - N_GROUPS).astype(F32), out)
    out = jnp.where(lane == 2, w1, out)
    out = jnp.where(lane == 3, w2, out)
    return out


def _store_slabs(ref, x):
    tm, d = x.shape
    per = d // LANES
    for s in range(per):
        ref[pl.ds(s, tm, stride=per), :] = x[:, s * LANES:(s + 1) * LANES]


def _load_slabs(ref, tm, per):
    return jnp.concatenate([ref[pl.ds(s, tm, stride=per), :] for s in range(per)], axis=1)


def _merge_body(x_ref, og_ref, gg_ref, of_ref, gw_ref, fw_ref, wo_ref, g1_ref, sh2_ref, sc2_ref, n2_ref,
                wr_hi_ref, wr_lo_ref, br_ref, *rest, n_heads, dh, n_tiles):
    x1_ref, h2_ref, route_ref = rest[-3:]
    i = pl.program_id(0)

    @pl.when(i < n_tiles)
    def _():
        og = _rms_heads(og_ref[...], n_heads, dh) * gw_ref[...] * _silu(gg_ref[...])
        of = _rms_heads(of_ref[...], n_heads, dh) * fw_ref[...]
        merged = jnp.concatenate([og, of], axis=1).astype(BF16)
        x1 = x_ref[...] + g1_ref[...] * _dot_nn(merged, wo_ref[...])
        x1_ref[...] = x1
        y = x1 * lax.rsqrt(jnp.mean(x1 * x1, axis=-1, keepdims=True) + EPS) * n2_ref[...]
        h2 = y * (1.0 + sc2_ref[...]) + sh2_ref[...]
        _store_slabs(h2_ref, h2)
        hh, hl, _ = _split3(h2)
        logits = (_dot_nn(hh, wr_hi_ref[...]) + _dot_nn(hl, wr_hi_ref[...]) + _dot_nn(hh, wr_lo_ref[...])
                  + br_ref[...])
        route_ref[...] = _route(logits)

    @pl.when(i >= n_tiles)
    def _():
        h2_ref[...] = jnp.zeros(h2_ref.shape, F32)


def _merge(x, o_gla, gla_in, o_fox, gla_w, fox_w, w_out_bf, g1, sh2, sc2, norm2_w, wr_hi, wr_lo, b_r,
           tm, rows_per_mod, n_heads, dh, slab_rows, row0, into=None):
    n, d = x.shape
    half = n_heads * dh
    blk0 = row0 // tm
    per = d // LANES
    n_tiles = n // tm
    first = into is None
    extra = -(-(slab_rows - n) // tm) if first else 0
    assert extra <= 1
    aliased = [] if first else [into]
    n_in = 14
    last = n_tiles - 1
    per_token = rows_per_mod == 1
    if per_token:
        mod_spec = pl.BlockSpec((tm, d), lambda i: (jnp.minimum(i, last), 0))
    else:
        tiles_per_mod = rows_per_mod // tm
        g1, sh2, sc2 = (a.reshape(a.shape[0], 1, d) for a in (g1, sh2, sc2))
        mod_spec = pl.BlockSpec((None, 1, d), lambda i: (jnp.minimum(i, last) // tiles_per_mod, 0, 0))
    row = lambda i: (jnp.minimum(i, last), 0)
    fixed = lambda i: (0, 0)
    return pl.pallas_call(
        functools.partial(_merge_body, n_heads=n_heads, dh=dh, n_tiles=n_tiles),
        grid=(n_tiles + extra,),
        in_specs=[pl.BlockSpec((tm, d), row),
                  pl.BlockSpec((tm, half), row),
                  pl.BlockSpec((tm, half), lambda i: (jnp.minimum(i, last), 2)),
                  pl.BlockSpec((tm, half), row),
                  pl.BlockSpec((1, half), fixed), pl.BlockSpec((1, half), fixed),
                  pl.BlockSpec((d, d), fixed),
                  mod_spec, mod_spec, mod_spec,
                  pl.BlockSpec((1, d), fixed),
                  pl.BlockSpec((d, LANES), fixed), pl.BlockSpec((d, LANES), fixed),
                  pl.BlockSpec((1, LANES), fixed)] + [pl.BlockSpec(memory_space=pl.ANY)] * len(aliased),
        out_specs=(pl.BlockSpec((tm, d), row), pl.BlockSpec((tm * per, LANES), lambda i: (blk0 + i, 0)),
                   pl.BlockSpec((tm, LANES), row)),
        out_shape=(jax.ShapeDtypeStruct((n, d), F32), jax.ShapeDtypeStruct((slab_rows * per, LANES), F32),
                   jax.ShapeDtypeStruct((n, LANES), F32)),
        input_output_aliases={n_in: 1} if aliased else {},
        compiler_params=_cparams(("arbitrary",)),
        name="merge_outproj",
    )(x, o_gla, gla_in, o_fox, gla_w.reshape(1, half), fox_w.reshape(1, half), w_out_bf, g1, sh2, sc2,
      norm2_w.reshape(1, d), wr_hi, wr_lo, b_r, *aliased)


GATHER_UNROLL = 8


def _experts_body(base_ref, first_ref, tok_ref, h_hbm, wg_ref, wu_ref, wd_ref, y_hbm,
                  xbuf, ybuf, wg_bf, wu_bf, wd_bf, gsem, ysem, zsem, *, tc, per, max_chunks):
    e = pl.program_id(0)
    n_exp = pl.num_programs(0)
    total = base_ref[n_exp]
    rows = tc * per

    def gather(g, slot):
        first = first_ref[g]

        def issue(k, carry):
            for u in range(GATHER_UNROLL):
                r = k * GATHER_UNROLL + u
                src = pl.multiple_of(tok_ref[first + r] * per, per)
                pltpu.make_async_copy(h_hbm.at[pl.ds(src, per), :], xbuf.at[slot, pl.ds(r * per, per), :],
                                      gsem.at[slot]).start()
            return carry
        lax.fori_loop(0, tc // GATHER_UNROLL, issue, 0)

    def y_copy(g, slot):
        dst = pl.multiple_of(g * rows, rows)
        return pltpu.make_async_copy(ybuf.at[slot], y_hbm.at[pl.ds(dst, rows), :], ysem.at[slot])

    @pl.when((e == 0) & (total > 0))
    def _():
        gather(0, 0)

    @pl.when(base_ref[e + 1] > base_ref[e])
    def _():
        wg_bf[...] = wg_ref[...].astype(BF16)
        wu_bf[...] = wu_ref[...].astype(BF16)
        wd_bf[...] = wd_ref[...].astype(BF16)

    def chunk(g, carry):
        slot = g % 2

        @pl.when(g + 1 < total)
        def _():
            gather(g + 1, 1 - slot)

        pltpu.make_async_copy(h_hbm.at[pl.ds(0, rows), :], xbuf.at[slot], gsem.at[slot]).wait()
        x = _load_slabs(xbuf.at[slot], tc, per).astype(BF16)
        a = (_silu(_dot_nn(x, wg_bf[...])) * _dot_nn(x, wu_bf[...])).astype(BF16)
        y = _dot_nn(a, wd_bf[...])

        @pl.when(g >= 2)
        def _():
            y_copy(g - 2, slot).wait()

        _store_slabs(ybuf.at[slot], y)
        y_copy(g, slot).start()
        return carry
    lax.fori_loop(base_ref[e], base_ref[e + 1], chunk, 0)

    @pl.when(e == n_exp - 1)
    def _():
        for back in (1, 2):
            @pl.when(total >= back)
            def _():
                y_copy(total - back, (total - back) % 2).wait()

        ybuf[0] = jnp.zeros((rows, LANES), F32)

        def zero_copy(g):
            dst = pl.multiple_of(g * rows, rows)
            return pltpu.make_async_copy(ybuf.at[0], y_hbm.at[pl.ds(dst, rows), :], zsem)

        def zstart(g, carry):
            zero_copy(g).start()
            return carry
        lax.fori_loop(total, max_chunks, zstart, 0)

        def zwait(g, carry):
            zero_copy(g).wait()
            return carry
        lax.fori_loop(total, max_chunks, zwait, 0)


def _experts(chunk_base, chunk_first, sorted_tok, h2_slabs, w_gate, w_up, w_down, tc):
    max_chunks = chunk_first.shape[0]
    n_exp, d, dff = w_gate.shape
    per = d // LANES
    wmap = lambda e, base, first, tok: (e, 0, 0)
    return pl.pallas_call(
        functools.partial(_experts_body, tc=tc, per=per, max_chunks=max_chunks),
        grid_spec=pltpu.PrefetchScalarGridSpec(
            num_scalar_prefetch=3,
            grid=(n_exp,),
            in_specs=[pl.BlockSpec(memory_space=pl.ANY),
                      pl.BlockSpec((None, d, dff), wmap),
                      pl.BlockSpec((None, d, dff), wmap),
                      pl.BlockSpec((None, dff, d), wmap)],
            out_specs=pl.BlockSpec(memory_space=pl.ANY),
            scratch_shapes=[pltpu.VMEM((2, tc * per, LANES), F32), pltpu.VMEM((2, tc * per, LANES), F32),
                            pltpu.VMEM((d, dff), BF16), pltpu.VMEM((d, dff), BF16), pltpu.VMEM((dff, d), BF16),
                            pltpu.SemaphoreType.DMA((2,)), pltpu.SemaphoreType.DMA((2,)),
                            pltpu.SemaphoreType.DMA(())]),
        out_shape=jax.ShapeDtypeStruct((max_chunks * tc * per, LANES), F32),
        compiler_params=_cparams(("arbitrary",)),
        name="experts",
    )(chunk_base, chunk_first, sorted_tok, h2_slabs, w_gate, w_up, w_down)


def _combine_body(pos_ref, y_hbm, x1_ref, route_ref, g2_ref, fw_ref, o_ref, buf, sem, *, tm, row0, per):
    i = pl.program_id(0)

    def gather(tile, slot):
        def issue(g, carry):
            for u in range(GATHER_UNROLL // 2):
                r = g * (GATHER_UNROLL // 2) + u
                tok = row0 + tile * tm + r
                for c in range(2):
                    src = pl.multiple_of(pos_ref[2 * tok + c] * per, per)
                    pltpu.make_async_copy(y_hbm.at[pl.ds(src, per), :],
                                          buf.at[slot, c, pl.ds(r * per, per), :], sem.at[slot]).start(priority=1)
            return carry
        lax.fori_loop(0, tm // (GATHER_UNROLL // 2), issue, 0)

    @pl.when(i == 0)
    def _():
        gather(0, 0)

    @pl.when(i + 1 < pl.num_programs(0))
    def _():
        gather(i + 1, (i + 1) % 2)

    slot = i % 2
    for c in range(2):
        pltpu.make_async_copy(y_hbm.at[pl.ds(0, tm * per), :], buf.at[slot, c], sem.at[slot]).wait()
    route = route_ref[...]
    moe = (route[:, 2:3] * _load_slabs(buf.at[slot, 0], tm, per)
           + route[:, 3:4] * _load_slabs(buf.at[slot, 1], tm, per))
    x2 = x1_ref[...] + g2_ref[...] * moe
    o_ref[...] = x2 * lax.rsqrt(jnp.mean(x2 * x2, axis=-1, keepdims=True) + EPS) * fw_ref[...]


def _combine(pos, y_slabs, x1, route, g2, final_w, row0, n_rows, tm, rows_per_mod):
    d = x1.shape[1]
    per = d // LANES
    per_token = rows_per_mod == 1
    if per_token:
        mod_spec = pl.BlockSpec((tm, d), lambda i, pos: (i, 0))
    else:
        tiles_per_mod = rows_per_mod // tm
        g2 = g2.reshape(g2.shape[0], 1, d)
        mod_spec = pl.BlockSpec((None, 1, d), lambda i, pos: (i // tiles_per_mod, 0, 0))
    return pl.pallas_call(
        functools.partial(_combine_body, tm=tm, row0=row0, per=per),
        grid_spec=pltpu.PrefetchScalarGridSpec(
            num_scalar_prefetch=1,
            grid=(n_rows // tm,),
            in_specs=[pl.BlockSpec(memory_space=pl.ANY),
                      pl.BlockSpec((tm, d), lambda i, pos: (i, 0)),
                      pl.BlockSpec((tm, LANES), lambda i, pos: (i, 0)),
                      mod_spec,
                      pl.BlockSpec((1, d), lambda i, pos: (0, 0))],
            out_specs=pl.BlockSpec((tm, d), lambda i, pos: (i, 0)),
            scratch_shapes=[pltpu.VMEM((2, 2, tm * per, LANES), F32), pltpu.SemaphoreType.DMA((2,))]),
        out_shape=jax.ShapeDtypeStruct((n_rows, d), F32),
        compiler_params=_cparams(("arbitrary",)),
        name="combine_norm",
    )(pos, y_slabs, x1, route, g2, final_w.reshape(1, d))


def _invert_body(where_ref, out_ref, *, n_pairs, n_out):
    def fill(g, carry):
        for u in range(GATHER_UNROLL):
            a = g * GATHER_UNROLL + u
            out_ref[where_ref[a]] = lax.shift_right_logical(a, 1)
        return carry
    lax.fori_loop(0, n_pairs // GATHER_UNROLL, fill, 0)

    def tail(t, carry):
        out_ref[n_pairs + t] = 0
        return carry
    lax.fori_loop(0, n_out - n_pairs, tail, 0)


def _sorted_tokens(where, n_out):
    n_pairs = where.shape[0]
    assert n_pairs % GATHER_UNROLL == 0
    smem = pl.BlockSpec(memory_space=pltpu.SMEM)
    return pl.pallas_call(
        functools.partial(_invert_body, n_pairs=n_pairs, n_out=n_out),
        in_specs=[smem], out_specs=smem,
        out_shape=jax.ShapeDtypeStruct((n_out,), I32),
        name="sorted_tokens",
    )(where)


def _plan(route, n_experts, tm):
    n = route.shape[0]
    e_flat = route[:, 0:2].astype(I32).reshape(-1)
    onehot = (e_flat[:, None] == jnp.arange(n_experts, dtype=I32)[None, :]).astype(I32)
    csum = jnp.cumsum(onehot, axis=0)
    counts = csum[-1]
    rank = jnp.sum(onehot * csum, axis=1) - 1
    tiles_e = (counts + tm - 1) // tm
    tile_end = jnp.cumsum(tiles_e)
    tile_off = tile_end - tiles_e
    cnt_off = jnp.cumsum(counts) - counts
    max_tiles = (2 * n) // tm + n_experts
    pos = jnp.sum(onehot * (tile_off * tm)[None, :], axis=1) + rank
    where = jnp.sum(onehot * cnt_off[None, :], axis=1) + rank
    tile_ids = jnp.arange(max_tiles, dtype=I32)
    tile_expert = jnp.minimum(jnp.sum((tile_end[None, :] <= tile_ids[:, None]).astype(I32), axis=1), n_experts - 1)
    is_e = (tile_expert[:, None] == jnp.arange(n_experts, dtype=I32)[None, :]).astype(I32)
    chunk_first = jnp.sum(is_e * (cnt_off - tile_off * tm)[None, :], axis=1) + tile_ids * tm
    chunk_first = jnp.clip(chunk_first, 0, 2 * n)
    chunk_base = jnp.concatenate([tile_off, tile_end[-1:]])
    return chunk_base.astype(I32), chunk_first.astype(I32), pos.astype(I32), where.astype(I32)


def kernel(x_prompt, x_sample, cache_k, cache_v, cache_logf, state_gla, page_table, c_prompt, c_sample,
           norm1_w, norm2_w, w_ada, b_ada, w_in, w_gk2, b_gk, b_fgate, gla_onorm_w, fox_onorm_w, w_out,
           w_rg, b_rg, w_re, b_re, w_gate_e, w_up_e, w_down_e, final_norm_w):
    depth = w_in.shape[0]
    assert depth == 1, "single-layer trunk"
    batch, seq, d = x_prompt.shape
    db, t_new, _ = x_sample.shape
    n_hg, dk, dv = state_gla.shape[2:]
    n_hf, dh = cache_k.shape[3:]
    rank = w_gk2.shape[1]
    n_experts = w_gate_e.shape[1]
    wq = n_hg * dk
    wv = n_hg * dv
    wf = n_hf * dh
    assert wv == 1024 and wf == 1024 and 2 * wq == 1024 and rank == 16 and n_hf == 8
    n_p = batch * seq
    n_s = db * t_new
    layer = 0

    wi = w_in[layer]
    o_alr = 2 * wq + 2 * wv
    o_fox = o_alr + rank
    o_fl = o_fox + 3 * wf
    w_gla = wi[:, :o_alr].astype(BF16)
    w_fox = wi[:, o_fox:o_fl].astype(BF16)
    w_small = jnp.concatenate([wi[:, o_alr:o_fox], wi[:, o_fl:], jnp.zeros((d, LANES - rank - n_hf), F32)],
                              axis=1).astype(BF16)
    b_small = jnp.zeros((1, LANES), F32).at[0, rank:rank + n_hf].set(b_fgate[layer])
    w_gk2_pad = jnp.zeros((LANES, wq), F32).at[:rank].set(w_gk2[layer])
    b_gk2 = b_gk[layer].reshape(1, wq)
    w_out_bf = w_out[layer].astype(BF16)
    w_r = jnp.concatenate([w_rg[layer], w_re[layer],
                           jnp.zeros((d, LANES - N_GROUPS - n_experts), F32)], axis=1)
    wr_hi = w_r.astype(BF16)
    wr_lo = (w_r - wr_hi.astype(F32)).astype(BF16)
    b_r = jnp.concatenate([b_rg[layer], b_re[layer], jnp.zeros((LANES - N_GROUPS - n_experts,), F32)]).reshape(1, LANES)

    n_c = batch + db
    c_all = jnp.concatenate([c_prompt, c_sample, jnp.zeros((-n_c % 8, d), F32)], axis=0)
    mod = _adaln(c_all, w_ada[layer], b_ada[layer])
    sh1, sc1, g1, sh2, sc2, g2 = (mod[:, i * d:(i + 1) * d] for i in range(6))
    p_rows = slice(0, batch)
    rep = lambda a: jnp.repeat(a[batch:n_c], t_new, axis=0)

    q_scale = dh ** -0.5
    xp = x_prompt.reshape(n_p, d)
    xs = x_sample.reshape(n_s, d)

    tm_p = 512
    gla_p, qf_p, kf_p, vf_p, kfb_p, vfb_p, small_p = _inproj(
        xp, sh1[p_rows], sc1[p_rows], norm1_w[layer], w_gla, w_fox, w_small, b_small, tm_p, seq, q_scale * LOG2E)
    s0_p = jnp.zeros((batch, n_hg, dk, dv), F32)
    o_gla_p, gla_state_p = _gla(gla_p, small_p, w_gk2_pad, b_gk2, s0_p, batch, seq, 128, n_hg, dk, dv)
    logf_p = small_p[:, rank:rank + n_hf]
    key_bias = _key_bias(small_p, batch, seq, rank, n_hf)
    o_fox_p = _fox_prompt(qf_p, kfb_p, vfb_p, key_bias, batch, seq, n_hf, dh, 512)
    n_all = n_p + n_s
    x1_p, h2_p, route_p = _merge(xp, o_gla_p, gla_p, o_fox_p, gla_onorm_w[layer], fox_onorm_w[layer], w_out_bf,
                                 g1[p_rows], sh2[p_rows], sc2[p_rows], norm2_w[layer], wr_hi, wr_lo, b_r,
                                 256, seq, n_hg, dv, n_all, 0)

    gla_s, qf_s, kf_s, vf_s, _, _, small_s = _inproj(
        xs, rep(sh1), rep(sc1), norm1_w[layer], w_gla, w_fox, w_small, b_small, n_s, 1, q_scale)
    t_pad = GLA_CHUNK
    pad_t = lambda a: jnp.pad(a.reshape(db, t_new, -1), ((0, 0), (0, t_pad - t_new), (0, 0))).reshape(db * t_pad, -1)
    o_gla_s_pad, gla_state_s = _gla(pad_t(gla_s), pad_t(small_s), w_gk2_pad, b_gk2, state_gla[layer],
                                    db, t_pad, t_pad, n_hg, dk, dv, t_valid=t_new)
    o_gla_s = o_gla_s_pad.reshape(db, t_pad, wv)[:, :t_new].reshape(n_s, wv)

    logf_s = small_s[:, rank:rank + n_hf]
    o_fox_s = _fox_sample_call(page_table, qf_s, kf_s, vf_s, logf_s, cache_k, cache_v, cache_logf,
                               layer, db, t_new, n_hf, dh, 8)
    x1_s, h2, route_s = _merge(xs, o_gla_s, gla_s, o_fox_s, gla_onorm_w[layer], fox_onorm_w[layer], w_out_bf,
                               rep(g1), rep(sh2), rep(sc2), norm2_w[layer], wr_hi, wr_lo, b_r,
                               n_s, 1, n_hg, dv, n_all, n_p, into=h2_p)

    tm_e = 128
    route = jnp.concatenate([route_p, route_s], axis=0)
    chunk_base, chunk_first, pos, where = _plan(route, n_experts, tm_e)
    sorted_tok = _sorted_tokens(where, where.shape[0] + tm_e)
    y_slabs = _experts(chunk_base, chunk_first, sorted_tok, h2, w_gate_e[layer], w_up_e[layer],
                       w_down_e[layer], tm_e)
    y_p = _combine(pos, y_slabs, x1_p, route_p, g2[p_rows], final_norm_w, 0, n_p, 256, seq)
    y_s = _combine(pos, y_slabs, x1_s, route_s, rep(g2), final_norm_w, n_p, n_s, n_s, 1)

    y_prompt = y_p.reshape(batch, seq, d)
    y_sample = y_s.reshape(db, t_new, d)
    k_prompt = kf_p.reshape(1, batch, seq, n_hf, dh)
    v_prompt = vf_p.reshape(1, batch, seq, n_hf, dh)
    logf_prompt = logf_p.reshape(1, batch, seq, n_hf)
    k_sample = kf_s.reshape(1, db, t_new, n_hf, dh)
    v_sample = vf_s.reshape(1, db, t_new, n_hf, dh)
    logf_sample = logf_s.reshape(1, db, t_new, n_hf)
    return (y_prompt, y_sample, k_prompt, v_prompt, logf_prompt, gla_state_p[None],
            k_sample, v_sample, logf_sample, gla_state_s[None])
```

```python
import functools

import jax
import jax.numpy as jnp
from jax import lax
from jax.experimental import pallas as pl
from jax.experimental.pallas import tpu as pltpu

F32 = jnp.float32
BF16 = jnp.bfloat16
I32 = jnp.int32
EPS = 1e-6
NEG_INF = float("-inf")

LANES = 128
GLA_CHUNK = 16
GLA_GATE_NORM = 16.0
N_GROUPS = 4
E_PER_GROUP = 8
VMEM_LIMIT = 56 * 1024 * 1024


def _cparams(sem):
    return pltpu.CompilerParams(dimension_semantics=sem, vmem_limit_bytes=VMEM_LIMIT)


def _log_sigmoid(z):
    return jnp.minimum(z, 0.0) - jnp.log1p(jnp.exp(-jnp.abs(z)))


def _silu(z):
    return z * jax.nn.sigmoid(z)


def _split3(a):
    hi = a.astype(BF16)
    r = a - hi.astype(F32)
    mid = r.astype(BF16)
    lo = (r - mid.astype(F32)).astype(BF16)
    return hi, mid, lo


def _dot_nn(a, b):
    return jnp.dot(a, b, preferred_element_type=F32)


def _dot_nt(a, b):
    return lax.dot_general(a, b, (((1,), (1,)), ((), ())), preferred_element_type=F32)


def _dot_f32(a, b):
    ah, al, _ = _split3(a)
    bh, bl, _ = _split3(b)
    return _dot_nn(ah, bh) + _dot_nn(al, bh) + _dot_nn(ah, bl)


def _dot_exact_rhs01(a, ones_bf16):
    hi, mid, lo = _split3(a)
    return _dot_nn(hi, ones_bf16) + _dot_nn(mid, ones_bf16) + _dot_nn(lo, ones_bf16)


def _dot_exact_lhs01(ones_bf16, b):
    hi, mid, lo = _split3(b)
    return _dot_nn(ones_bf16, hi) + _dot_nn(ones_bf16, mid) + _dot_nn(ones_bf16, lo)


def _adaln_body(c_ref, w_ref, b_ref, o_ref):
    s = _silu(c_ref[...]).astype(BF16)
    o_ref[...] = _dot_nn(s, w_ref[...].astype(BF16)) + b_ref[...]


def _adaln(c_all, w_ada, b_ada, tn=1024):
    rows, d = c_all.shape
    n6 = w_ada.shape[1]
    return pl.pallas_call(
        _adaln_body,
        grid=(n6 // tn,),
        in_specs=[pl.BlockSpec((rows, d), lambda j: (0, 0)),
                  pl.BlockSpec((d, tn), lambda j: (0, j)),
                  pl.BlockSpec((1, tn), lambda j: (0, j))],
        out_specs=pl.BlockSpec((rows, tn), lambda j: (0, j)),
        out_shape=jax.ShapeDtypeStruct((rows, n6), F32),
        compiler_params=_cparams(("arbitrary",)),
        name="adaln",
    )(c_all, w_ada, b_ada.reshape(1, n6))


def _inproj_body(x_ref, sh_ref, sc_ref, nw_ref, wa_ref, wf_ref, ws_ref, bs_ref,
                 gla_ref, qf_ref, kf_ref, vf_ref, kfb_ref, vfb_ref, small_ref, h_scr, *, q_scale, nb):
    j = pl.program_id(1)

    @pl.when(j == 0)
    def _():
        x = x_ref[...]
        y = x * lax.rsqrt(jnp.mean(x * x, axis=-1, keepdims=True) + EPS) * nw_ref[...]
        hb = (y * (1.0 + sc_ref[...]) + sh_ref[...]).astype(BF16)
        h_scr[...] = hb
        sm = _dot_nn(hb, ws_ref[...])
        lane = lax.broadcasted_iota(I32, sm.shape, 1)
        small_ref[...] = jnp.where((lane >= 16) & (lane < 24), _log_sigmoid(sm + bs_ref[...]), sm)

    @pl.when(j < 3 * nb)
    def _():
        gla_ref[...] = _dot_nn(h_scr[...], wa_ref[...])

    @pl.when((j >= 3 * nb) & (j < 4 * nb))
    def _():
        qf_ref[...] = (_dot_nn(h_scr[...], wf_ref[...]) * q_scale).astype(BF16)

    @pl.when((j >= 4 * nb) & (j < 5 * nb))
    def _():
        acc = _dot_nn(h_scr[...], wf_ref[...])
        kf_ref[...] = acc
        kfb_ref[...] = acc.astype(BF16)

    @pl.when(j >= 5 * nb)
    def _():
        acc = _dot_nn(h_scr[...], wf_ref[...])
        vf_ref[...] = acc
        vfb_ref[...] = acc.astype(BF16)


def _inproj(x, sh, sc, norm_w, w_gla, w_fox, w_small, b_small, tm, rows_per_mod, q_scale):
    n, d = x.shape
    wide = w_fox.shape[1] // 3
    tn = 1024
    nb = wide // tn
    per_token = rows_per_mod == 1
    if per_token:
        mod_spec = pl.BlockSpec((tm, d), lambda i, j: (i, 0))
    else:
        tiles_per_mod = rows_per_mod // tm
        sh = sh.reshape(sh.shape[0], 1, d)
        sc = sc.reshape(sc.shape[0], 1, d)
        mod_spec = pl.BlockSpec((None, 1, d), lambda i, j: (i // tiles_per_mod, 0, 0))
    row = lambda i, j: (i, 0)
    last = n // tm - 1

    def group(g):
        def imap(i, j):
            done = (j >= (g + 1) * nb) & (i < last)
            return jnp.where(done, i + 1, i), jnp.where(done, 0, jnp.clip(j - g * nb, 0, nb - 1))
        return imap

    def gla_map(i, j):
        done = (j >= 3 * nb) & (i < last)
        return jnp.where(done, i + 1, i), jnp.where(done, 0, jnp.minimum(j, 3 * nb - 1))
    out_shapes = (jax.ShapeDtypeStruct((n, 3 * wide), F32),
                  jax.ShapeDtypeStruct((n, wide), BF16),
                  jax.ShapeDtypeStruct((n, wide), F32),
                  jax.ShapeDtypeStruct((n, wide), F32),
                  jax.ShapeDtypeStruct((n, wide), BF16),
                  jax.ShapeDtypeStruct((n, wide), BF16),
                  jax.ShapeDtypeStruct((n, LANES), F32))
    return pl.pallas_call(
        functools.partial(_inproj_body, q_scale=q_scale, nb=nb),
        grid=(n // tm, 6 * nb),
        in_specs=[pl.BlockSpec((tm, d), row), mod_spec, mod_spec,
                  pl.BlockSpec((1, d), lambda i, j: (0, 0)),
                  pl.BlockSpec((d, tn), lambda i, j: (0, jnp.minimum(j, 3 * nb - 1))),
                  pl.BlockSpec((d, tn), lambda i, j: (0, jnp.where(j < 3 * nb, 3 * nb - 1, j - 3 * nb))),
                  pl.BlockSpec((d, LANES), lambda i, j: (0, 0)),
                  pl.BlockSpec((1, LANES), lambda i, j: (0, 0))],
        out_specs=(pl.BlockSpec((tm, tn), gla_map),
                   pl.BlockSpec((tm, tn), group(3)), pl.BlockSpec((tm, tn), group(4)),
                   pl.BlockSpec((tm, tn), group(5)), pl.BlockSpec((tm, tn), group(4)),
                   pl.BlockSpec((tm, tn), group(5)),
                   pl.BlockSpec((tm, LANES), row)),
        out_shape=out_shapes,
        scratch_shapes=[pltpu.VMEM((tm, d), BF16)],
        compiler_params=_cparams(("arbitrary", "arbitrary")),
        name="inproj",
    )(x, sh, sc, norm_w.reshape(1, d), w_gla, w_fox, w_small, b_small)


def _level_ref(b, half, rows_i):
    tt, n = b.shape
    if half >= 4:
        size = 2 * half
        blocks = [jnp.broadcast_to(b[g * size + half - 1:g * size + half, :], (size, n)) for g in range(tt // size)]
        return blocks[0] if len(blocks) == 1 else jnp.concatenate(blocks, axis=0)
    if half == 2:
        lo = jnp.concatenate([jnp.broadcast_to(b[8 * g + 1:8 * g + 2, :], (8, n)) for g in range(tt // 8)], axis=0)
        hi = jnp.concatenate([jnp.broadcast_to(b[8 * g + 5:8 * g + 6, :], (8, n)) for g in range(tt // 8)], axis=0)
        return jnp.where(rows_i % 8 < 4, lo, hi)
    return jnp.where(rows_i % 2 == 1, pltpu.roll(b, 1, 0), b)


def _gla_body(q_ref, k_ref, v_ref, sm_ref, w2_ref, b2_ref, s0_ref, o_ref, sout_ref, st_scr,
              *, tt, dk, dv, n_pairs, t_valid):
    t = pl.program_id(1)
    zero_blk = jnp.zeros((dv, dk), F32)

    @pl.when(t == 0)
    def _():
        for p in range(n_pairs):
            top = jnp.concatenate([s0_ref[2 * p].T, zero_blk], axis=1)
            bot = jnp.concatenate([zero_blk, s0_ref[2 * p + 1].T], axis=1)
            st_scr[p] = jnp.concatenate([top, bot], axis=0)

    gk = _log_sigmoid(_dot_f32(sm_ref[...], w2_ref[...]) + b2_ref[...]) / GLA_GATE_NORM
    rows_i = lax.broadcasted_iota(I32, (tt, 1), 0)
    if t_valid < tt:
        gk = jnp.where(rows_i < t_valid, gk, 0.0)
    r_i = lax.broadcasted_iota(I32, (tt, tt), 0)
    c_i = lax.broadcasted_iota(I32, (tt, tt), 1)
    b_all = _dot_exact_lhs01((c_i <= r_i).astype(BF16), gk)

    halves = []
    half = tt // 2
    while half >= 1:
        halves.append(half)
        half //= 2
    refs = [_level_ref(b_all, hf, rows_i) for hf in halves]
    valid = [(r_i // (2 * hf) == c_i // (2 * hf)) & (r_i % (2 * hf) >= hf) & (c_i % (2 * hf) < hf) for hf in halves]
    on_diag = r_i == c_i
    head0 = lax.broadcasted_iota(I32, (tt, 2 * dk), 1) < dk
    rr = lax.broadcasted_iota(I32, (2 * dv, 2 * dk), 0) // dv
    cc = lax.broadcasted_iota(I32, (2 * dv, 2 * dk), 1) // dk
    diag = (rr == cc).astype(F32)

    def head_scores(qx, kx):
        stacked = jnp.concatenate([jnp.where(head0, qx, 0.0), jnp.where(head0, 0.0, qx)], axis=0).astype(BF16)
        return _dot_nt(stacked, kx.astype(BF16))

    for p in range(n_pairs):
        cs = slice(p * 2 * dk, (p + 1) * 2 * dk)
        vs = slice(p * 2 * dv, (p + 1) * 2 * dv)
        q = q_ref[:, cs] * (dk ** -0.5)
        k = k_ref[:, cs]
        v = v_ref[:, vs]
        b = b_all[:, cs]
        sc = head_scores(q, k)
        a0 = jnp.where(on_diag, sc[:tt], 0.0)
        a1 = jnp.where(on_diag, sc[tt:], 0.0)
        for ref_all, ok in zip(refs, valid):
            ref = ref_all[:, cs]
            sc = head_scores(q * jnp.exp(jnp.minimum(b - ref, 0.0)), k * jnp.exp(jnp.minimum(ref - b, 0.0)))
            a0 = jnp.where(ok, sc[:tt], a0)
            a1 = jnp.where(ok, sc[tt:], a1)
        vb = v.astype(BF16)
        o_intra = jnp.concatenate([_dot_nn(a0.astype(BF16), vb[:, :dv]), _dot_nn(a1.astype(BF16), vb[:, dv:])], axis=1)

        st = st_scr[p]
        bend = b[tt - 1:tt]
        o_inter = _dot_nt((q * jnp.exp(b)).astype(BF16), st.astype(BF16))
        kdec = (k * jnp.exp(bend - b)).astype(BF16)
        st = st * jnp.exp(bend) + _dot_nn(v.T.astype(BF16), kdec) * diag
        st_scr[p] = st
        o_ref[:, vs] = o_intra + o_inter

    @pl.when(t == pl.num_programs(1) - 1)
    def _():
        for p in range(n_pairs):
            st = st_scr[p]
            sout_ref[2 * p] = st[:dv, :dk].T
            sout_ref[2 * p + 1] = st[dv:, dk:].T


def _gla(gla_in, small, w_gk2_pad, b_gk, s0, batch, seq, tt, n_heads, dk, dv, t_valid=None):
    n = batch * seq
    nt = seq // tt
    wq = n_heads * dk
    wv = n_heads * dv
    assert wv == 2 * wq
    row = lambda b, t: b * nt + t
    return pl.pallas_call(
        functools.partial(_gla_body, tt=tt, dk=dk, dv=dv, n_pairs=n_heads // 2,
                          t_valid=tt if t_valid is None else t_valid),
        grid=(batch, nt),
        in_specs=[pl.BlockSpec((tt, wq), lambda b, t: (row(b, t), 0)),
                  pl.BlockSpec((tt, wq), lambda b, t: (row(b, t), 1)),
                  pl.BlockSpec((tt, wv), lambda b, t: (row(b, t), 1)),
                  pl.BlockSpec((tt, LANES), lambda b, t: (row(b, t), 0)),
                  pl.BlockSpec((LANES, wq), lambda b, t: (0, 0)),
                  pl.BlockSpec((1, wq), lambda b, t: (0, 0)),
                  pl.BlockSpec((None, n_heads, dk, dv), lambda b, t: (b, 0, 0, 0))],
        out_specs=(pl.BlockSpec((tt, wv), lambda b, t: (row(b, t), 0)),
                   pl.BlockSpec((None, n_heads, dk, dv), lambda b, t: (b, 0, 0, 0))),
        out_shape=(jax.ShapeDtypeStruct((n, wv), F32),
                   jax.ShapeDtypeStruct((batch, n_heads, dk, dv), F32)),
        scratch_shapes=[pltpu.VMEM((n_heads // 2, 2 * dv, 2 * dk), F32)],
        compiler_params=_cparams(("arbitrary", "arbitrary")),
        name="gla",
    )(gla_in, gla_in, gla_in, small, w_gk2_pad, b_gk, s0)


LOG2E = 1.4426950408889634
BIAS_PARTS = 3


def _key_bias_body(sm_ref, o_ref, *, lane0, n_heads, blk):
    s = sm_ref.shape[0]
    r_i = lax.broadcasted_iota(I32, (blk, blk), 0)
    c_i = lax.broadcasted_iota(I32, (blk, blk), 1)
    lower = (c_i <= r_i).astype(BF16)
    head = r_i - lane0
    sel = [((head >= 0) & (head < n_heads) & (c_i == BIAS_PARTS * head + part)).astype(BF16)
           for part in range(BIAS_PARTS)]
    carry = jnp.zeros((1, sm_ref.shape[1]), F32)
    for i in range(s // blk):
        f = _dot_exact_lhs01(lower, sm_ref[i * blk:(i + 1) * blk, :]) + carry
        carry = f[blk - 1:blk, :]
        pieces = _split3(f * (-LOG2E))
        placed = sum(_dot_nn(piece, sel_p) for piece, sel_p in zip(pieces, sel))
        o_ref[i * blk:(i + 1) * blk, :] = placed.astype(BF16)


def _key_bias(small, batch, seq, lane0, n_heads):
    return pl.pallas_call(
        functools.partial(_key_bias_body, lane0=lane0, n_heads=n_heads, blk=LANES),
        grid=(batch,),
        in_specs=[pl.BlockSpec((seq, LANES), lambda i: (i, 0))],
        out_specs=pl.BlockSpec((seq, LANES), lambda i: (i, 0)),
        out_shape=jax.ShapeDtypeStruct((batch * seq, LANES), BF16),
        compiler_params=_cparams(("arbitrary",)),
        name="fox_key_bias",
    )(small)


def _fox_prompt_body(q_ref, k_ref, v_ref, kb_ref, o_ref, m_scr, l_scr, acc_scr, *, n_heads, dh):
    qi = pl.program_id(1)
    ki = pl.program_id(2)
    tq = q_ref.shape[0]
    tk = k_ref.shape[0]

    @pl.when(ki == 0)
    def _():
        m_scr[...] = jnp.full(m_scr.shape, NEG_INF, F32)
        l_scr[...] = jnp.zeros(l_scr.shape, F32)
        acc_scr[...] = jnp.zeros(acc_scr.shape, F32)

    def step(masked):
        lane = lax.broadcasted_iota(I32, (tq, LANES), 1)
        kb = kb_ref[...]
        if masked:
            keep = lax.broadcasted_iota(I32, (tk, tq), 0) <= lax.broadcasted_iota(I32, (tk, tq), 1)
        for h in range(n_heads):
            hs = slice(h * dh, (h + 1) * dh)
            ones = ((lane >= BIAS_PARTS * h) & (lane < BIAS_PARTS * (h + 1))).astype(BF16)
            q_aug = jnp.concatenate([q_ref[:, hs], ones], axis=1)
            k_aug = jnp.concatenate([k_ref[:, hs], kb], axis=1)
            s_t = _dot_nt(k_aug, q_aug)
            if masked:
                s_t = jnp.where(keep, s_t, NEG_INF)
            m_old = m_scr[h]
            m_new = jnp.maximum(m_old, jnp.max(s_t, axis=0, keepdims=True))
            alpha = jnp.exp2(m_old - m_new)
            p_t = jnp.exp2(s_t - m_new)
            l_scr[h] = alpha * l_scr[h] + jnp.sum(p_t, axis=0, keepdims=True)
            acc_scr[h] = alpha * acc_scr[h] + _dot_nn(v_ref[:, hs].T, p_t.astype(BF16))
            m_scr[h] = m_new

    @pl.when(ki < qi)
    def _():
        step(False)

    @pl.when(ki == qi)
    def _():
        step(True)

    @pl.when(ki == pl.num_programs(2) - 1)
    def _():
        for h in range(n_heads):
            o_ref[:, h * dh:(h + 1) * dh] = (acc_scr[h] / l_scr[h]).T


def _fox_prompt(qf, kfb, vfb, key_bias, batch, seq, n_heads, dh, tq):
    nq = seq // tq
    width = n_heads * dh
    kv_map = lambda b, qi, ki: (b * nq + jnp.minimum(ki, qi), 0)
    return pl.pallas_call(
        functools.partial(_fox_prompt_body, n_heads=n_heads, dh=dh),
        grid=(batch, nq, nq),
        in_specs=[pl.BlockSpec((tq, width), lambda b, qi, ki: (b * nq + qi, 0)),
                  pl.BlockSpec((tq, width), kv_map),
                  pl.BlockSpec((tq, width), kv_map),
                  pl.BlockSpec((tq, LANES), kv_map)],
        out_specs=pl.BlockSpec((tq, width), lambda b, qi, ki: (b * nq + qi, 0)),
        out_shape=jax.ShapeDtypeStruct((batch * seq, width), F32),
        scratch_shapes=[pltpu.VMEM((n_heads, 1, tq), F32), pltpu.VMEM((n_heads, 1, tq), F32),
                        pltpu.VMEM((n_heads, dh, tq), F32)],
        compiler_params=_cparams(("arbitrary", "arbitrary", "arbitrary")),
        name="fox_prompt",
    )(qf, kfb, vfb, key_bias)


def _fox_sample_body(pt_ref, q_ref, kn_ref, vn_ref, lfn_ref, ck_hbm, cv_hbm, lf_hbm, o_ref,
                     kbuf, vbuf, lfbuf, sem, m_scr, l_scr, acc_scr, carry_scr,
                     *, n_heads, pages_per_step, layer, n_pages):
    P = pages_per_step
    b = pl.program_id(0)
    j = pl.program_id(1)
    steps = pl.num_programs(1)
    rows = q_ref.shape[0]
    tiles = lfbuf.shape[2]

    def fetch(seq, stp, slot):
        for i in range(P):
            pg = pt_ref[seq, n_pages - 1 - (stp * P + i)]
            pltpu.make_async_copy(ck_hbm.at[layer, pg], kbuf.at[slot, i], sem.at[slot]).start()
            pltpu.make_async_copy(cv_hbm.at[layer, pg], vbuf.at[slot, i], sem.at[slot]).start()
            pltpu.make_async_copy(lf_hbm.at[layer, pg], lfbuf.at[slot, i], sem.at[slot]).start()

    step = b * steps + j
    slot = step % 2

    @pl.when(step == 0)
    def _():
        fetch(0, 0, 0)

    @pl.when(step + 1 < pl.num_programs(0) * steps)
    def _():
        wrap = j + 1 == steps
        fetch(jnp.where(wrap, b + 1, b), jnp.where(wrap, 0, j + 1), 1 - slot)

    q = q_ref[...]
    lane = lax.broadcasted_iota(I32, (rows, LANES), 1)
    rowi = lax.broadcasted_iota(I32, (rows, LANES), 0)
    same_head = (lane % n_heads) == (rowi % n_heads)
    l_i = lax.broadcasted_iota(I32, (LANES, LANES), 0)
    l_j = lax.broadcasted_iota(I32, (LANES, LANES), 1)
    head_eq = (l_i % n_heads) == (l_j % n_heads)

    @pl.when(j == 0)
    def _():
        fn = _dot_exact_rhs01(lfn_ref[...], (head_eq & (l_i <= l_j)).astype(BF16))[0:1, :]
        s = _dot_nt(q, kn_ref[...]) - fn
        keep = same_head & (lane // n_heads <= rowi // n_heads) & (lane < rows)
        s = jnp.where(keep, s, NEG_INF)
        m = jnp.max(s, axis=-1, keepdims=True)
        p = jnp.exp(s - m)
        m_scr[...] = m
        l_scr[...] = jnp.sum(p, axis=-1, keepdims=True)
        acc_scr[...] = _dot_nn(p.astype(BF16), vn_ref[...])
        carry_scr[...] = jnp.zeros(carry_scr.shape, F32)

    pltpu.make_async_copy(ck_hbm.at[layer, pl.ds(0, P)], kbuf.at[slot], sem.at[slot]).wait()
    pltpu.make_async_copy(cv_hbm.at[layer, pl.ds(0, P)], vbuf.at[slot], sem.at[slot]).wait()
    pltpu.make_async_copy(lf_hbm.at[layer, pl.ds(0, P)], lfbuf.at[slot], sem.at[slot]).wait()

    x = lfbuf[slot].reshape(P * tiles, LANES)
    within = _dot_exact_rhs01(x, (head_eq & (l_i > l_j)).astype(BF16))
    tot = _dot_exact_rhs01(x, head_eq.astype(BF16))
    nr = P * tiles
    r_i = lax.broadcasted_iota(I32, (nr, nr), 0)
    c_i = lax.broadcasted_iota(I32, (nr, nr), 1)
    later_rows = ((c_i // tiles < r_i // tiles) | ((c_i // tiles == r_i // tiles) & (c_i > r_i))).astype(BF16)
    carry = carry_scr[...]
    suf = within + _dot_exact_lhs01(later_rows, tot) + carry
    carry_scr[...] = carry + jnp.sum(tot, axis=0, keepdims=True)

    k2 = kbuf[slot].reshape(-1, LANES).astype(BF16)
    v2 = vbuf[slot].reshape(-1, LANES).astype(BF16)
    s = _dot_nt(q, k2)
    blocks = [jnp.where(same_head, s[:, c * LANES:(c + 1) * LANES] + suf[c:c + 1, :], NEG_INF) for c in range(nr)]
    blk_max = blocks[0]
    for blk in blocks[1:]:
        blk_max = jnp.maximum(blk_max, blk)
    m_old = m_scr[...]
    m_new = jnp.maximum(m_old, jnp.max(blk_max, axis=-1, keepdims=True))
    alpha = jnp.exp(m_old - m_new)
    probs = [jnp.exp(blk - m_new) for blk in blocks]
    psum = probs[0]
    for pb in probs[1:]:
        psum = psum + pb
    l_scr[...] = alpha * l_scr[...] + jnp.sum(psum, axis=-1, keepdims=True)
    p_all = jnp.concatenate([pb.astype(BF16) for pb in probs], axis=1)
    acc_scr[...] = alpha * acc_scr[...] + _dot_nn(p_all, v2)
    m_scr[...] = m_new

    @pl.when(j == pl.num_programs(1) - 1)
    def _():
        o_ref[...] = acc_scr[...] / l_scr[...]


def _fox_sample_call(page_table, qf, kf, vf, logf, cache_k, cache_v, cache_logf, layer, db, t_new, n_heads, dh,
                     pages_per_step):
    assert dh == LANES
    rows = t_new * n_heads
    n_pool, page = cache_k.shape[1:3]
    n_pages = page_table.shape[1]
    tiles = page * n_heads // LANES
    P = pages_per_step
    steps = n_pages // P
    pad_rows = lambda a: jnp.pad(a.reshape(db, rows, dh), ((0, 0), (0, LANES - rows), (0, 0))).astype(BF16)
    q2 = qf.reshape(db, rows, dh).astype(BF16)
    lfn = jnp.pad(logf.reshape(db, 1, rows), ((0, 0), (0, 7), (0, LANES - rows)))
    lf_pages = cache_logf.reshape(cache_logf.shape[0], n_pool, tiles, LANES)

    per_seq = lambda r: pl.BlockSpec((None, r, LANES), lambda b, j, pt: (b, 0, 0))
    hbm = pl.BlockSpec(memory_space=pl.ANY)
    out = pl.pallas_call(
        functools.partial(_fox_sample_body, n_heads=n_heads, pages_per_step=P, layer=layer, n_pages=n_pages),
        grid_spec=pltpu.PrefetchScalarGridSpec(
            num_scalar_prefetch=1,
            grid=(db, steps),
            in_specs=[per_seq(rows), per_seq(LANES), per_seq(LANES), per_seq(8), hbm, hbm, hbm],
            out_specs=per_seq(rows),
            scratch_shapes=[pltpu.VMEM((2, P, page, n_heads, dh), F32), pltpu.VMEM((2, P, page, n_heads, dh), F32),
                            pltpu.VMEM((2, P, tiles, LANES), F32), pltpu.SemaphoreType.DMA((2,)),
                            pltpu.VMEM((rows, 1), F32), pltpu.VMEM((rows, 1), F32),
                            pltpu.VMEM((rows, LANES), F32), pltpu.VMEM((1, LANES), F32)]),
        out_shape=jax.ShapeDtypeStruct((db, rows, LANES), F32),
        compiler_params=_cparams(("arbitrary", "arbitrary")),
        name="fox_sample",
    )(page_table, q2, pad_rows(kf), pad_rows(vf), lfn, cache_k, cache_v, lf_pages)
    return out.reshape(db * t_new, n_heads * dh)


def _rms_heads(x, n_heads, dh):
    outs = []
    for h in range(n_heads):
        xs = x[:, h * dh:(h + 1) * dh]
        outs.append(xs * lax.rsqrt(jnp.mean(xs * xs, axis=-1, keepdims=True) + EPS))
    return jnp.concatenate(outs, axis=1)


def _route(logits):
    lane = lax.broadcasted_iota(I32, logits.shape, 1)
    big = jnp.int32(LANES)
    gl = jnp.where(lane < N_GROUPS, logits, NEG_INF)
    gmax = jnp.max(gl, axis=-1, keepdims=True)
    g_sel = jnp.min(jnp.where(gl == gmax, lane, big), axis=-1, keepdims=True)
    p_g = 1.0 / jnp.sum(jnp.exp(gl - gmax), axis=-1, keepdims=True)
    lo = N_GROUPS + E_PER_GROUP * g_sel
    ev = jnp.where((lane >= lo) & (lane < lo + E_PER_GROUP), logits, NEG_INF)
    v1 = jnp.max(ev, axis=-1, keepdims=True)
    i1 = jnp.min(jnp.where(ev == v1, lane, big), axis=-1, keepdims=True)
    ev2 = jnp.where(lane == i1, NEG_INF, ev)
    v2 = jnp.max(ev2, axis=-1, keepdims=True)
    i2 = jnp.min(jnp.where(ev2 == v2, lane, big), axis=-1, keepdims=True)
    e21 = jnp.exp(v2 - v1)
    w1 = p_g / (1.0 + e21)
    w2 = p_g * e21 / (1.0 + e21)
    out = jnp.where(lane == 0, (i1 - N_GROUPS).astype(F32), 0.0)
    out = jnp.where(lane == 1, (i2 - N_GROUPS).astype(F32), out)
    out = jnp.where(lane == 2, w1, out)
    out = jnp.where(lane == 3, w2, out)
    return out


def _store_slabs(ref, x):
    tm, d = x.shape
    per = d // LANES
    for s in range(per):
        ref[pl.ds(s, tm, stride=per), :] = x[:, s * LANES:(s + 1) * LANES]


def _load_slabs(ref, tm, per):
    return jnp.concatenate([ref[pl.ds(s, tm, stride=per), :] for s in range(per)], axis=1)


def _merge_body(x_ref, og_ref, gg_ref, of_ref, gw_ref, fw_ref, wo_ref, g1_ref, sh2_ref, sc2_ref, n2_ref,
                wr_hi_ref, wr_lo_ref, br_ref, *rest, n_heads, dh, n_tiles):
    x1_ref, h2_ref, route_ref = rest[-3:]
    i = pl.program_id(0)

    @pl.when(i < n_tiles)
    def _():
        og = _rms_heads(og_ref[...], n_heads, dh) * gw_ref[...] * _silu(gg_ref[...])
        of = _rms_heads(of_ref[...], n_heads, dh) * fw_ref[...]
        merged = jnp.concatenate([og, of], axis=1).astype(BF16)
        x1 = x_ref[...] + g1_ref[...] * _dot_nn(merged, wo_ref[...])
        x1_ref[...] = x1
        y = x1 * lax.rsqrt(jnp.mean(x1 * x1, axis=-1, keepdims=True) + EPS) * n2_ref[...]
        h2 = y * (1.0 + sc2_ref[...]) + sh2_ref[...]
        _store_slabs(h2_ref, h2)
        hh, hl, _ = _split3(h2)
        logits = (_dot_nn(hh, wr_hi_ref[...]) + _dot_nn(hl, wr_hi_ref[...]) + _dot_nn(hh, wr_lo_ref[...])
                  + br_ref[...])
        route_ref[...] = _route(logits)

    @pl.when(i >= n_tiles)
    def _():
        h2_ref[...] = jnp.zeros(h2_ref.shape, F32)


def _merge(x, o_gla, gla_in, o_fox, gla_w, fox_w, w_out_bf, g1, sh2, sc2, norm2_w, wr_hi, wr_lo, b_r,
           tm, rows_per_mod, n_heads, dh, slab_rows, row0, into=None):
    n, d = x.shape
    half = n_heads * dh
    blk0 = row0 // tm
    per = d // LANES
    n_tiles = n // tm
    first = into is None
    extra = -(-(slab_rows - n) // tm) if first else 0
    assert extra <= 1
    aliased = [] if first else [into]
    n_in = 14
    last = n_tiles - 1
    per_token = rows_per_mod == 1
    if per_token:
        mod_spec = pl.BlockSpec((tm, d), lambda i: (jnp.minimum(i, last), 0))
    else:
        tiles_per_mod = rows_per_mod // tm
        g1, sh2, sc2 = (a.reshape(a.shape[0], 1, d) for a in (g1, sh2, sc2))
        mod_spec = pl.BlockSpec((None, 1, d), lambda i: (jnp.minimum(i, last) // tiles_per_mod, 0, 0))
    row = lambda i: (jnp.minimum(i, last), 0)
    fixed = lambda i: (0, 0)
    return pl.pallas_call(
        functools.partial(_merge_body, n_heads=n_heads, dh=dh, n_tiles=n_tiles),
        grid=(n_tiles + extra,),
        in_specs=[pl.BlockSpec((tm, d), row),
                  pl.BlockSpec((tm, half), row),
                  pl.BlockSpec((tm, half), lambda i: (jnp.minimum(i, last), 2)),
                  pl.BlockSpec((tm, half), row),
                  pl.BlockSpec((1, half), fixed), pl.BlockSpec((1, half), fixed),
                  pl.BlockSpec((d, d), fixed),
                  mod_spec, mod_spec, mod_spec,
                  pl.BlockSpec((1, d), fixed),
                  pl.BlockSpec((d, LANES), fixed), pl.BlockSpec((d, LANES), fixed),
                  pl.BlockSpec((1, LANES), fixed)] + [pl.BlockSpec(memory_space=pl.ANY)] * len(aliased),
        out_specs=(pl.BlockSpec((tm, d), row), pl.BlockSpec((tm * per, LANES), lambda i: (blk0 + i, 0)),
                   pl.BlockSpec((tm, LANES), row)),
        out_shape=(jax.ShapeDtypeStruct((n, d), F32), jax.ShapeDtypeStruct((slab_rows * per, LANES), F32),
                   jax.ShapeDtypeStruct((n, LANES), F32)),
        input_output_aliases={n_in: 1} if aliased else {},
        compiler_params=_cparams(("arbitrary",)),
        name="merge_outproj",
    )(x, o_gla, gla_in, o_fox, gla_w.reshape(1, half), fox_w.reshape(1, half), w_out_bf, g1, sh2, sc2,
      norm2_w.reshape(1, d), wr_hi, wr_lo, b_r, *aliased)


GATHER_UNROLL = 8


def _experts_body(te_ref, first_ref, tok_ref, nt_ref, grp_ref, nxt_ref, h_hbm, wg_hbm, wu_hbm, wd_hbm, y_ref,
                  xbuf, wg_f, wu_f, wd_f, wg_bf, wu_bf, wd_bf, sem, wsem, *, tm, per):
    i = pl.program_id(0)
    n_active = nt_ref[0]

    def weight_copies(expert, wslot):
        return (pltpu.make_async_copy(wg_hbm.at[expert], wg_f.at[wslot], wsem.at[wslot]),
                pltpu.make_async_copy(wu_hbm.at[expert], wu_f.at[wslot], wsem.at[wslot]),
                pltpu.make_async_copy(wd_hbm.at[expert], wd_f.at[wslot], wsem.at[wslot]))

    def gather(tile, slot):
        first = first_ref[tile]

        def issue(g, carry):
            for u in range(GATHER_UNROLL):
                r = g * GATHER_UNROLL + u
                src = pl.multiple_of(tok_ref[first + r] * per, per)
                pltpu.make_async_copy(h_hbm.at[pl.ds(src, per), :], xbuf.at[slot, pl.ds(r * per, per), :],
                                      sem.at[slot]).start()
            return carry
        lax.fori_loop(0, tm // GATHER_UNROLL, issue, 0)

    @pl.when((i == 0) & (n_active > 0))
    def _():
        gather(0, 0)
        for cp in weight_copies(te_ref[0], 0):
            cp.start()

    @pl.when(i + 1 < n_active)
    def _():
        gather(i + 1, (i + 1) % 2)

    @pl.when(i < n_active)
    def _():
        slot = i % 2
        prev = te_ref[jnp.maximum(i - 1, 0)]

        @pl.when((i == 0) | (te_ref[i] != prev))
        def _():
            wslot = grp_ref[i] % 2

            @pl.when(nxt_ref[i] >= 0)
            def _():
                for cp in weight_copies(nxt_ref[i], 1 - wslot):
                    cp.start()

            for cp in weight_copies(te_ref[i], wslot):
                cp.wait()
            wg_bf[...] = wg_f[wslot].astype(BF16)
            wu_bf[...] = wu_f[wslot].astype(BF16)
            wd_bf[...] = wd_f[wslot].astype(BF16)

        pltpu.make_async_copy(h_hbm.at[pl.ds(0, tm * per), :], xbuf.at[slot], sem.at[slot]).wait()
        x = _load_slabs(xbuf.at[slot], tm, per).astype(BF16)
        a = (_silu(_dot_nn(x, wg_bf[...])) * _dot_nn(x, wu_bf[...])).astype(BF16)
        _store_slabs(y_ref, _dot_nn(a, wd_bf[...]))

    @pl.when(i >= n_active)
    def _():
        y_ref[...] = jnp.zeros(y_ref.shape, F32)


def _experts(tile_expert, tile_first, sorted_tok, n_tiles, tile_group, next_expert, h2_slabs,
             w_gate, w_up, w_down, tm):
    max_tiles = tile_expert.shape[0]
    _, d, dff = w_gate.shape
    per = d // LANES
    hbm = pl.BlockSpec(memory_space=pl.ANY)
    return pl.pallas_call(
        functools.partial(_experts_body, tm=tm, per=per),
        grid_spec=pltpu.PrefetchScalarGridSpec(
            num_scalar_prefetch=6,
            grid=(max_tiles,),
            in_specs=[hbm, hbm, hbm, hbm],
            out_specs=pl.BlockSpec((tm * per, LANES), lambda i, *prefetch: (i, 0)),
            scratch_shapes=[pltpu.VMEM((2, tm * per, LANES), F32),
                            pltpu.VMEM((2, d, dff), F32), pltpu.VMEM((2, d, dff), F32), pltpu.VMEM((2, dff, d), F32),
                            pltpu.VMEM((d, dff), BF16), pltpu.VMEM((d, dff), BF16), pltpu.VMEM((dff, d), BF16),
                            pltpu.SemaphoreType.DMA((2,)), pltpu.SemaphoreType.DMA((2,))]),
        out_shape=jax.ShapeDtypeStruct((max_tiles * tm * per, LANES), F32),
        compiler_params=_cparams(("arbitrary",)),
        name="experts",
    )(tile_expert, tile_first, sorted_tok, n_tiles, tile_group, next_expert, h2_slabs, w_gate, w_up, w_down)


def _combine_body(pos_ref, y_hbm, x1_ref, route_ref, g2_ref, fw_ref, o_ref, buf, sem, *, tm, row0, per):
    i = pl.program_id(0)

    def gather(tile, slot):
        def issue(g, carry):
            for u in range(GATHER_UNROLL // 2):
                r = g * (GATHER_UNROLL // 2) + u
                tok = row0 + tile * tm + r
                for c in range(2):
                    src = pl.multiple_of(pos_ref[2 * tok + c] * per, per)
                    pltpu.make_async_copy(y_hbm.at[pl.ds(src, per), :],
                                          buf.at[slot, c, pl.ds(r * per, per), :], sem.at[slot]).start(priority=1)
            return carry
        lax.fori_loop(0, tm // (GATHER_UNROLL // 2), issue, 0)

    @pl.when(i == 0)
    def _():
        gather(0, 0)

    @pl.when(i + 1 < pl.num_programs(0))
    def _():
        gather(i + 1, (i + 1) % 2)

    slot = i % 2
    for c in range(2):
        pltpu.make_async_copy(y_hbm.at[pl.ds(0, tm * per), :], buf.at[slot, c], sem.at[slot]).wait()
    route = route_ref[...]
    moe = (route[:, 2:3] * _load_slabs(buf.at[slot, 0], tm, per)
           + route[:, 3:4] * _load_slabs(buf.at[slot, 1], tm, per))
    x2 = x1_ref[...] + g2_ref[...] * moe
    o_ref[...] = x2 * lax.rsqrt(jnp.mean(x2 * x2, axis=-1, keepdims=True) + EPS) * fw_ref[...]


def _combine(pos, y_slabs, x1, route, g2, final_w, row0, n_rows, tm, rows_per_mod):
    d = x1.shape[1]
    per = d // LANES
    per_token = rows_per_mod == 1
    if per_token:
        mod_spec = pl.BlockSpec((tm, d), lambda i, pos: (i, 0))
    else:
        tiles_per_mod = rows_per_mod // tm
        g2 = g2.reshape(g2.shape[0], 1, d)
        mod_spec = pl.BlockSpec((None, 1, d), lambda i, pos: (i // tiles_per_mod, 0, 0))
    return pl.pallas_call(
        functools.partial(_combine_body, tm=tm, row0=row0, per=per),
        grid_spec=pltpu.PrefetchScalarGridSpec(
            num_scalar_prefetch=1,
            grid=(n_rows // tm,),
            in_specs=[pl.BlockSpec(memory_space=pl.ANY),
                      pl.BlockSpec((tm, d), lambda i, pos: (i, 0)),
                      pl.BlockSpec((tm, LANES), lambda i, pos: (i, 0)),
                      mod_spec,
                      pl.BlockSpec((1, d), lambda i, pos: (0, 0))],
            out_specs=pl.BlockSpec((tm, d), lambda i, pos: (i, 0)),
            scratch_shapes=[pltpu.VMEM((2, 2, tm * per, LANES), F32), pltpu.SemaphoreType.DMA((2,))]),
        out_shape=jax.ShapeDtypeStruct((n_rows, d), F32),
        compiler_params=_cparams(("arbitrary",)),
        name="combine_norm",
    )(pos, y_slabs, x1, route, g2, final_w.reshape(1, d))


def _invert_body(where_ref, out_ref, *, n_pairs, n_out):
    def fill(g, carry):
        for u in range(GATHER_UNROLL):
            a = g * GATHER_UNROLL + u
            out_ref[where_ref[a]] = lax.shift_right_logical(a, 1)
        return carry
    lax.fori_loop(0, n_pairs // GATHER_UNROLL, fill, 0)

    def tail(t, carry):
        out_ref[n_pairs + t] = 0
        return carry
    lax.fori_loop(0, n_out - n_pairs, tail, 0)


def _sorted_tokens(where, n_out):
    n_pairs = where.shape[0]
    assert n_pairs % GATHER_UNROLL == 0
    smem = pl.BlockSpec(memory_space=pltpu.SMEM)
    return pl.pallas_call(
        functools.partial(_invert_body, n_pairs=n_pairs, n_out=n_out),
        in_specs=[smem], out_specs=smem,
        out_shape=jax.ShapeDtypeStruct((n_out,), I32),
        name="sorted_tokens",
    )(where)


def _plan(route, n_experts, tm):
    n = route.shape[0]
    e_flat = route[:, 0:2].astype(I32).reshape(-1)
    onehot = (e_flat[:, None] == jnp.arange(n_experts, dtype=I32)[None, :]).astype(I32)
    csum = jnp.cumsum(onehot, axis=0)
    counts = csum[-1]
    rank = jnp.sum(onehot * csum, axis=1) - 1
    tiles_e = (counts + tm - 1) // tm
    tile_end = jnp.cumsum(tiles_e)
    tile_off = tile_end - tiles_e
    cnt_off = jnp.cumsum(counts) - counts
    max_tiles = (2 * n) // tm + n_experts
    pos = jnp.sum(onehot * (tile_off * tm)[None, :], axis=1) + rank
    where = jnp.sum(onehot * cnt_off[None, :], axis=1) + rank
    tile_ids = jnp.arange(max_tiles, dtype=I32)
    live_ids = jnp.minimum(tile_ids, tile_end[-1] - 1)
    tile_expert = jnp.minimum(jnp.sum((tile_end[None, :] <= live_ids[:, None]).astype(I32), axis=1), n_experts - 1)
    is_e = (tile_expert[:, None] == jnp.arange(n_experts, dtype=I32)[None, :]).astype(I32)
    tile_first = jnp.sum(is_e * (cnt_off - tile_off * tm)[None, :], axis=1) + tile_ids * tm
    tile_first = jnp.clip(tile_first, 0, 2 * n)
    e_ids = jnp.arange(n_experts, dtype=I32)
    used = tiles_e > 0
    run_of_e = jnp.cumsum(used.astype(I32)) - 1
    later_used = used[None, :] & (e_ids[None, :] > e_ids[:, None])
    next_of_e = jnp.min(jnp.where(later_used, e_ids[None, :], n_experts), axis=1)
    next_of_e = jnp.where(next_of_e < n_experts, next_of_e, -1)
    tile_group = jnp.sum(is_e * run_of_e[None, :], axis=1)
    next_expert = jnp.sum(is_e * next_of_e[None, :], axis=1)
    return (tile_expert, tile_first.astype(I32), tile_end[-1].reshape(1).astype(I32), tile_group.astype(I32),
            next_expert.astype(I32), pos.astype(I32), where.astype(I32))


def kernel(x_prompt, x_sample, cache_k, cache_v, cache_logf, state_gla, page_table, c_prompt, c_sample,
           norm1_w, norm2_w, w_ada, b_ada, w_in, w_gk2, b_gk, b_fgate, gla_onorm_w, fox_onorm_w, w_out,
           w_rg, b_rg, w_re, b_re, w_gate_e, w_up_e, w_down_e, final_norm_w):
    depth = w_in.shape[0]
    assert depth == 1, "single-layer trunk"
    batch, seq, d = x_prompt.shape
    db, t_new, _ = x_sample.shape
    n_hg, dk, dv = state_gla.shape[2:]
    n_hf, dh = cache_k.shape[3:]
    rank = w_gk2.shape[1]
    n_experts = w_gate_e.shape[1]
    wq = n_hg * dk
    wv = n_hg * dv
    wf = n_hf * dh
    assert wv == 1024 and wf == 1024 and 2 * wq == 1024 and rank == 16 and n_hf == 8
    n_p = batch * seq
    n_s = db * t_new
    layer = 0

    wi = w_in[layer]
    o_alr = 2 * wq + 2 * wv
    o_fox = o_alr + rank
    o_fl = o_fox + 3 * wf
    w_gla = wi[:, :o_alr].astype(BF16)
    w_fox = wi[:, o_fox:o_fl].astype(BF16)
    w_small = jnp.concatenate([wi[:, o_alr:o_fox], wi[:, o_fl:], jnp.zeros((d, LANES - rank - n_hf), F32)],
                              axis=1).astype(BF16)
    b_small = jnp.zeros((1, LANES), F32).at[0, rank:rank + n_hf].set(b_fgate[layer])
    w_gk2_pad = jnp.zeros((LANES, wq), F32).at[:rank].set(w_gk2[layer])
    b_gk2 = b_gk[layer].reshape(1, wq)
    w_out_bf = w_out[layer].astype(BF16)
    w_r = jnp.concatenate([w_rg[layer], w_re[layer],
                           jnp.zeros((d, LANES - N_GROUPS - n_experts), F32)], axis=1)
    wr_hi = w_r.astype(BF16)
    wr_lo = (w_r - wr_hi.astype(F32)).astype(BF16)
    b_r = jnp.concatenate([b_rg[layer], b_re[layer], jnp.zeros((LANES - N_GROUPS - n_experts,), F32)]).reshape(1, LANES)

    n_c = batch + db
    c_all = jnp.concatenate([c_prompt, c_sample, jnp.zeros((-n_c % 8, d), F32)], axis=0)
    mod = _adaln(c_all, w_ada[layer], b_ada[layer])
    sh1, sc1, g1, sh2, sc2, g2 = (mod[:, i * d:(i + 1) * d] for i in range(6))
    p_rows = slice(0, batch)
    rep = lambda a: jnp.repeat(a[batch:n_c], t_new, axis=0)

    q_scale = dh ** -0.5
    xp = x_prompt.reshape(n_p, d)
    xs = x_sample.reshape(n_s, d)

    tm_p = 512
    gla_p, qf_p, kf_p, vf_p, kfb_p, vfb_p, small_p = _inproj(
        xp, sh1[p_rows], sc1[p_rows], norm1_w[layer], w_gla, w_fox, w_small, b_small, tm_p, seq, q_scale * LOG2E)
    s0_p = jnp.zeros((batch, n_hg, dk, dv), F32)
    o_gla_p, gla_state_p = _gla(gla_p, small_p, w_gk2_pad, b_gk2, s0_p, batch, seq, 128, n_hg, dk, dv)
    logf_p = small_p[:, rank:rank + n_hf]
    key_bias = _key_bias(small_p, batch, seq, rank, n_hf)
    o_fox_p = _fox_prompt(qf_p, kfb_p, vfb_p, key_bias, batch, seq, n_hf, dh, 512)
    n_all = n_p + n_s
    x1_p, h2_p, route_p = _merge(xp, o_gla_p, gla_p, o_fox_p, gla_onorm_w[layer], fox_onorm_w[layer], w_out_bf,
                                 g1[p_rows], sh2[p_rows], sc2[p_rows], norm2_w[layer], wr_hi, wr_lo, b_r,
                                 256, seq, n_hg, dv, n_all, 0)

    gla_s, qf_s, kf_s, vf_s, _, _, small_s = _inproj(
        xs, rep(sh1), rep(sc1), norm1_w[layer], w_gla, w_fox, w_small, b_small, n_s, 1, q_scale)
    t_pad = GLA_CHUNK
    pad_t = lambda a: jnp.pad(a.reshape(db, t_new, -1), ((0, 0), (0, t_pad - t_new), (0, 0))).reshape(db * t_pad, -1)
    o_gla_s_pad, gla_state_s = _gla(pad_t(gla_s), pad_t(small_s), w_gk2_pad, b_gk2, state_gla[layer],
                                    db, t_pad, t_pad, n_hg, dk, dv, t_valid=t_new)
    o_gla_s = o_gla_s_pad.reshape(db, t_pad, wv)[:, :t_new].reshape(n_s, wv)

    logf_s = small_s[:, rank:rank + n_hf]
    o_fox_s = _fox_sample_call(page_table, qf_s, kf_s, vf_s, logf_s, cache_k, cache_v, cache_logf,
                               layer, db, t_new, n_hf, dh, 8)
    x1_s, h2, route_s = _merge(xs, o_gla_s, gla_s, o_fox_s, gla_onorm_w[layer], fox_onorm_w[layer], w_out_bf,
                               rep(g1), rep(sh2), rep(sc2), norm2_w[layer], wr_hi, wr_lo, b_r,
                               n_s, 1, n_hg, dv, n_all, n_p, into=h2_p)

    tm_e = 256
    route = jnp.concatenate([route_p, route_s], axis=0)
    tile_expert, tile_first, n_tiles, tile_group, next_expert, pos, where = _plan(route, n_experts, tm_e)
    sorted_tok = _sorted_tokens(where, where.shape[0] + tm_e)
    y_slabs = _experts(tile_expert, tile_first, sorted_tok, n_tiles, tile_group, next_expert, h2,
                       w_gate_e[layer], w_up_e[layer], w_down_e[layer], tm_e)
    y_p = _combine(pos, y_slabs, x1_p, route_p, g2[p_rows], final_norm_w, 0, n_p, 256, seq)
    y_s = _combine(pos, y_slabs, x1_s, route_s, rep(g2), final_norm_w, n_p, n_s, n_s, 1)

    y_prompt = y_p.reshape(batch, seq, d)
    y_sample = y_s.reshape(db, t_new, d)
    k_prompt = kf_p.reshape(1, batch, seq, n_hf, dh)
    v_prompt = vf_p.reshape(1, batch, seq, n_hf, dh)
    logf_prompt = logf_p.reshape(1, batch, seq, n_hf)
    k_sample = kf_s.reshape(1, db, t_new, n_hf, dh)
    v_sample = vf_s.reshape(1, db, t_new, n_hf, dh)
    logf_sample = logf_s.reshape(1, db, t_new, n_hf)
    return (y_prompt, y_sample, k_prompt, v_prompt, logf_prompt, gla_state_p[None],
            k_sample, v_sample, logf_sample, gla_state_s[None])
```

```python
import functools

import jax
import jax.numpy as jnp
from jax import lax
from jax.experimental import pallas as pl
from jax.experimental.pallas import tpu as pltpu

F32 = jnp.float32
BF16 = jnp.bfloat16
I32 = jnp.int32
EPS = 1e-6
NEG_INF = float("-inf")

LANES = 128
GLA_CHUNK = 16
GLA_GATE_NORM = 16.0
N_GROUPS = 4
E_PER_GROUP = 8
VMEM_LIMIT = 56 * 1024 * 1024


def _cparams(sem):
    return pltpu.CompilerParams(dimension_semantics=sem, vmem_limit_bytes=VMEM_LIMIT)


def _log_sigmoid(z):
    return jnp.minimum(z, 0.0) - jnp.log1p(jnp.exp(-jnp.abs(z)))


def _silu(z):
    return z * jax.nn.sigmoid(z)


def _split3(a):
    hi = a.astype(BF16)
    r = a - hi.astype(F32)
    mid = r.astype(BF16)
    lo = (r - mid.astype(F32)).astype(BF16)
    return hi, mid, lo


def _dot_nn(a, b):
    return jnp.dot(a, b, preferred_element_type=F32)


def _dot_nt(a, b):
    return lax.dot_general(a, b, (((1,), (1,)), ((), ())), preferred_element_type=F32)


def _dot_f32(a, b):
    ah, al, _ = _split3(a)
    bh, bl, _ = _split3(b)
    return _dot_nn(ah, bh) + _dot_nn(al, bh) + _dot_nn(ah, bl)


def _dot_exact_rhs01(a, ones_bf16):
    hi, mid, lo = _split3(a)
    return _dot_nn(hi, ones_bf16) + _dot_nn(mid, ones_bf16) + _dot_nn(lo, ones_bf16)


def _dot_exact_lhs01(ones_bf16, b):
    hi, mid, lo = _split3(b)
    return _dot_nn(ones_bf16, hi) + _dot_nn(ones_bf16, mid) + _dot_nn(ones_bf16, lo)


def _adaln_body(c_ref, w_ref, b_ref, o_ref):
    s = _silu(c_ref[...]).astype(BF16)
    o_ref[...] = _dot_nn(s, w_ref[...].astype(BF16)) + b_ref[...]


def _adaln(c_all, w_ada, b_ada, tn=1024):
    rows, d = c_all.shape
    n6 = w_ada.shape[1]
    return pl.pallas_call(
        _adaln_body,
        grid=(n6 // tn,),
        in_specs=[pl.BlockSpec((rows, d), lambda j: (0, 0)),
                  pl.BlockSpec((d, tn), lambda j: (0, j)),
                  pl.BlockSpec((1, tn), lambda j: (0, j))],
        out_specs=pl.BlockSpec((rows, tn), lambda j: (0, j)),
        out_shape=jax.ShapeDtypeStruct((rows, n6), F32),
        compiler_params=_cparams(("arbitrary",)),
        name="adaln",
    )(c_all, w_ada, b_ada.reshape(1, n6))


def _pack_w_in_body(a_ref, b_ref, c_ref, o_ref, *, shift, n_first):
    j = pl.program_id(0)

    @pl.when(j < n_first)
    def _():
        o_ref[...] = a_ref[...].astype(BF16)

    @pl.when(j >= n_first)
    def _():
        both = jnp.concatenate([b_ref[...], c_ref[...]], axis=1)
        o_ref[...] = both[:, shift:shift + o_ref.shape[1]].astype(BF16)


def _pack_w_in(wi, cols, second_start):
    d = wi.shape[0]
    tn = 512
    shift = second_start % LANES
    base = second_start - shift
    assert cols % tn == 0 and base % tn == 0 and shift > 0
    n_first = cols // tn
    second = lambda j: jnp.maximum(j - n_first, 0)
    return pl.pallas_call(
        functools.partial(_pack_w_in_body, shift=shift, n_first=n_first),
        grid=(2 * n_first,),
        in_specs=[pl.BlockSpec((d, tn), lambda j: (0, jnp.minimum(j, n_first - 1))),
                  pl.BlockSpec((d, tn), lambda j: (0, base // tn + second(j))),
                  pl.BlockSpec((d, LANES), lambda j: (0, (base + tn) // LANES + second(j) * (tn // LANES)))],
        out_specs=pl.BlockSpec((d, tn), lambda j: (0, j)),
        out_shape=jax.ShapeDtypeStruct((d, 2 * cols), BF16),
        compiler_params=_cparams(("arbitrary",)),
        name="pack_w_in",
    )(wi, wi, wi)


def _inproj_body(x_ref, sh_ref, sc_ref, nw_ref, w_ref, ws_ref, bs_ref,
                 gla_ref, qf_ref, kf_ref, vf_ref, kfb_ref, vfb_ref, small_ref, h_scr, *, q_scale):
    j = pl.program_id(1)

    @pl.when(j == 0)
    def _():
        x = x_ref[...]
        y = x * lax.rsqrt(jnp.mean(x * x, axis=-1, keepdims=True) + EPS) * nw_ref[...]
        hb = (y * (1.0 + sc_ref[...]) + sh_ref[...]).astype(BF16)
        h_scr[...] = hb
        sm = _dot_nn(hb, ws_ref[...])
        lane = lax.broadcasted_iota(I32, sm.shape, 1)
        small_ref[...] = jnp.where((lane >= 16) & (lane < 24), _log_sigmoid(sm + bs_ref[...]), sm)

    acc = _dot_nn(h_scr[...], w_ref[...])

    @pl.when(j < 3)
    def _():
        gla_ref[...] = acc

    @pl.when(j == 3)
    def _():
        qf_ref[...] = (acc * q_scale).astype(BF16)

    @pl.when(j == 4)
    def _():
        kf_ref[...] = acc
        kfb_ref[...] = acc.astype(BF16)

    @pl.when(j == 5)
    def _():
        vf_ref[...] = acc
        vfb_ref[...] = acc.astype(BF16)


def _inproj(x, sh, sc, norm_w, w_main, w_small, b_small, tm, rows_per_mod, q_scale):
    n, d = x.shape
    tn = w_main.shape[1] // 6
    per_token = rows_per_mod == 1
    if per_token:
        mod_spec = pl.BlockSpec((tm, d), lambda i, j: (i, 0))
    else:
        tiles_per_mod = rows_per_mod // tm
        sh = sh.reshape(sh.shape[0], 1, d)
        sc = sc.reshape(sc.shape[0], 1, d)
        mod_spec = pl.BlockSpec((None, 1, d), lambda i, j: (i // tiles_per_mod, 0, 0))
    row = lambda i, j: (i, 0)
    out_shapes = (jax.ShapeDtypeStruct((n, 3 * tn), F32),
                  jax.ShapeDtypeStruct((n, tn), BF16),
                  jax.ShapeDtypeStruct((n, tn), F32),
                  jax.ShapeDtypeStruct((n, tn), F32),
                  jax.ShapeDtypeStruct((n, tn), BF16),
                  jax.ShapeDtypeStruct((n, tn), BF16),
                  jax.ShapeDtypeStruct((n, LANES), F32))
    return pl.pallas_call(
        functools.partial(_inproj_body, q_scale=q_scale),
        grid=(n // tm, 6),
        in_specs=[pl.BlockSpec((tm, d), row), mod_spec, mod_spec,
                  pl.BlockSpec((1, d), lambda i, j: (0, 0)),
                  pl.BlockSpec((d, tn), lambda i, j: (0, j)),
                  pl.BlockSpec((d, LANES), lambda i, j: (0, 0)),
                  pl.BlockSpec((1, LANES), lambda i, j: (0, 0))],
        out_specs=(pl.BlockSpec((tm, tn), lambda i, j: (i, jnp.minimum(j, 2))),
                   pl.BlockSpec((tm, tn), row), pl.BlockSpec((tm, tn), row), pl.BlockSpec((tm, tn), row),
                   pl.BlockSpec((tm, tn), row), pl.BlockSpec((tm, tn), row),
                   pl.BlockSpec((tm, LANES), row)),
        out_shape=out_shapes,
        scratch_shapes=[pltpu.VMEM((tm, d), BF16)],
        compiler_params=_cparams(("arbitrary", "arbitrary")),
        name="inproj",
    )(x, sh, sc, norm_w.reshape(1, d), w_main, w_small, b_small)


def _level_ref(b, half, rows_i):
    tt, n = b.shape
    if half >= 4:
        size = 2 * half
        blocks = [jnp.broadcast_to(b[g * size + half - 1:g * size + half, :], (size, n)) for g in range(tt // size)]
        return blocks[0] if len(blocks) == 1 else jnp.concatenate(blocks, axis=0)
    if half == 2:
        lo = jnp.concatenate([jnp.broadcast_to(b[8 * g + 1:8 * g + 2, :], (8, n)) for g in range(tt // 8)], axis=0)
        hi = jnp.concatenate([jnp.broadcast_to(b[8 * g + 5:8 * g + 6, :], (8, n)) for g in range(tt // 8)], axis=0)
        return jnp.where(rows_i % 8 < 4, lo, hi)
    return jnp.where(rows_i % 2 == 1, pltpu.roll(b, 1, 0), b)


def _gla_body(q_ref, k_ref, v_ref, sm_ref, w2_ref, b2_ref, s0_ref, o_ref, sout_ref, st_scr,
              *, tt, dk, dv, n_pairs, t_valid):
    t = pl.program_id(1)
    zero_blk = jnp.zeros((dv, dk), F32)

    @pl.when(t == 0)
    def _():
        for p in range(n_pairs):
            top = jnp.concatenate([s0_ref[2 * p].T, zero_blk], axis=1)
            bot = jnp.concatenate([zero_blk, s0_ref[2 * p + 1].T], axis=1)
            st_scr[p] = jnp.concatenate([top, bot], axis=0)

    gk = _log_sigmoid(_dot_f32(sm_ref[...], w2_ref[...]) + b2_ref[...]) / GLA_GATE_NORM
    rows_i = lax.broadcasted_iota(I32, (tt, 1), 0)
    if t_valid < tt:
        gk = jnp.where(rows_i < t_valid, gk, 0.0)
    r_i = lax.broadcasted_iota(I32, (tt, tt), 0)
    c_i = lax.broadcasted_iota(I32, (tt, tt), 1)
    b_all = _dot_exact_lhs01((c_i <= r_i).astype(BF16), gk)

    halves = []
    half = tt // 2
    while half >= 1:
        halves.append(half)
        half //= 2
    refs = [_level_ref(b_all, hf, rows_i) for hf in halves]
    valid = [(r_i // (2 * hf) == c_i // (2 * hf)) & (r_i % (2 * hf) >= hf) & (c_i % (2 * hf) < hf) for hf in halves]
    on_diag = r_i == c_i
    head0 = lax.broadcasted_iota(I32, (tt, 2 * dk), 1) < dk
    rr = lax.broadcasted_iota(I32, (2 * dv, 2 * dk), 0) // dv
    cc = lax.broadcasted_iota(I32, (2 * dv, 2 * dk), 1) // dk
    diag = (rr == cc).astype(F32)

    def head_scores(qx, kx):
        stacked = jnp.concatenate([jnp.where(head0, qx, 0.0), jnp.where(head0, 0.0, qx)], axis=0).astype(BF16)
        return _dot_nt(stacked, kx.astype(BF16))

    for p in range(n_pairs):
        cs = slice(p * 2 * dk, (p + 1) * 2 * dk)
        vs = slice(p * 2 * dv, (p + 1) * 2 * dv)
        q = q_ref[:, cs] * (dk ** -0.5)
        k = k_ref[:, cs]
        v = v_ref[:, vs]
        b = b_all[:, cs]
        sc = head_scores(q, k)
        a0 = jnp.where(on_diag, sc[:tt], 0.0)
        a1 = jnp.where(on_diag, sc[tt:], 0.0)
        for ref_all, ok in zip(refs, valid):
            ref = ref_all[:, cs]
            sc = head_scores(q * jnp.exp(jnp.minimum(b - ref, 0.0)), k * jnp.exp(jnp.minimum(ref - b, 0.0)))
            a0 = jnp.where(ok, sc[:tt], a0)
            a1 = jnp.where(ok, sc[tt:], a1)
        vb = v.astype(BF16)
        o_intra = jnp.concatenate([_dot_nn(a0.astype(BF16), vb[:, :dv]), _dot_nn(a1.astype(BF16), vb[:, dv:])], axis=1)

        st = st_scr[p]
        bend = b[tt - 1:tt]
        o_inter = _dot_nt((q * jnp.exp(b)).astype(BF16), st.astype(BF16))
        kdec = (k * jnp.exp(bend - b)).astype(BF16)
        st = st * jnp.exp(bend) + _dot_nn(v.T.astype(BF16), kdec) * diag
        st_scr[p] = st
        o_ref[:, vs] = o_intra + o_inter

    @pl.when(t == pl.num_programs(1) - 1)
    def _():
        for p in range(n_pairs):
            st = st_scr[p]
            sout_ref[2 * p] = st[:dv, :dk].T
            sout_ref[2 * p + 1] = st[dv:, dk:].T


def _gla(gla_in, small, w_gk2_pad, b_gk, s0, batch, seq, tt, n_heads, dk, dv, t_valid=None):
    n = batch * seq
    nt = seq // tt
    wq = n_heads * dk
    wv = n_heads * dv
    assert wv == 2 * wq
    row = lambda b, t: b * nt + t
    return pl.pallas_call(
        functools.partial(_gla_body, tt=tt, dk=dk, dv=dv, n_pairs=n_heads // 2,
                          t_valid=tt if t_valid is None else t_valid),
        grid=(batch, nt),
        in_specs=[pl.BlockSpec((tt, wq), lambda b, t: (row(b, t), 0)),
                  pl.BlockSpec((tt, wq), lambda b, t: (row(b, t), 1)),
                  pl.BlockSpec((tt, wv), lambda b, t: (row(b, t), 1)),
                  pl.BlockSpec((tt, LANES), lambda b, t: (row(b, t), 0)),
                  pl.BlockSpec((LANES, wq), lambda b, t: (0, 0)),
                  pl.BlockSpec((1, wq), lambda b, t: (0, 0)),
                  pl.BlockSpec((None, n_heads, dk, dv), lambda b, t: (b, 0, 0, 0))],
        out_specs=(pl.BlockSpec((tt, wv), lambda b, t: (row(b, t), 0)),
                   pl.BlockSpec((None, n_heads, dk, dv), lambda b, t: (b, 0, 0, 0))),
        out_shape=(jax.ShapeDtypeStruct((n, wv), F32),
                   jax.ShapeDtypeStruct((batch, n_heads, dk, dv), F32)),
        scratch_shapes=[pltpu.VMEM((n_heads // 2, 2 * dv, 2 * dk), F32)],
        compiler_params=_cparams(("arbitrary", "arbitrary")),
        name="gla",
    )(gla_in, gla_in, gla_in, small, w_gk2_pad, b_gk, s0)


LOG2E = 1.4426950408889634
BIAS_PARTS = 3


def _key_bias_body(sm_ref, o_ref, *, lane0, n_heads, blk):
    s = sm_ref.shape[0]
    r_i = lax.broadcasted_iota(I32, (blk, blk), 0)
    c_i = lax.broadcasted_iota(I32, (blk, blk), 1)
    lower = (c_i <= r_i).astype(BF16)
    head = r_i - lane0
    sel = [((head >= 0) & (head < n_heads) & (c_i == BIAS_PARTS * head + part)).astype(BF16)
           for part in range(BIAS_PARTS)]
    carry = jnp.zeros((1, sm_ref.shape[1]), F32)
    for i in range(s // blk):
        f = _dot_exact_lhs01(lower, sm_ref[i * blk:(i + 1) * blk, :]) + carry
        carry = f[blk - 1:blk, :]
        pieces = _split3(f * (-LOG2E))
        placed = sum(_dot_nn(piece, sel_p) for piece, sel_p in zip(pieces, sel))
        o_ref[i * blk:(i + 1) * blk, :] = placed.astype(BF16)


def _key_bias(small, batch, seq, lane0, n_heads):
    return pl.pallas_call(
        functools.partial(_key_bias_body, lane0=lane0, n_heads=n_heads, blk=LANES),
        grid=(batch,),
        in_specs=[pl.BlockSpec((seq, LANES), lambda i: (i, 0))],
        out_specs=pl.BlockSpec((seq, LANES), lambda i: (i, 0)),
        out_shape=jax.ShapeDtypeStruct((batch * seq, LANES), BF16),
        compiler_params=_cparams(("arbitrary",)),
        name="fox_key_bias",
    )(small)


def _fox_prompt_body(q_ref, k_ref, v_ref, kb_ref, o_ref, m_scr, l_scr, acc_scr, *, n_heads, dh):
    qi = pl.program_id(1)
    ki = pl.program_id(2)
    tq = q_ref.shape[0]
    tk = k_ref.shape[0]

    @pl.when(ki == 0)
    def _():
        m_scr[...] = jnp.full(m_scr.shape, NEG_INF, F32)
        l_scr[...] = jnp.zeros(l_scr.shape, F32)
        acc_scr[...] = jnp.zeros(acc_scr.shape, F32)

    def step(masked):
        lane = lax.broadcasted_iota(I32, (tq, LANES), 1)
        kb = kb_ref[...]
        if masked:
            keep = lax.broadcasted_iota(I32, (tk, tq), 0) <= lax.broadcasted_iota(I32, (tk, tq), 1)
        for h in range(n_heads):
            hs = slice(h * dh, (h + 1) * dh)
            ones = ((lane >= BIAS_PARTS * h) & (lane < BIAS_PARTS * (h + 1))).astype(BF16)
            q_aug = jnp.concatenate([q_ref[:, hs], ones], axis=1)
            k_aug = jnp.concatenate([k_ref[:, hs], kb], axis=1)
            s_t = _dot_nt(k_aug, q_aug)
            if masked:
                s_t = jnp.where(keep, s_t, NEG_INF)
            m_old = m_scr[h]
            m_new = jnp.maximum(m_old, jnp.max(s_t, axis=0, keepdims=True))
            alpha = jnp.exp2(m_old - m_new)
            p_t = jnp.exp2(s_t - m_new)
            l_scr[h] = alpha * l_scr[h] + jnp.sum(p_t, axis=0, keepdims=True)
            acc_scr[h] = alpha * acc_scr[h] + _dot_nn(v_ref[:, hs].T, p_t.astype(BF16))
            m_scr[h] = m_new

    @pl.when(ki < qi)
    def _():
        step(False)

    @pl.when(ki == qi)
    def _():
        step(True)

    @pl.when(ki == pl.num_programs(2) - 1)
    def _():
        for h in range(n_heads):
            o_ref[:, h * dh:(h + 1) * dh] = (acc_scr[h] / l_scr[h]).T


def _fox_prompt(qf, kfb, vfb, key_bias, batch, seq, n_heads, dh, tq):
    nq = seq // tq
    width = n_heads * dh
    kv_map = lambda b, qi, ki: (b * nq + jnp.minimum(ki, qi), 0)
    return pl.pallas_call(
        functools.partial(_fox_prompt_body, n_heads=n_heads, dh=dh),
        grid=(batch, nq, nq),
        in_specs=[pl.BlockSpec((tq, width), lambda b, qi, ki: (b * nq + qi, 0)),
                  pl.BlockSpec((tq, width), kv_map),
                  pl.BlockSpec((tq, width), kv_map),
                  pl.BlockSpec((tq, LANES), kv_map)],
        out_specs=pl.BlockSpec((tq, width), lambda b, qi, ki: (b * nq + qi, 0)),
        out_shape=jax.ShapeDtypeStruct((batch * seq, width), F32),
        scratch_shapes=[pltpu.VMEM((n_heads, 1, tq), F32), pltpu.VMEM((n_heads, 1, tq), F32),
                        pltpu.VMEM((n_heads, dh, tq), F32)],
        compiler_params=_cparams(("arbitrary", "arbitrary", "arbitrary")),
        name="fox_prompt",
    )(qf, kfb, vfb, key_bias)


def _fox_sample_body(pt_ref, q_ref, kn_ref, vn_ref, lfn_ref, ck_hbm, cv_hbm, lf_hbm, o_ref,
                     kbuf, vbuf, lfbuf, sem, m_scr, l_scr, acc_scr, carry_scr,
                     *, n_heads, pages_per_step, layer, n_pages):
    P = pages_per_step
    b = pl.program_id(0)
    j = pl.program_id(1)
    steps = pl.num_programs(1)
    rows = q_ref.shape[0]
    tiles = lfbuf.shape[2]

    def fetch(seq, stp, slot):
        for i in range(P):
            pg = pt_ref[seq, n_pages - 1 - (stp * P + i)]
            pltpu.make_async_copy(ck_hbm.at[layer, pg], kbuf.at[slot, i], sem.at[slot]).start()
            pltpu.make_async_copy(cv_hbm.at[layer, pg], vbuf.at[slot, i], sem.at[slot]).start()
            pltpu.make_async_copy(lf_hbm.at[layer, pg], lfbuf.at[slot, i], sem.at[slot]).start()

    step = b * steps + j
    slot = step % 2

    @pl.when(step == 0)
    def _():
        fetch(0, 0, 0)

    @pl.when(step + 1 < pl.num_programs(0) * steps)
    def _():
        wrap = j + 1 == steps
        fetch(jnp.where(wrap, b + 1, b), jnp.where(wrap, 0, j + 1), 1 - slot)

    q = q_ref[...]
    lane = lax.broadcasted_iota(I32, (rows, LANES), 1)
    rowi = lax.broadcasted_iota(I32, (rows, LANES), 0)
    same_head = (lane % n_heads) == (rowi % n_heads)
    l_i = lax.broadcasted_iota(I32, (LANES, LANES), 0)
    l_j = lax.broadcasted_iota(I32, (LANES, LANES), 1)
    head_eq = (l_i % n_heads) == (l_j % n_heads)

    @pl.when(j == 0)
    def _():
        fn = _dot_exact_rhs01(lfn_ref[...], (head_eq & (l_i <= l_j)).astype(BF16))[0:1, :]
        s = _dot_nt(q, kn_ref[...]) - fn
        keep = same_head & (lane // n_heads <= rowi // n_heads) & (lane < rows)
        s = jnp.where(keep, s, NEG_INF)
        m = jnp.max(s, axis=-1, keepdims=True)
        p = jnp.exp(s - m)
        m_scr[...] = m
        l_scr[...] = jnp.sum(p, axis=-1, keepdims=True)
        acc_scr[...] = _dot_nn(p.astype(BF16), vn_ref[...])
        carry_scr[...] = jnp.zeros(carry_scr.shape, F32)

    pltpu.make_async_copy(ck_hbm.at[layer, pl.ds(0, P)], kbuf.at[slot], sem.at[slot]).wait()
    pltpu.make_async_copy(cv_hbm.at[layer, pl.ds(0, P)], vbuf.at[slot], sem.at[slot]).wait()
    pltpu.make_async_copy(lf_hbm.at[layer, pl.ds(0, P)], lfbuf.at[slot], sem.at[slot]).wait()

    x = lfbuf[slot].reshape(P * tiles, LANES)
    within = _dot_exact_rhs01(x, (head_eq & (l_i > l_j)).astype(BF16))
    tot = _dot_exact_rhs01(x, head_eq.astype(BF16))
    nr = P * tiles
    r_i = lax.broadcasted_iota(I32, (nr, nr), 0)
    c_i = lax.broadcasted_iota(I32, (nr, nr), 1)
    later_rows = ((c_i // tiles < r_i // tiles) | ((c_i // tiles == r_i // tiles) & (c_i > r_i))).astype(BF16)
    carry = carry_scr[...]
    suf = within + _dot_exact_lhs01(later_rows, tot) + carry
    carry_scr[...] = carry + jnp.sum(tot, axis=0, keepdims=True)

    k2 = kbuf[slot].reshape(-1, LANES).astype(BF16)
    v2 = vbuf[slot].reshape(-1, LANES).astype(BF16)
    s = _dot_nt(q, k2)
    blocks = [jnp.where(same_head, s[:, c * LANES:(c + 1) * LANES] + suf[c:c + 1, :], NEG_INF) for c in range(nr)]
    blk_max = blocks[0]
    for blk in blocks[1:]:
        blk_max = jnp.maximum(blk_max, blk)
    m_old = m_scr[...]
    m_new = jnp.maximum(m_old, jnp.max(blk_max, axis=-1, keepdims=True))
    alpha = jnp.exp(m_old - m_new)
    probs = [jnp.exp(blk - m_new) for blk in blocks]
    psum = probs[0]
    for pb in probs[1:]:
        psum = psum + pb
    l_scr[...] = alpha * l_scr[...] + jnp.sum(psum, axis=-1, keepdims=True)
    p_all = jnp.concatenate([pb.astype(BF16) for pb in probs], axis=1)
    acc_scr[...] = alpha * acc_scr[...] + _dot_nn(p_all, v2)
    m_scr[...] = m_new

    @pl.when(j == pl.num_programs(1) - 1)
    def _():
        o_ref[...] = acc_scr[...] / l_scr[...]


def _fox_sample_call(page_table, qf, kf, vf, logf, cache_k, cache_v, cache_logf, layer, db, t_new, n_heads, dh,
                     pages_per_step):
    assert dh == LANES
    rows = t_new * n_heads
    n_pool, page = cache_k.shape[1:3]
    n_pages = page_table.shape[1]
    tiles = page * n_heads // LANES
    P = pages_per_step
    steps = n_pages // P
    pad_rows = lambda a: jnp.pad(a.reshape(db, rows, dh), ((0, 0), (0, LANES - rows), (0, 0))).astype(BF16)
    q2 = qf.reshape(db, rows, dh).astype(BF16)
    lfn = jnp.pad(logf.reshape(db, 1, rows), ((0, 0), (0, 7), (0, LANES - rows)))
    lf_pages = cache_logf.reshape(cache_logf.shape[0], n_pool, tiles, LANES)

    per_seq = lambda r: pl.BlockSpec((None, r, LANES), lambda b, j, pt: (b, 0, 0))
    hbm = pl.BlockSpec(memory_space=pl.ANY)
    out = pl.pallas_call(
        functools.partial(_fox_sample_body, n_heads=n_heads, pages_per_step=P, layer=layer, n_pages=n_pages),
        grid_spec=pltpu.PrefetchScalarGridSpec(
            num_scalar_prefetch=1,
            grid=(db, steps),
            in_specs=[per_seq(rows), per_seq(LANES), per_seq(LANES), per_seq(8), hbm, hbm, hbm],
            out_specs=per_seq(rows),
            scratch_shapes=[pltpu.VMEM((2, P, page, n_heads, dh), F32), pltpu.VMEM((2, P, page, n_heads, dh), F32),
                            pltpu.VMEM((2, P, tiles, LANES), F32), pltpu.SemaphoreType.DMA((2,)),
                            pltpu.VMEM((rows, 1), F32), pltpu.VMEM((rows, 1), F32),
                            pltpu.VMEM((rows, LANES), F32), pltpu.VMEM((1, LANES), F32)]),
        out_shape=jax.ShapeDtypeStruct((db, rows, LANES), F32),
        compiler_params=_cparams(("arbitrary", "arbitrary")),
        name="fox_sample",
    )(page_table, q2, pad_rows(kf), pad_rows(vf), lfn, cache_k, cache_v, lf_pages)
    return out.reshape(db * t_new, n_heads * dh)


def _rms_heads(x, n_heads, dh):
    outs = []
    for h in range(n_heads):
        xs = x[:, h * dh:(h + 1) * dh]
        outs.append(xs * lax.rsqrt(jnp.mean(xs * xs, axis=-1, keepdims=True) + EPS))
    return jnp.concatenate(outs, axis=1)


def _route(logits):
    lane = lax.broadcasted_iota(I32, logits.shape, 1)
    big = jnp.int32(LANES)
    gl = jnp.where(lane < N_GROUPS, logits, NEG_INF)
    gmax = jnp.max(gl, axis=-1, keepdims=True)
    g_sel = jnp.min(jnp.where(gl == gmax, lane, big), axis=-1, keepdims=True)
    p_g = 1.0 / jnp.sum(jnp.exp(gl - gmax), axis=-1, keepdims=True)
    lo = N_GROUPS + E_PER_GROUP * g_sel
    ev = jnp.where((lane >= lo) & (lane < lo + E_PER_GROUP), logits, NEG_INF)
    v1 = jnp.max(ev, axis=-1, keepdims=True)
    i1 = jnp.min(jnp.where(ev == v1, lane, big), axis=-1, keepdims=True)
    ev2 = jnp.where(lane == i1, NEG_INF, ev)
    v2 = jnp.max(ev2, axis=-1, keepdims=True)
    i2 = jnp.min(jnp.where(ev2 == v2, lane, big), axis=-1, keepdims=True)
    e21 = jnp.exp(v2 - v1)
    w1 = p_g / (1.0 + e21)
    w2 = p_g * e21 / (1.0 + e21)
    out = jnp.where(lane == 0, (i1 - N_GROUPS).astype(F32), 0.0)
    out = jnp.where(lane == 1, (i2 - N_GROUPS).astype(F32), out)
    out = jnp.where(lane == 2, w1, out)
    out = jnp.where(lane == 3, w2, out)
    return out


def _store_slabs(ref, x):
    tm, d = x.shape
    per = d // LANES
    for s in range(per):
        ref[pl.ds(s, tm, stride=per), :] = x[:, s * LANES:(s + 1) * LANES]


def _load_slabs(ref, tm, per):
    return jnp.concatenate([ref[pl.ds(s, tm, stride=per), :] for s in range(per)], axis=1)


def _merge_body(x_ref, og_ref, gg_ref, of_ref, gw_ref, fw_ref, wo_ref, g1_ref, sh2_ref, sc2_ref, n2_ref,
                wr_hi_ref, wr_lo_ref, br_ref, *rest, n_heads, dh, n_tiles):
    x1_ref, h2_ref, route_ref = rest[-3:]
    i = pl.program_id(0)

    @pl.when(i < n_tiles)
    def _():
        og = _rms_heads(og_ref[...], n_heads, dh) * gw_ref[...] * _silu(gg_ref[...])
        of = _rms_heads(of_ref[...], n_heads, dh) * fw_ref[...]
        merged = jnp.concatenate([og, of], axis=1).astype(BF16)
        x1 = x_ref[...] + g1_ref[...] * _dot_nn(merged, wo_ref[...])
        x1_ref[...] = x1
        y = x1 * lax.rsqrt(jnp.mean(x1 * x1, axis=-1, keepdims=True) + EPS) * n2_ref[...]
        h2 = y * (1.0 + sc2_ref[...]) + sh2_ref[...]
        _store_slabs(h2_ref, h2)
        hh, hl, _ = _split3(h2)
        logits = (_dot_nn(hh, wr_hi_ref[...]) + _dot_nn(hl, wr_hi_ref[...]) + _dot_nn(hh, wr_lo_ref[...])
                  + br_ref[...])
        route_ref[...] = _route(logits)

    @pl.when(i >= n_tiles)
    def _():
        h2_ref[...] = jnp.zeros(h2_ref.shape, F32)


def _merge(x, o_gla, gla_in, o_fox, gla_w, fox_w, w_out_bf, g1, sh2, sc2, norm2_w, wr_hi, wr_lo, b_r,
           tm, rows_per_mod, n_heads, dh, slab_rows, row0, into=None):
    n, d = x.shape
    half = n_heads * dh
    blk0 = row0 // tm
    per = d // LANES
    n_tiles = n // tm
    first = into is None
    extra = -(-(slab_rows - n) // tm) if first else 0
    assert extra <= 1
    aliased = [] if first else [into]
    n_in = 14
    last = n_tiles - 1
    per_token = rows_per_mod == 1
    if per_token:
        mod_spec = pl.BlockSpec((tm, d), lambda i: (jnp.minimum(i, last), 0))
    else:
        tiles_per_mod = rows_per_mod // tm
        g1, sh2, sc2 = (a.reshape(a.shape[0], 1, d) for a in (g1, sh2, sc2))
        mod_spec = pl.BlockSpec((None, 1, d), lambda i: (jnp.minimum(i, last) // tiles_per_mod, 0, 0))
    row = lambda i: (jnp.minimum(i, last), 0)
    fixed = lambda i: (0, 0)
    return pl.pallas_call(
        functools.partial(_merge_body, n_heads=n_heads, dh=dh, n_tiles=n_tiles),
        grid=(n_tiles + extra,),
        in_specs=[pl.BlockSpec((tm, d), row),
                  pl.BlockSpec((tm, half), row),
                  pl.BlockSpec((tm, half), lambda i: (jnp.minimum(i, last), 2)),
                  pl.BlockSpec((tm, half), row),
                  pl.BlockSpec((1, half), fixed), pl.BlockSpec((1, half), fixed),
                  pl.BlockSpec((d, d), fixed),
                  mod_spec, mod_spec, mod_spec,
                  pl.BlockSpec((1, d), fixed),
                  pl.BlockSpec((d, LANES), fixed), pl.BlockSpec((d, LANES), fixed),
                  pl.BlockSpec((1, LANES), fixed)] + [pl.BlockSpec(memory_space=pl.ANY)] * len(aliased),
        out_specs=(pl.BlockSpec((tm, d), row), pl.BlockSpec((tm * per, LANES), lambda i: (blk0 + i, 0)),
                   pl.BlockSpec((tm, LANES), row)),
        out_shape=(jax.ShapeDtypeStruct((n, d), F32), jax.ShapeDtypeStruct((slab_rows * per, LANES), F32),
                   jax.ShapeDtypeStruct((n, LANES), F32)),
        input_output_aliases={n_in: 1} if aliased else {},
        compiler_params=_cparams(("arbitrary",)),
        name="merge_outproj",
    )(x, o_gla, gla_in, o_fox, gla_w.reshape(1, half), fox_w.reshape(1, half), w_out_bf, g1, sh2, sc2,
      norm2_w.reshape(1, d), wr_hi, wr_lo, b_r, *aliased)


GATHER_UNROLL = 8


def _experts_body(te_ref, first_ref, tok_ref, nt_ref, grp_ref, nxt_ref, h_hbm, wg_hbm, wu_hbm, wd_hbm, y_ref,
                  xbuf, wg_f, wu_f, wd_f, wg_bf, wu_bf, wd_bf, sem, wsem, *, tm, per):
    i = pl.program_id(0)
    n_active = nt_ref[0]

    def weight_copies(expert, wslot):
        return (pltpu.make_async_copy(wg_hbm.at[expert], wg_f.at[wslot], wsem.at[wslot]),
                pltpu.make_async_copy(wu_hbm.at[expert], wu_f.at[wslot], wsem.at[wslot]),
                pltpu.make_async_copy(wd_hbm.at[expert], wd_f.at[wslot], wsem.at[wslot]))

    def gather(tile, slot):
        first = first_ref[tile]

        def issue(g, carry):
            for u in range(GATHER_UNROLL):
                r = g * GATHER_UNROLL + u
                src = pl.multiple_of(tok_ref[first + r] * per, per)
                pltpu.make_async_copy(h_hbm.at[pl.ds(src, per), :], xbuf.at[slot, pl.ds(r * per, per), :],
                                      sem.at[slot]).start()
            return carry
        lax.fori_loop(0, tm // GATHER_UNROLL, issue, 0)

    @pl.when((i == 0) & (n_active > 0))
    def _():
        gather(0, 0)
        for cp in weight_copies(te_ref[0], 0):
            cp.start()

    @pl.when(i + 1 < n_active)
    def _():
        gather(i + 1, (i + 1) % 2)

    @pl.when(i < n_active)
    def _():
        slot = i % 2
        prev = te_ref[jnp.maximum(i - 1, 0)]

        @pl.when((i == 0) | (te_ref[i] != prev))
        def _():
            wslot = grp_ref[i] % 2

            @pl.when(nxt_ref[i] >= 0)
            def _():
                for cp in weight_copies(nxt_ref[i], 1 - wslot):
                    cp.start()

            for cp in weight_copies(te_ref[i], wslot):
                cp.wait()
            wg_bf[...] = wg_f[wslot].astype(BF16)
            wu_bf[...] = wu_f[wslot].astype(BF16)
            wd_bf[...] = wd_f[wslot].astype(BF16)

        pltpu.make_async_copy(h_hbm.at[pl.ds(0, tm * per), :], xbuf.at[slot], sem.at[slot]).wait()
        x = _load_slabs(xbuf.at[slot], tm, per).astype(BF16)
        a = (_silu(_dot_nn(x, wg_bf[...])) * _dot_nn(x, wu_bf[...])).astype(BF16)
        _store_slabs(y_ref, _dot_nn(a, wd_bf[...]))

    @pl.when(i >= n_active)
    def _():
        y_ref[...] = jnp.zeros(y_ref.shape, F32)


def _experts(tile_expert, tile_first, sorted_tok, n_tiles, tile_group, next_expert, h2_slabs,
             w_gate, w_up, w_down, tm):
    max_tiles = tile_expert.shape[0]
    _, d, dff = w_gate.shape
    per = d // LANES
    hbm = pl.BlockSpec(memory_space=pl.ANY)
    return pl.pallas_call(
        functools.partial(_experts_body, tm=tm, per=per),
        grid_spec=pltpu.PrefetchScalarGridSpec(
            num_scalar_prefetch=6,
            grid=(max_tiles,),
            in_specs=[hbm, hbm, hbm, hbm],
            out_specs=pl.BlockSpec((tm * per, LANES), lambda i, *prefetch: (i, 0)),
            scratch_shapes=[pltpu.VMEM((2, tm * per, LANES), F32),
                            pltpu.VMEM((2, d, dff), F32), pltpu.VMEM((2, d, dff), F32), pltpu.VMEM((2, dff, d), F32),
                            pltpu.VMEM((d, dff), BF16), pltpu.VMEM((d, dff), BF16), pltpu.VMEM((dff, d), BF16),
                            pltpu.SemaphoreType.DMA((2,)), pltpu.SemaphoreType.DMA((2,))]),
        out_shape=jax.ShapeDtypeStruct((max_tiles * tm * per, LANES), F32),
        compiler_params=_cparams(("arbitrary",)),
        name="experts",
    )(tile_expert, tile_first, sorted_tok, n_tiles, tile_group, next_expert, h2_slabs, w_gate, w_up, w_down)


def _combine_body(pos_ref, y_hbm, x1_ref, route_ref, g2_ref, fw_ref, o_ref, buf, sem, *, tm, row0, per):
    i = pl.program_id(0)

    def gather(tile, slot):
        def issue(g, carry):
            for u in range(GATHER_UNROLL // 2):
                r = g * (GATHER_UNROLL // 2) + u
                tok = row0 + tile * tm + r
                for c in range(2):
                    src = pl.multiple_of(pos_ref[2 * tok + c] * per, per)
                    pltpu.make_async_copy(y_hbm.at[pl.ds(src, per), :],
                                          buf.at[slot, c, pl.ds(r * per, per), :], sem.at[slot]).start(priority=1)
            return carry
        lax.fori_loop(0, tm // (GATHER_UNROLL // 2), issue, 0)

    @pl.when(i == 0)
    def _():
        gather(0, 0)

    @pl.when(i + 1 < pl.num_programs(0))
    def _():
        gather(i + 1, (i + 1) % 2)

    slot = i % 2
    for c in range(2):
        pltpu.make_async_copy(y_hbm.at[pl.ds(0, tm * per), :], buf.at[slot, c], sem.at[slot]).wait()
    route = route_ref[...]
    moe = (route[:, 2:3] * _load_slabs(buf.at[slot, 0], tm, per)
           + route[:, 3:4] * _load_slabs(buf.at[slot, 1], tm, per))
    x2 = x1_ref[...] + g2_ref[...] * moe
    o_ref[...] = x2 * lax.rsqrt(jnp.mean(x2 * x2, axis=-1, keepdims=True) + EPS) * fw_ref[...]


def _combine(pos, y_slabs, x1, route, g2, final_w, row0, n_rows, tm, rows_per_mod):
    d = x1.shape[1]
    per = d // LANES
    per_token = rows_per_mod == 1
    if per_token:
        mod_spec = pl.BlockSpec((tm, d), lambda i, pos: (i, 0))
    else:
        tiles_per_mod = rows_per_mod // tm
        g2 = g2.reshape(g2.shape[0], 1, d)
        mod_spec = pl.BlockSpec((None, 1, d), lambda i, pos: (i // tiles_per_mod, 0, 0))
    return pl.pallas_call(
        functools.partial(_combine_body, tm=tm, row0=row0, per=per),
        grid_spec=pltpu.PrefetchScalarGridSpec(
            num_scalar_prefetch=1,
            grid=(n_rows // tm,),
            in_specs=[pl.BlockSpec(memory_space=pl.ANY),
                      pl.BlockSpec((tm, d), lambda i, pos: (i, 0)),
                      pl.BlockSpec((tm, LANES), lambda i, pos: (i, 0)),
                      mod_spec,
                      pl.BlockSpec((1, d), lambda i, pos: (0, 0))],
            out_specs=pl.BlockSpec((tm, d), lambda i, pos: (i, 0)),
            scratch_shapes=[pltpu.VMEM((2, 2, tm * per, LANES), F32), pltpu.SemaphoreType.DMA((2,))]),
        out_shape=jax.ShapeDtypeStruct((n_rows, d), F32),
        compiler_params=_cparams(("arbitrary",)),
        name="combine_norm",
    )(pos, y_slabs, x1, route, g2, final_w.reshape(1, d))


def _invert_body(where_ref, out_ref, *, n_pairs, n_out):
    def fill(g, carry):
        for u in range(GATHER_UNROLL):
            a = g * GATHER_UNROLL + u
            out_ref[where_ref[a]] = lax.shift_right_logical(a, 1)
        return carry
    lax.fori_loop(0, n_pairs // GATHER_UNROLL, fill, 0)

    def tail(t, carry):
        out_ref[n_pairs + t] = 0
        return carry
    lax.fori_loop(0, n_out - n_pairs, tail, 0)


def _sorted_tokens(where, n_out):
    n_pairs = where.shape[0]
    assert n_pairs % GATHER_UNROLL == 0
    smem = pl.BlockSpec(memory_space=pltpu.SMEM)
    return pl.pallas_call(
        functools.partial(_invert_body, n_pairs=n_pairs, n_out=n_out),
        in_specs=[smem], out_specs=smem,
        out_shape=jax.ShapeDtypeStruct((n_out,), I32),
        name="sorted_tokens",
    )(where)


def _plan(route, n_experts, tm):
    n = route.shape[0]
    e_flat = route[:, 0:2].astype(I32).reshape(-1)
    onehot = (e_flat[:, None] == jnp.arange(n_experts, dtype=I32)[None, :]).astype(I32)
    csum = jnp.cumsum(onehot, axis=0)
    counts = csum[-1]
    rank = jnp.sum(onehot * csum, axis=1) - 1
    tiles_e = (counts + tm - 1) // tm
    tile_end = jnp.cumsum(tiles_e)
    tile_off = tile_end - tiles_e
    cnt_off = jnp.cumsum(counts) - counts
    max_tiles = (2 * n) // tm + n_experts
    pos = jnp.sum(onehot * (tile_off * tm)[None, :], axis=1) + rank
    where = jnp.sum(onehot * cnt_off[None, :], axis=1) + rank
    tile_ids = jnp.arange(max_tiles, dtype=I32)
    live_ids = jnp.minimum(tile_ids, tile_end[-1] - 1)
    tile_expert = jnp.minimum(jnp.sum((tile_end[None, :] <= live_ids[:, None]).astype(I32), axis=1), n_experts - 1)
    is_e = (tile_expert[:, None] == jnp.arange(n_experts, dtype=I32)[None, :]).astype(I32)
    tile_first = jnp.sum(is_e * (cnt_off - tile_off * tm)[None, :], axis=1) + tile_ids * tm
    tile_first = jnp.clip(tile_first, 0, 2 * n)
    e_ids = jnp.arange(n_experts, dtype=I32)
    used = tiles_e > 0
    run_of_e = jnp.cumsum(used.astype(I32)) - 1
    later_used = used[None, :] & (e_ids[None, :] > e_ids[:, None])
    next_of_e = jnp.min(jnp.where(later_used, e_ids[None, :], n_experts), axis=1)
    next_of_e = jnp.where(next_of_e < n_experts, next_of_e, -1)
    tile_group = jnp.sum(is_e * run_of_e[None, :], axis=1)
    next_expert = jnp.sum(is_e * next_of_e[None, :], axis=1)
    return (tile_expert, tile_first.astype(I32), tile_end[-1].reshape(1).astype(I32), tile_group.astype(I32),
            next_expert.astype(I32), pos.astype(I32), where.astype(I32))


def kernel(x_prompt, x_sample, cache_k, cache_v, cache_logf, state_gla, page_table, c_prompt, c_sample,
           norm1_w, norm2_w, w_ada, b_ada, w_in, w_gk2, b_gk, b_fgate, gla_onorm_w, fox_onorm_w, w_out,
           w_rg, b_rg, w_re, b_re, w_gate_e, w_up_e, w_down_e, final_norm_w):
    depth = w_in.shape[0]
    assert depth == 1, "single-layer trunk"
    batch, seq, d = x_prompt.shape
    db, t_new, _ = x_sample.shape
    n_hg, dk, dv = state_gla.shape[2:]
    n_hf, dh = cache_k.shape[3:]
    rank = w_gk2.shape[1]
    n_experts = w_gate_e.shape[1]
    wq = n_hg * dk
    wv = n_hg * dv
    wf = n_hf * dh
    assert wv == 1024 and wf == 1024 and 2 * wq == 1024 and rank == 16 and n_hf == 8
    n_p = batch * seq
    n_s = db * t_new
    layer = 0

    wi = w_in[layer]
    o_alr = 2 * wq + 2 * wv
    o_fox = o_alr + rank
    o_fl = o_fox + 3 * wf
    assert o_fl - o_fox == o_alr
    w_main = _pack_w_in(wi, o_alr, o_fox)
    w_small = jnp.concatenate([wi[:, o_alr:o_fox], wi[:, o_fl:], jnp.zeros((d, LANES - rank - n_hf), F32)],
                              axis=1).astype(BF16)
    b_small = jnp.zeros((1, LANES), F32).at[0, rank:rank + n_hf].set(b_fgate[layer])
    w_gk2_pad = jnp.zeros((LANES, wq), F32).at[:rank].set(w_gk2[layer])
    b_gk2 = b_gk[layer].reshape(1, wq)
    w_out_bf = w_out[layer].astype(BF16)
    w_r = jnp.concatenate([w_rg[layer], w_re[layer],
                           jnp.zeros((d, LANES - N_GROUPS - n_experts), F32)], axis=1)
    wr_hi = w_r.astype(BF16)
    wr_lo = (w_r - wr_hi.astype(F32)).astype(BF16)
    b_r = jnp.concatenate([b_rg[layer], b_re[layer], jnp.zeros((LANES - N_GROUPS - n_experts,), F32)]).reshape(1, LANES)

    n_c = batch + db
    c_all = jnp.concatenate([c_prompt, c_sample, jnp.zeros((-n_c % 8, d), F32)], axis=0)
    mod = _adaln(c_all, w_ada[layer], b_ada[layer])
    sh1, sc1, g1, sh2, sc2, g2 = (mod[:, i * d:(i + 1) * d] for i in range(6))
    p_rows = slice(0, batch)
    rep = lambda a: jnp.repeat(a[batch:n_c], t_new, axis=0)

    q_scale = dh ** -0.5
    xp = x_prompt.reshape(n_p, d)
    xs = x_sample.reshape(n_s, d)

    tm_p = 512
    gla_p, qf_p, kf_p, vf_p, kfb_p, vfb_p, small_p = _inproj(
        xp, sh1[p_rows], sc1[p_rows], norm1_w[layer], w_main, w_small, b_small, tm_p, seq, q_scale * LOG2E)
    s0_p = jnp.zeros((batch, n_hg, dk, dv), F32)
    o_gla_p, gla_state_p = _gla(gla_p, small_p, w_gk2_pad, b_gk2, s0_p, batch, seq, 128, n_hg, dk, dv)
    logf_p = small_p[:, rank:rank + n_hf]
    key_bias = _key_bias(small_p, batch, seq, rank, n_hf)
    o_fox_p = _fox_prompt(qf_p, kfb_p, vfb_p, key_bias, batch, seq, n_hf, dh, 512)
    n_all = n_p + n_s
    x1_p, h2_p, route_p = _merge(xp, o_gla_p, gla_p, o_fox_p, gla_onorm_w[layer], fox_onorm_w[layer], w_out_bf,
                                 g1[p_rows], sh2[p_rows], sc2[p_rows], norm2_w[layer], wr_hi, wr_lo, b_r,
                                 256, seq, n_hg, dv, n_all, 0)

    gla_s, qf_s, kf_s, vf_s, _, _, small_s = _inproj(
        xs, rep(sh1), rep(sc1), norm1_w[layer], w_main, w_small, b_small, n_s, 1, q_scale)
    t_pad = GLA_CHUNK
    pad_t = lambda a: jnp.pad(a.reshape(db, t_new, -1), ((0, 0), (0, t_pad - t_new), (0, 0))).reshape(db * t_pad, -1)
    o_gla_s_pad, gla_state_s = _gla(pad_t(gla_s), pad_t(small_s), w_gk2_pad, b_gk2, state_gla[layer],
                                    db, t_pad, t_pad, n_hg, dk, dv, t_valid=t_new)
    o_gla_s = o_gla_s_pad.reshape(db, t_pad, wv)[:, :t_new].reshape(n_s, wv)

    logf_s = small_s[:, rank:rank + n_hf]
    o_fox_s = _fox_sample_call(page_table, qf_s, kf_s, vf_s, logf_s, cache_k, cache_v, cache_logf,
                               layer, db, t_new, n_hf, dh, 8)
    x1_s, h2, route_s = _merge(xs, o_gla_s, gla_s, o_fox_s, gla_onorm_w[layer], fox_onorm_w[layer], w_out_bf,
                               rep(g1), rep(sh2), rep(sc2), norm2_w[layer], wr_hi, wr_lo, b_r,
                               n_s, 1, n_hg, dv, n_all, n_p, into=h2_p)

    tm_e = 256
    route = jnp.concatenate([route_p, route_s], axis=0)
    tile_expert, tile_first, n_tiles, tile_group, next_expert, pos, where = _plan(route, n_experts, tm_e)
    sorted_tok = _sorted_tokens(where, where.shape[0] + tm_e)
    y_slabs = _experts(tile_expert, tile_first, sorted_tok, n_tiles, tile_group, next_expert, h2,
                       w_gate_e[layer], w_up_e[layer], w_down_e[layer], tm_e)
    y_p = _combine(pos, y_slabs, x1_p, route_p, g2[p_rows], final_norm_w, 0, n_p, 256, seq)
    y_s = _combine(pos, y_slabs, x1_s, route_s, rep(g2), final_norm_w, n_p, n_s, n_s, 1)

    y_prompt = y_p.reshape(batch, seq, d)
    y_sample = y_s.reshape(db, t_new, d)
    k_prompt = kf_p.reshape(1, batch, seq, n_hf, dh)
    v_prompt = vf_p.reshape(1, batch, seq, n_hf, dh)
    logf_prompt = logf_p.reshape(1, batch, seq, n_hf)
    k_sample = kf_s.reshape(1, db, t_new, n_hf, dh)
    v_sample = vf_s.reshape(1, db, t_new, n_hf, dh)
    logf_sample = logf_s.reshape(1, db, t_new, n_hf)
    return (y_prompt, y_sample, k_prompt, v_prompt, logf_prompt, gla_state_p[None],
            k_sample, v_sample, logf_sample, gla_state_s[None])
```

```python
import functools

import jax
import jax.numpy as jnp
from jax import lax
from jax.experimental import pallas as pl
from jax.experimental.pallas import tpu as pltpu

F32 = jnp.float32
BF16 = jnp.bfloat16
I32 = jnp.int32
EPS = 1e-6
NEG_INF = float("-inf")

LANES = 128
GLA_CHUNK = 16
GLA_GATE_NORM = 16.0
N_GROUPS = 4
E_PER_GROUP = 8
VMEM_LIMIT = 56 * 1024 * 1024


def _cparams(sem):
    return pltpu.CompilerParams(dimension_semantics=sem, vmem_limit_bytes=VMEM_LIMIT)


def _log_sigmoid(z):
    return jnp.minimum(z, 0.0) - jnp.log1p(jnp.exp(-jnp.abs(z)))


def _silu(z):
    return z * jax.nn.sigmoid(z)


def _split3(a):
    hi = a.astype(BF16)
    r = a - hi.astype(F32)
    mid = r.astype(BF16)
    lo = (r - mid.astype(F32)).astype(BF16)
    return hi, mid, lo


def _dot_nn(a, b):
    return jnp.dot(a, b, preferred_element_type=F32)


def _dot_nt(a, b):
    return lax.dot_general(a, b, (((1,), (1,)), ((), ())), preferred_element_type=F32)


def _dot_f32(a, b):
    ah, al, _ = _split3(a)
    bh, bl, _ = _split3(b)
    return _dot_nn(ah, bh) + _dot_nn(al, bh) + _dot_nn(ah, bl)


def _dot_exact_rhs01(a, ones_bf16):
    hi, mid, lo = _split3(a)
    return _dot_nn(hi, ones_bf16) + _dot_nn(mid, ones_bf16) + _dot_nn(lo, ones_bf16)


def _dot_exact_lhs01(ones_bf16, b):
    hi, mid, lo = _split3(b)
    return _dot_nn(ones_bf16, hi) + _dot_nn(ones_bf16, mid) + _dot_nn(ones_bf16, lo)


def _adaln_body(c_ref, w_ref, b_ref, o_ref):
    s = _silu(c_ref[...]).astype(BF16)
    o_ref[...] = _dot_nn(s, w_ref[...].astype(BF16)) + b_ref[...]


def _adaln(c_all, w_ada, b_ada, tn=1024):
    rows, d = c_all.shape
    n6 = w_ada.shape[1]
    return pl.pallas_call(
        _adaln_body,
        grid=(n6 // tn,),
        in_specs=[pl.BlockSpec((rows, d), lambda j: (0, 0)),
                  pl.BlockSpec((d, tn), lambda j: (0, j)),
                  pl.BlockSpec((1, tn), lambda j: (0, j))],
        out_specs=pl.BlockSpec((rows, tn), lambda j: (0, j)),
        out_shape=jax.ShapeDtypeStruct((rows, n6), F32),
        compiler_params=_cparams(("arbitrary",)),
        name="adaln",
    )(c_all, w_ada, b_ada.reshape(1, n6))


def _pack_w_in_body(a_ref, b_ref, c_ref, o_ref, *, shift, n_first):
    j = pl.program_id(0)

    @pl.when(j < n_first)
    def _():
        o_ref[...] = a_ref[...].astype(BF16)

    @pl.when(j >= n_first)
    def _():
        both = jnp.concatenate([b_ref[...], c_ref[...]], axis=0)
        o_ref[...] = both[shift:shift + o_ref.shape[0], :].astype(BF16)


def _pack_w_in(wt, rows, second_start):
    d = wt.shape[1]
    tr = 512
    shift = second_start % tr
    base = second_start - shift
    assert rows % tr == 0 and shift % 16 == 0 and 0 < shift and tr % shift == 0
    n_first = rows // tr
    second = lambda j: jnp.maximum(j - n_first, 0)
    return pl.pallas_call(
        functools.partial(_pack_w_in_body, shift=shift, n_first=n_first),
        grid=(2 * n_first,),
        in_specs=[pl.BlockSpec((tr, d), lambda j: (jnp.minimum(j, n_first - 1), 0)),
                  pl.BlockSpec((tr, d), lambda j: (base // tr + second(j), 0)),
                  pl.BlockSpec((shift, d), lambda j: ((base + tr) // shift + second(j) * (tr // shift), 0))],
        out_specs=pl.BlockSpec((tr, d), lambda j: (j, 0)),
        out_shape=jax.ShapeDtypeStruct((2 * rows, d), BF16),
        compiler_params=_cparams(("arbitrary",)),
        name="pack_w_in",
    )(wt, wt, wt)


def _inproj_body(x_ref, sh_ref, sc_ref, nw_ref, w_ref, ws_ref, bs_ref,
                 gla_ref, qf_ref, kf_ref, vf_ref, kfb_ref, vfb_ref, small_ref, h_scr, *, q_scale):
    j = pl.program_id(1)

    @pl.when(j == 0)
    def _():
        x = x_ref[...]
        y = x * lax.rsqrt(jnp.mean(x * x, axis=-1, keepdims=True) + EPS) * nw_ref[...]
        hb = (y * (1.0 + sc_ref[...]) + sh_ref[...]).astype(BF16)
        h_scr[...] = hb
        sm = _dot_nt(hb, ws_ref[...].astype(BF16))
        lane = lax.broadcasted_iota(I32, sm.shape, 1)
        small_ref[...] = jnp.where((lane >= 16) & (lane < 24), _log_sigmoid(sm + bs_ref[...]), sm)

    acc = _dot_nt(h_scr[...], w_ref[...])

    @pl.when(j < 3)
    def _():
        gla_ref[...] = acc

    @pl.when(j == 3)
    def _():
        qf_ref[...] = (acc * q_scale).astype(BF16)

    @pl.when(j == 4)
    def _():
        kf_ref[...] = acc
        kfb_ref[...] = acc.astype(BF16)

    @pl.when(j == 5)
    def _():
        vf_ref[...] = acc
        vfb_ref[...] = acc.astype(BF16)


def _inproj(x, sh, sc, norm_w, w_main, w_small, b_small, tm, rows_per_mod, q_scale):
    n, d = x.shape
    tn = w_main.shape[0] // 6
    per_token = rows_per_mod == 1
    if per_token:
        mod_spec = pl.BlockSpec((tm, d), lambda i, j: (i, 0))
    else:
        tiles_per_mod = rows_per_mod // tm
        sh = sh.reshape(sh.shape[0], 1, d)
        sc = sc.reshape(sc.shape[0], 1, d)
        mod_spec = pl.BlockSpec((None, 1, d), lambda i, j: (i // tiles_per_mod, 0, 0))
    row = lambda i, j: (i, 0)
    out_shapes = (jax.ShapeDtypeStruct((n, 3 * tn), F32),
                  jax.ShapeDtypeStruct((n, tn), BF16),
                  jax.ShapeDtypeStruct((n, tn), F32),
                  jax.ShapeDtypeStruct((n, tn), F32),
                  jax.ShapeDtypeStruct((n, tn), BF16),
                  jax.ShapeDtypeStruct((n, tn), BF16),
                  jax.ShapeDtypeStruct((n, LANES), F32))
    return pl.pallas_call(
        functools.partial(_inproj_body, q_scale=q_scale),
        grid=(n // tm, 6),
        in_specs=[pl.BlockSpec((tm, d), row), mod_spec, mod_spec,
                  pl.BlockSpec((1, d), lambda i, j: (0, 0)),
                  pl.BlockSpec((tn, d), lambda i, j: (j, 0)),
                  pl.BlockSpec((LANES, d), lambda i, j: (0, 0)),
                  pl.BlockSpec((1, LANES), lambda i, j: (0, 0))],
        out_specs=(pl.BlockSpec((tm, tn), lambda i, j: (i, jnp.minimum(j, 2))),
                   pl.BlockSpec((tm, tn), row), pl.BlockSpec((tm, tn), row), pl.BlockSpec((tm, tn), row),
                   pl.BlockSpec((tm, tn), row), pl.BlockSpec((tm, tn), row),
                   pl.BlockSpec((tm, LANES), row)),
        out_shape=out_shapes,
        scratch_shapes=[pltpu.VMEM((tm, d), BF16)],
        compiler_params=_cparams(("arbitrary", "arbitrary")),
        name="inproj",
    )(x, sh, sc, norm_w.reshape(1, d), w_main, w_small, b_small)


def _level_ref(b, half, rows_i):
    tt, n = b.shape
    if half >= 4:
        size = 2 * half
        blocks = [jnp.broadcast_to(b[g * size + half - 1:g * size + half, :], (size, n)) for g in range(tt // size)]
        return blocks[0] if len(blocks) == 1 else jnp.concatenate(blocks, axis=0)
    if half == 2:
        lo = jnp.concatenate([jnp.broadcast_to(b[8 * g + 1:8 * g + 2, :], (8, n)) for g in range(tt // 8)], axis=0)
        hi = jnp.concatenate([jnp.broadcast_to(b[8 * g + 5:8 * g + 6, :], (8, n)) for g in range(tt // 8)], axis=0)
        return jnp.where(rows_i % 8 < 4, lo, hi)
    return jnp.where(rows_i % 2 == 1, pltpu.roll(b, 1, 0), b)


def _gla_body(q_ref, k_ref, v_ref, sm_ref, w2_ref, b2_ref, s0_ref, o_ref, sout_ref, st_scr,
              *, tt, dk, dv, n_pairs, t_valid):
    t = pl.program_id(1)
    zero_blk = jnp.zeros((dv, dk), F32)

    @pl.when(t == 0)
    def _():
        for p in range(n_pairs):
            top = jnp.concatenate([s0_ref[2 * p].T, zero_blk], axis=1)
            bot = jnp.concatenate([zero_blk, s0_ref[2 * p + 1].T], axis=1)
            st_scr[p] = jnp.concatenate([top, bot], axis=0)

    gk = _log_sigmoid(_dot_f32(sm_ref[...], w2_ref[...]) + b2_ref[...]) / GLA_GATE_NORM
    rows_i = lax.broadcasted_iota(I32, (tt, 1), 0)
    if t_valid < tt:
        gk = jnp.where(rows_i < t_valid, gk, 0.0)
    r_i = lax.broadcasted_iota(I32, (tt, tt), 0)
    c_i = lax.broadcasted_iota(I32, (tt, tt), 1)
    b_all = _dot_exact_lhs01((c_i <= r_i).astype(BF16), gk)

    halves = []
    half = tt // 2
    while half >= 1:
        halves.append(half)
        half //= 2
    refs = [_level_ref(b_all, hf, rows_i) for hf in halves]
    valid = [(r_i // (2 * hf) == c_i // (2 * hf)) & (r_i % (2 * hf) >= hf) & (c_i % (2 * hf) < hf) for hf in halves]
    on_diag = r_i == c_i
    head0 = lax.broadcasted_iota(I32, (tt, 2 * dk), 1) < dk
    rr = lax.broadcasted_iota(I32, (2 * dv, 2 * dk), 0) // dv
    cc = lax.broadcasted_iota(I32, (2 * dv, 2 * dk), 1) // dk
    diag = (rr == cc).astype(F32)

    def head_scores(qx, kx):
        stacked = jnp.concatenate([jnp.where(head0, qx, 0.0), jnp.where(head0, 0.0, qx)], axis=0).astype(BF16)
        return _dot_nt(stacked, kx.astype(BF16))

    for p in range(n_pairs):
        cs = slice(p * 2 * dk, (p + 1) * 2 * dk)
        vs = slice(p * 2 * dv, (p + 1) * 2 * dv)
        q = q_ref[:, cs] * (dk ** -0.5)
        k = k_ref[:, cs]
        v = v_ref[:, vs]
        b = b_all[:, cs]
        sc = head_scores(q, k)
        a0 = jnp.where(on_diag, sc[:tt], 0.0)
        a1 = jnp.where(on_diag, sc[tt:], 0.0)
        for ref_all, ok in zip(refs, valid):
            ref = ref_all[:, cs]
            sc = head_scores(q * jnp.exp(jnp.minimum(b - ref, 0.0)), k * jnp.exp(jnp.minimum(ref - b, 0.0)))
            a0 = jnp.where(ok, sc[:tt], a0)
            a1 = jnp.where(ok, sc[tt:], a1)
        vb = v.astype(BF16)
        o_intra = jnp.concatenate([_dot_nn(a0.astype(BF16), vb[:, :dv]), _dot_nn(a1.astype(BF16), vb[:, dv:])], axis=1)

        st = st_scr[p]
        bend = b[tt - 1:tt]
        o_inter = _dot_nt((q * jnp.exp(b)).astype(BF16), st.astype(BF16))
        kdec = (k * jnp.exp(bend - b)).astype(BF16)
        st = st * jnp.exp(bend) + _dot_nn(v.T.astype(BF16), kdec) * diag
        st_scr[p] = st
        o_ref[:, vs] = o_intra + o_inter

    @pl.when(t == pl.num_programs(1) - 1)
    def _():
        for p in range(n_pairs):
            st = st_scr[p]
            sout_ref[2 * p] = st[:dv, :dk].T
            sout_ref[2 * p + 1] = st[dv:, dk:].T


def _gla(gla_in, small, w_gk2_pad, b_gk, s0, batch, seq, tt, n_heads, dk, dv, t_valid=None):
    n = batch * seq
    nt = seq // tt
    wq = n_heads * dk
    wv = n_heads * dv
    assert wv == 2 * wq
    row = lambda b, t: b * nt + t
    return pl.pallas_call(
        functools.partial(_gla_body, tt=tt, dk=dk, dv=dv, n_pairs=n_heads // 2,
                          t_valid=tt if t_valid is None else t_valid),
        grid=(batch, nt),
        in_specs=[pl.BlockSpec((tt, wq), lambda b, t: (row(b, t), 0)),
                  pl.BlockSpec((tt, wq), lambda b, t: (row(b, t), 1)),
                  pl.BlockSpec((tt, wv), lambda b, t: (row(b, t), 1)),
                  pl.BlockSpec((tt, LANES), lambda b, t: (row(b, t), 0)),
                  pl.BlockSpec((LANES, wq), lambda b, t: (0, 0)),
                  pl.BlockSpec((1, wq), lambda b, t: (0, 0)),
                  pl.BlockSpec((None, n_heads, dk, dv), lambda b, t: (b, 0, 0, 0))],
        out_specs=(pl.BlockSpec((tt, wv), lambda b, t: (row(b, t), 0)),
                   pl.BlockSpec((None, n_heads, dk, dv), lambda b, t: (b, 0, 0, 0))),
        out_shape=(jax.ShapeDtypeStruct((n, wv), F32),
                   jax.ShapeDtypeStruct((batch, n_heads, dk, dv), F32)),
        scratch_shapes=[pltpu.VMEM((n_heads // 2, 2 * dv, 2 * dk), F32)],
        compiler_params=_cparams(("arbitrary", "arbitrary")),
        name="gla",
    )(gla_in, gla_in, gla_in, small, w_gk2_pad, b_gk, s0)


LOG2E = 1.4426950408889634
BIAS_PARTS = 3


def _key_bias_body(sm_ref, o_ref, *, lane0, n_heads, blk):
    s = sm_ref.shape[0]
    r_i = lax.broadcasted_iota(I32, (blk, blk), 0)
    c_i = lax.broadcasted_iota(I32, (blk, blk), 1)
    lower = (c_i <= r_i).astype(BF16)
    head = r_i - lane0
    sel = [((head >= 0) & (head < n_heads) & (c_i == BIAS_PARTS * head + part)).astype(BF16)
           for part in range(BIAS_PARTS)]
    carry = jnp.zeros((1, sm_ref.shape[1]), F32)
    for i in range(s // blk):
        f = _dot_exact_lhs01(lower, sm_ref[i * blk:(i + 1) * blk, :]) + carry
        carry = f[blk - 1:blk, :]
        pieces = _split3(f * (-LOG2E))
        placed = sum(_dot_nn(piece, sel_p) for piece, sel_p in zip(pieces, sel))
        o_ref[i * blk:(i + 1) * blk, :] = placed.astype(BF16)


def _key_bias(small, batch, seq, lane0, n_heads):
    return pl.pallas_call(
        functools.partial(_key_bias_body, lane0=lane0, n_heads=n_heads, blk=LANES),
        grid=(batch,),
        in_specs=[pl.BlockSpec((seq, LANES), lambda i: (i, 0))],
        out_specs=pl.BlockSpec((seq, LANES), lambda i: (i, 0)),
        out_shape=jax.ShapeDtypeStruct((batch * seq, LANES), BF16),
        compiler_params=_cparams(("arbitrary",)),
        name="fox_key_bias",
    )(small)


def _fox_prompt_body(q_ref, k_ref, v_ref, kb_ref, o_ref, m_scr, l_scr, acc_scr, *, n_heads, dh):
    qi = pl.program_id(1)
    ki = pl.program_id(2)
    tq = q_ref.shape[0]
    tk = k_ref.shape[0]

    @pl.when(ki == 0)
    def _():
        m_scr[...] = jnp.full(m_scr.shape, NEG_INF, F32)
        l_scr[...] = jnp.zeros(l_scr.shape, F32)
        acc_scr[...] = jnp.zeros(acc_scr.shape, F32)

    def step(masked):
        lane = lax.broadcasted_iota(I32, (tq, LANES), 1)
        kb = kb_ref[...]
        if masked:
            keep = lax.broadcasted_iota(I32, (tk, tq), 0) <= lax.broadcasted_iota(I32, (tk, tq), 1)
        for h in range(n_heads):
            hs = slice(h * dh, (h + 1) * dh)
            ones = ((lane >= BIAS_PARTS * h) & (lane < BIAS_PARTS * (h + 1))).astype(BF16)
            q_aug = jnp.concatenate([q_ref[:, hs], ones], axis=1)
            k_aug = jnp.concatenate([k_ref[:, hs], kb], axis=1)
            s_t = _dot_nt(k_aug, q_aug)
            if masked:
                s_t = jnp.where(keep, s_t, NEG_INF)
            m_old = m_scr[h]
            m_new = jnp.maximum(m_old, jnp.max(s_t, axis=0, keepdims=True))
            alpha = jnp.exp2(m_old - m_new)
            p_t = jnp.exp2(s_t - m_new)
            l_scr[h] = alpha * l_scr[h] + jnp.sum(p_t, axis=0, keepdims=True)
            acc_scr[h] = alpha * acc_scr[h] + _dot_nn(v_ref[:, hs].T, p_t.astype(BF16))
            m_scr[h] = m_new

    @pl.when(ki < qi)
    def _():
        step(False)

    @pl.when(ki == qi)
    def _():
        step(True)

    @pl.when(ki == pl.num_programs(2) - 1)
    def _():
        for h in range(n_heads):
            o_ref[:, h * dh:(h + 1) * dh] = (acc_scr[h] / l_scr[h]).T


def _fox_prompt(qf, kfb, vfb, key_bias, batch, seq, n_heads, dh, tq):
    nq = seq // tq
    width = n_heads * dh
    kv_map = lambda b, qi, ki: (b * nq + jnp.minimum(ki, qi), 0)
    return pl.pallas_call(
        functools.partial(_fox_prompt_body, n_heads=n_heads, dh=dh),
        grid=(batch, nq, nq),
        in_specs=[pl.BlockSpec((tq, width), lambda b, qi, ki: (b * nq + qi, 0)),
                  pl.BlockSpec((tq, width), kv_map),
                  pl.BlockSpec((tq, width), kv_map),
                  pl.BlockSpec((tq, LANES), kv_map)],
        out_specs=pl.BlockSpec((tq, width), lambda b, qi, ki: (b * nq + qi, 0)),
        out_shape=jax.ShapeDtypeStruct((batch * seq, width), F32),
        scratch_shapes=[pltpu.VMEM((n_heads, 1, tq), F32), pltpu.VMEM((n_heads, 1, tq), F32),
                        pltpu.VMEM((n_heads, dh, tq), F32)],
        compiler_params=_cparams(("arbitrary", "arbitrary", "arbitrary")),
        name="fox_prompt",
    )(qf, kfb, vfb, key_bias)


def _fox_sample_body(pt_ref, q_ref, kn_ref, vn_ref, lfn_ref, ck_hbm, cv_hbm, lf_hbm, o_ref,
                     kbuf, vbuf, lfbuf, sem, m_scr, l_scr, acc_scr, carry_scr,
                     *, n_heads, pages_per_step, layer, n_pages):
    P = pages_per_step
    b = pl.program_id(0)
    j = pl.program_id(1)
    steps = pl.num_programs(1)
    rows = q_ref.shape[0]
    tiles = lfbuf.shape[2]

    def fetch(seq, stp, slot):
        for i in range(P):
            pg = pt_ref[seq, n_pages - 1 - (stp * P + i)]
            pltpu.make_async_copy(ck_hbm.at[layer, pg], kbuf.at[slot, i], sem.at[slot]).start()
            pltpu.make_async_copy(cv_hbm.at[layer, pg], vbuf.at[slot, i], sem.at[slot]).start()
            pltpu.make_async_copy(lf_hbm.at[layer, pg], lfbuf.at[slot, i], sem.at[slot]).start()

    step = b * steps + j
    slot = step % 2

    @pl.when(step == 0)
    def _():
        fetch(0, 0, 0)

    @pl.when(step + 1 < pl.num_programs(0) * steps)
    def _():
        wrap = j + 1 == steps
        fetch(jnp.where(wrap, b + 1, b), jnp.where(wrap, 0, j + 1), 1 - slot)

    q = q_ref[...]
    lane = lax.broadcasted_iota(I32, (rows, LANES), 1)
    rowi = lax.broadcasted_iota(I32, (rows, LANES), 0)
    same_head = (lane % n_heads) == (rowi % n_heads)
    l_i = lax.broadcasted_iota(I32, (LANES, LANES), 0)
    l_j = lax.broadcasted_iota(I32, (LANES, LANES), 1)
    head_eq = (l_i % n_heads) == (l_j % n_heads)

    @pl.when(j == 0)
    def _():
        fn = _dot_exact_rhs01(lfn_ref[...], (head_eq & (l_i <= l_j)).astype(BF16))[0:1, :]
        s = _dot_nt(q, kn_ref[...]) - fn
        keep = same_head & (lane // n_heads <= rowi // n_heads) & (lane < rows)
        s = jnp.where(keep, s, NEG_INF)
        m = jnp.max(s, axis=-1, keepdims=True)
        p = jnp.exp(s - m)
        m_scr[...] = m
        l_scr[...] = jnp.sum(p, axis=-1, keepdims=True)
        acc_scr[...] = _dot_nn(p.astype(BF16), vn_ref[...])
        carry_scr[...] = jnp.zeros(carry_scr.shape, F32)

    pltpu.make_async_copy(ck_hbm.at[layer, pl.ds(0, P)], kbuf.at[slot], sem.at[slot]).wait()
    pltpu.make_async_copy(cv_hbm.at[layer, pl.ds(0, P)], vbuf.at[slot], sem.at[slot]).wait()
    pltpu.make_async_copy(lf_hbm.at[layer, pl.ds(0, P)], lfbuf.at[slot], sem.at[slot]).wait()

    x = lfbuf[slot].reshape(P * tiles, LANES)
    within = _dot_exact_rhs01(x, (head_eq & (l_i > l_j)).astype(BF16))
    tot = _dot_exact_rhs01(x, head_eq.astype(BF16))
    nr = P * tiles
    r_i = lax.broadcasted_iota(I32, (nr, nr), 0)
    c_i = lax.broadcasted_iota(I32, (nr, nr), 1)
    later_rows = ((c_i // tiles < r_i // tiles) | ((c_i // tiles == r_i // tiles) & (c_i > r_i))).astype(BF16)
    carry = carry_scr[...]
    suf = within + _dot_exact_lhs01(later_rows, tot) + carry
    carry_scr[...] = carry + jnp.sum(tot, axis=0, keepdims=True)

    k2 = kbuf[slot].reshape(-1, LANES).astype(BF16)
    v2 = vbuf[slot].reshape(-1, LANES).astype(BF16)
    s = _dot_nt(q, k2)
    blocks = [jnp.where(same_head, s[:, c * LANES:(c + 1) * LANES] + suf[c:c + 1, :], NEG_INF) for c in range(nr)]
    blk_max = blocks[0]
    for blk in blocks[1:]:
        blk_max = jnp.maximum(blk_max, blk)
    m_old = m_scr[...]
    m_new = jnp.maximum(m_old, jnp.max(blk_max, axis=-1, keepdims=True))
    alpha = jnp.exp(m_old - m_new)
    probs = [jnp.exp(blk - m_new) for blk in blocks]
    psum = probs[0]
    for pb in probs[1:]:
        psum = psum + pb
    l_scr[...] = alpha * l_scr[...] + jnp.sum(psum, axis=-1, keepdims=True)
    p_all = jnp.concatenate([pb.astype(BF16) for pb in probs], axis=1)
    acc_scr[...] = alpha * acc_scr[...] + _dot_nn(p_all, v2)
    m_scr[...] = m_new

    @pl.when(j == pl.num_programs(1) - 1)
    def _():
        o_ref[...] = acc_scr[...] / l_scr[...]


def _fox_sample_call(page_table, qf, kf, vf, logf, cache_k, cache_v, cache_logf, layer, db, t_new, n_heads, dh,
                     pages_per_step):
    assert dh == LANES
    rows = t_new * n_heads
    n_pool, page = cache_k.shape[1:3]
    n_pages = page_table.shape[1]
    tiles = page * n_heads // LANES
    P = pages_per_step
    steps = n_pages // P
    pad_rows = lambda a: jnp.pad(a.reshape(db, rows, dh), ((0, 0), (0, LANES - rows), (0, 0))).astype(BF16)
    q2 = qf.reshape(db, rows, dh).astype(BF16)
    lfn = jnp.pad(logf.reshape(db, 1, rows), ((0, 0), (0, 7), (0, LANES - rows)))
    lf_pages = cache_logf.reshape(cache_logf.shape[0], n_pool, tiles, LANES)

    per_seq = lambda r: pl.BlockSpec((None, r, LANES), lambda b, j, pt: (b, 0, 0))
    hbm = pl.BlockSpec(memory_space=pl.ANY)
    out = pl.pallas_call(
        functools.partial(_fox_sample_body, n_heads=n_heads, pages_per_step=P, layer=layer, n_pages=n_pages),
        grid_spec=pltpu.PrefetchScalarGridSpec(
            num_scalar_prefetch=1,
            grid=(db, steps),
            in_specs=[per_seq(rows), per_seq(LANES), per_seq(LANES), per_seq(8), hbm, hbm, hbm],
            out_specs=per_seq(rows),
            scratch_shapes=[pltpu.VMEM((2, P, page, n_heads, dh), F32), pltpu.VMEM((2, P, page, n_heads, dh), F32),
                            pltpu.VMEM((2, P, tiles, LANES), F32), pltpu.SemaphoreType.DMA((2,)),
                            pltpu.VMEM((rows, 1), F32), pltpu.VMEM((rows, 1), F32),
                            pltpu.VMEM((rows, LANES), F32), pltpu.VMEM((1, LANES), F32)]),
        out_shape=jax.ShapeDtypeStruct((db, rows, LANES), F32),
        compiler_params=_cparams(("arbitrary", "arbitrary")),
        name="fox_sample",
    )(page_table, q2, pad_rows(kf), pad_rows(vf), lfn, cache_k, cache_v, lf_pages)
    return out.reshape(db * t_new, n_heads * dh)


def _rms_heads(x, n_heads, dh):
    outs = []
    for h in range(n_heads):
        xs = x[:, h * dh:(h + 1) * dh]
        outs.append(xs * lax.rsqrt(jnp.mean(xs * xs, axis=-1, keepdims=True) + EPS))
    return jnp.concatenate(outs, axis=1)


def _route(logits):
    lane = lax.broadcasted_iota(I32, logits.shape, 1)
    big = jnp.int32(LANES)
    gl = jnp.where(lane < N_GROUPS, logits, NEG_INF)
    gmax = jnp.max(gl, axis=-1, keepdims=True)
    g_sel = jnp.min(jnp.where(gl == gmax, lane, big), axis=-1, keepdims=True)
    p_g = 1.0 / jnp.sum(jnp.exp(gl - gmax), axis=-1, keepdims=True)
    lo = N_GROUPS + E_PER_GROUP * g_sel
    ev = jnp.where((lane >= lo) & (lane < lo + E_PER_GROUP), logits, NEG_INF)
    v1 = jnp.max(ev, axis=-1, keepdims=True)
    i1 = jnp.min(jnp.where(ev == v1, lane, big), axis=-1, keepdims=True)
    ev2 = jnp.where(lane == i1, NEG_INF, ev)
    v2 = jnp.max(ev2, axis=-1, keepdims=True)
    i2 = jnp.min(jnp.where(ev2 == v2, lane, big), axis=-1, keepdims=True)
    e21 = jnp.exp(v2 - v1)
    w1 = p_g / (1.0 + e21)
    w2 = p_g * e21 / (1.0 + e21)
    out = jnp.where(lane == 0, (i1 - N_GROUPS).astype(F32), 0.0)
    out = jnp.where(lane == 1, (i2 - N_GROUPS).astype(F32), out)
    out = jnp.where(lane == 2, w1, out)
    out = jnp.where(lane == 3, w2, out)
    return out


def _store_slabs(ref, x):
    tm, d = x.shape
    per = d // LANES
    for s in range(per):
        ref[pl.ds(s, tm, stride=per), :] = x[:, s * LANES:(s + 1) * LANES]


def _load_slabs(ref, tm, per):
    return jnp.concatenate([ref[pl.ds(s, tm, stride=per), :] for s in range(per)], axis=1)


def _merge_body(x_ref, og_ref, gg_ref, of_ref, gw_ref, fw_ref, wo_ref, g1_ref, sh2_ref, sc2_ref, n2_ref,
                wr_hi_ref, wr_lo_ref, br_ref, *rest, n_heads, dh, n_tiles):
    x1_ref, h2_ref, route_ref = rest[-3:]
    i = pl.program_id(0)

    @pl.when(i < n_tiles)
    def _():
        og = _rms_heads(og_ref[...], n_heads, dh) * gw_ref[...] * _silu(gg_ref[...])
        of = _rms_heads(of_ref[...], n_heads, dh) * fw_ref[...]
        merged = jnp.concatenate([og, of], axis=1).astype(BF16)
        x1 = x_ref[...] + g1_ref[...] * _dot_nn(merged, wo_ref[...])
        x1_ref[...] = x1
        y = x1 * lax.rsqrt(jnp.mean(x1 * x1, axis=-1, keepdims=True) + EPS) * n2_ref[...]
        h2 = y * (1.0 + sc2_ref[...]) + sh2_ref[...]
        _store_slabs(h2_ref, h2)
        hh, hl, _ = _split3(h2)
        logits = (_dot_nn(hh, wr_hi_ref[...]) + _dot_nn(hl, wr_hi_ref[...]) + _dot_nn(hh, wr_lo_ref[...])
                  + br_ref[...])
        route_ref[...] = _route(logits)

    @pl.when(i >= n_tiles)
    def _():
        h2_ref[...] = jnp.zeros(h2_ref.shape, F32)


def _merge(x, o_gla, gla_in, o_fox, gla_w, fox_w, w_out_bf, g1, sh2, sc2, norm2_w, wr_hi, wr_lo, b_r,
           tm, rows_per_mod, n_heads, dh, slab_rows, row0, into=None):
    n, d = x.shape
    half = n_heads * dh
    blk0 = row0 // tm
    per = d // LANES
    n_tiles = n // tm
    first = into is None
    extra = -(-(slab_rows - n) // tm) if first else 0
    assert extra <= 1
    aliased = [] if first else [into]
    n_in = 14
    last = n_tiles - 1
    per_token = rows_per_mod == 1
    if per_token:
        mod_spec = pl.BlockSpec((tm, d), lambda i: (jnp.minimum(i, last), 0))
    else:
        tiles_per_mod = rows_per_mod // tm
        g1, sh2, sc2 = (a.reshape(a.shape[0], 1, d) for a in (g1, sh2, sc2))
        mod_spec = pl.BlockSpec((None, 1, d), lambda i: (jnp.minimum(i, last) // tiles_per_mod, 0, 0))
    row = lambda i: (jnp.minimum(i, last), 0)
    fixed = lambda i: (0, 0)
    return pl.pallas_call(
        functools.partial(_merge_body, n_heads=n_heads, dh=dh, n_tiles=n_tiles),
        grid=(n_tiles + extra,),
        in_specs=[pl.BlockSpec((tm, d), row),
                  pl.BlockSpec((tm, half), row),
                  pl.BlockSpec((tm, half), lambda i: (jnp.minimum(i, last), 2)),
                  pl.BlockSpec((tm, half), row),
                  pl.BlockSpec((1, half), fixed), pl.BlockSpec((1, half), fixed),
                  pl.BlockSpec((d, d), fixed),
                  mod_spec, mod_spec, mod_spec,
                  pl.BlockSpec((1, d), fixed),
                  pl.BlockSpec((d, LANES), fixed), pl.BlockSpec((d, LANES), fixed),
                  pl.BlockSpec((1, LANES), fixed)] + [pl.BlockSpec(memory_space=pl.ANY)] * len(aliased),
        out_specs=(pl.BlockSpec((tm, d), row), pl.BlockSpec((tm * per, LANES), lambda i: (blk0 + i, 0)),
                   pl.BlockSpec((tm, LANES), row)),
        out_shape=(jax.ShapeDtypeStruct((n, d), F32), jax.ShapeDtypeStruct((slab_rows * per, LANES), F32),
                   jax.ShapeDtypeStruct((n, LANES), F32)),
        input_output_aliases={n_in: 1} if aliased else {},
        compiler_params=_cparams(("arbitrary",)),
        name="merge_outproj",
    )(x, o_gla, gla_in, o_fox, gla_w.reshape(1, half), fox_w.reshape(1, half), w_out_bf, g1, sh2, sc2,
      norm2_w.reshape(1, d), wr_hi, wr_lo, b_r, *aliased)


GATHER_UNROLL = 8


def _experts_body(te_ref, first_ref, tok_ref, nt_ref, grp_ref, nxt_ref, h_hbm, wg_hbm, wu_hbm, wd_hbm, y_ref,
                  xbuf, wg_f, wu_f, wd_f, wg_bf, wu_bf, wd_bf, sem, wsem, *, tm, per):
    i = pl.program_id(0)
    n_active = nt_ref[0]

    def weight_copies(expert, wslot):
        return (pltpu.make_async_copy(wg_hbm.at[expert], wg_f.at[wslot], wsem.at[wslot]),
                pltpu.make_async_copy(wu_hbm.at[expert], wu_f.at[wslot], wsem.at[wslot]),
                pltpu.make_async_copy(wd_hbm.at[expert], wd_f.at[wslot], wsem.at[wslot]))

    def gather(tile, slot):
        first = first_ref[tile]

        def issue(g, carry):
            for u in range(GATHER_UNROLL):
                r = g * GATHER_UNROLL + u
                src = pl.multiple_of(tok_ref[first + r] * per, per)
                pltpu.make_async_copy(h_hbm.at[pl.ds(src, per), :], xbuf.at[slot, pl.ds(r * per, per), :],
                                      sem.at[slot]).start()
            return carry
        lax.fori_loop(0, tm // GATHER_UNROLL, issue, 0)

    @pl.when((i == 0) & (n_active > 0))
    def _():
        gather(0, 0)
        for cp in weight_copies(te_ref[0], 0):
            cp.start()

    @pl.when(i + 1 < n_active)
    def _():
        gather(i + 1, (i + 1) % 2)

    @pl.when(i < n_active)
    def _():
        slot = i % 2
        prev = te_ref[jnp.maximum(i - 1, 0)]

        @pl.when((i == 0) | (te_ref[i] != prev))
        def _():
            wslot = grp_ref[i] % 2

            @pl.when(nxt_ref[i] >= 0)
            def _():
                for cp in weight_copies(nxt_ref[i], 1 - wslot):
                    cp.start()

            for cp in weight_copies(te_ref[i], wslot):
                cp.wait()
            wg_bf[...] = wg_f[wslot].astype(BF16)
            wu_bf[...] = wu_f[wslot].astype(BF16)
            wd_bf[...] = wd_f[wslot].astype(BF16)

        pltpu.make_async_copy(h_hbm.at[pl.ds(0, tm * per), :], xbuf.at[slot], sem.at[slot]).wait()
        x = _load_slabs(xbuf.at[slot], tm, per).astype(BF16)
        a = (_silu(_dot_nn(x, wg_bf[...])) * _dot_nn(x, wu_bf[...])).astype(BF16)
        _store_slabs(y_ref, _dot_nn(a, wd_bf[...]))

    @pl.when(i >= n_active)
    def _():
        y_ref[...] = jnp.zeros(y_ref.shape, F32)


def _experts(tile_expert, tile_first, sorted_tok, n_tiles, tile_group, next_expert, h2_slabs,
             w_gate, w_up, w_down, tm):
    max_tiles = tile_expert.shape[0]
    _, d, dff = w_gate.shape
    per = d // LANES
    hbm = pl.BlockSpec(memory_space=pl.ANY)
    return pl.pallas_call(
        functools.partial(_experts_body, tm=tm, per=per),
        grid_spec=pltpu.PrefetchScalarGridSpec(
            num_scalar_prefetch=6,
            grid=(max_tiles,),
            in_specs=[hbm, hbm, hbm, hbm],
            out_specs=pl.BlockSpec((tm * per, LANES), lambda i, *prefetch: (i, 0)),
            scratch_shapes=[pltpu.VMEM((2, tm * per, LANES), F32),
                            pltpu.VMEM((2, d, dff), F32), pltpu.VMEM((2, d, dff), F32), pltpu.VMEM((2, dff, d), F32),
                            pltpu.VMEM((d, dff), BF16), pltpu.VMEM((d, dff), BF16), pltpu.VMEM((dff, d), BF16),
                            pltpu.SemaphoreType.DMA((2,)), pltpu.SemaphoreType.DMA((2,))]),
        out_shape=jax.ShapeDtypeStruct((max_tiles * tm * per, LANES), F32),
        compiler_params=_cparams(("arbitrary",)),
        name="experts",
    )(tile_expert, tile_first, sorted_tok, n_tiles, tile_group, next_expert, h2_slabs, w_gate, w_up, w_down)


def _combine_body(pos_ref, y_hbm, x1_ref, route_ref, g2_ref, fw_ref, o_ref, buf, sem, *, tm, row0, per):
    i = pl.program_id(0)

    def gather(tile, slot):
        def issue(g, carry):
            for u in range(GATHER_UNROLL // 2):
                r = g * (GATHER_UNROLL // 2) + u
                tok = row0 + tile * tm + r
                for c in range(2):
                    src = pl.multiple_of(pos_ref[2 * tok + c] * per, per)
                    pltpu.make_async_copy(y_hbm.at[pl.ds(src, per), :],
                                          buf.at[slot, c, pl.ds(r * per, per), :], sem.at[slot]).start(priority=1)
            return carry
        lax.fori_loop(0, tm // (GATHER_UNROLL // 2), issue, 0)

    @pl.when(i == 0)
    def _():
        gather(0, 0)

    @pl.when(i + 1 < pl.num_programs(0))
    def _():
        gather(i + 1, (i + 1) % 2)

    slot = i % 2
    for c in range(2):
        pltpu.make_async_copy(y_hbm.at[pl.ds(0, tm * per), :], buf.at[slot, c], sem.at[slot]).wait()
    route = route_ref[...]
    moe = (route[:, 2:3] * _load_slabs(buf.at[slot, 0], tm, per)
           + route[:, 3:4] * _load_slabs(buf.at[slot, 1], tm, per))
    x2 = x1_ref[...] + g2_ref[...] * moe
    o_ref[...] = x2 * lax.rsqrt(jnp.mean(x2 * x2, axis=-1, keepdims=True) + EPS) * fw_ref[...]


def _combine(pos, y_slabs, x1, route, g2, final_w, row0, n_rows, tm, rows_per_mod):
    d = x1.shape[1]
    per = d // LANES
    per_token = rows_per_mod == 1
    if per_token:
        mod_spec = pl.BlockSpec((tm, d), lambda i, pos: (i, 0))
    else:
        tiles_per_mod = rows_per_mod // tm
        g2 = g2.reshape(g2.shape[0], 1, d)
        mod_spec = pl.BlockSpec((None, 1, d), lambda i, pos: (i // tiles_per_mod, 0, 0))
    return pl.pallas_call(
        functools.partial(_combine_body, tm=tm, row0=row0, per=per),
        grid_spec=pltpu.PrefetchScalarGridSpec(
            num_scalar_prefetch=1,
            grid=(n_rows // tm,),
            in_specs=[pl.BlockSpec(memory_space=pl.ANY),
                      pl.BlockSpec((tm, d), lambda i, pos: (i, 0)),
                      pl.BlockSpec((tm, LANES), lambda i, pos: (i, 0)),
                      mod_spec,
                      pl.BlockSpec((1, d), lambda i, pos: (0, 0))],
            out_specs=pl.BlockSpec((tm, d), lambda i, pos: (i, 0)),
            scratch_shapes=[pltpu.VMEM((2, 2, tm * per, LANES), F32), pltpu.SemaphoreType.DMA((2,))]),
        out_shape=jax.ShapeDtypeStruct((n_rows, d), F32),
        compiler_params=_cparams(("arbitrary",)),
        name="combine_norm",
    )(pos, y_slabs, x1, route, g2, final_w.reshape(1, d))


def _invert_body(where_ref, out_ref, *, n_pairs, n_out):
    def fill(g, carry):
        for u in range(GATHER_UNROLL):
            a = g * GATHER_UNROLL + u
            out_ref[where_ref[a]] = lax.shift_right_logical(a, 1)
        return carry
    lax.fori_loop(0, n_pairs // GATHER_UNROLL, fill, 0)

    def tail(t, carry):
        out_ref[n_pairs + t] = 0
        return carry
    lax.fori_loop(0, n_out - n_pairs, tail, 0)


def _sorted_tokens(where, n_out):
    n_pairs = where.shape[0]
    assert n_pairs % GATHER_UNROLL == 0
    smem = pl.BlockSpec(memory_space=pltpu.SMEM)
    return pl.pallas_call(
        functools.partial(_invert_body, n_pairs=n_pairs, n_out=n_out),
        in_specs=[smem], out_specs=smem,
        out_shape=jax.ShapeDtypeStruct((n_out,), I32),
        name="sorted_tokens",
    )(where)


def _plan(route, n_experts, tm):
    n = route.shape[0]
    e_flat = route[:, 0:2].astype(I32).reshape(-1)
    onehot = (e_flat[:, None] == jnp.arange(n_experts, dtype=I32)[None, :]).astype(I32)
    csum = jnp.cumsum(onehot, axis=0)
    counts = csum[-1]
    rank = jnp.sum(onehot * csum, axis=1) - 1
    tiles_e = (counts + tm - 1) // tm
    tile_end = jnp.cumsum(tiles_e)
    tile_off = tile_end - tiles_e
    cnt_off = jnp.cumsum(counts) - counts
    max_tiles = (2 * n) // tm + n_experts
    pos = jnp.sum(onehot * (tile_off * tm)[None, :], axis=1) + rank
    where = jnp.sum(onehot * cnt_off[None, :], axis=1) + rank
    tile_ids = jnp.arange(max_tiles, dtype=I32)
    live_ids = jnp.minimum(tile_ids, tile_end[-1] - 1)
    tile_expert = jnp.minimum(jnp.sum((tile_end[None, :] <= live_ids[:, None]).astype(I32), axis=1), n_experts - 1)
    is_e = (tile_expert[:, None] == jnp.arange(n_experts, dtype=I32)[None, :]).astype(I32)
    tile_first = jnp.sum(is_e * (cnt_off - tile_off * tm)[None, :], axis=1) + tile_ids * tm
    tile_first = jnp.clip(tile_first, 0, 2 * n)
    e_ids = jnp.arange(n_experts, dtype=I32)
    used = tiles_e > 0
    run_of_e = jnp.cumsum(used.astype(I32)) - 1
    later_used = used[None, :] & (e_ids[None, :] > e_ids[:, None])
    next_of_e = jnp.min(jnp.where(later_used, e_ids[None, :], n_experts), axis=1)
    next_of_e = jnp.where(next_of_e < n_experts, next_of_e, -1)
    tile_group = jnp.sum(is_e * run_of_e[None, :], axis=1)
    next_expert = jnp.sum(is_e * next_of_e[None, :], axis=1)
    return (tile_expert, tile_first.astype(I32), tile_end[-1].reshape(1).astype(I32), tile_group.astype(I32),
            next_expert.astype(I32), pos.astype(I32), where.astype(I32))


def kernel(x_prompt, x_sample, cache_k, cache_v, cache_logf, state_gla, page_table, c_prompt, c_sample,
           norm1_w, norm2_w, w_ada, b_ada, w_in, w_gk2, b_gk, b_fgate, gla_onorm_w, fox_onorm_w, w_out,
           w_rg, b_rg, w_re, b_re, w_gate_e, w_up_e, w_down_e, final_norm_w):
    depth = w_in.shape[0]
    assert depth == 1, "single-layer trunk"
    batch, seq, d = x_prompt.shape
    db, t_new, _ = x_sample.shape
    n_hg, dk, dv = state_gla.shape[2:]
    n_hf, dh = cache_k.shape[3:]
    rank = w_gk2.shape[1]
    n_experts = w_gate_e.shape[1]
    wq = n_hg * dk
    wv = n_hg * dv
    wf = n_hf * dh
    assert wv == 1024 and wf == 1024 and 2 * wq == 1024 and rank == 16 and n_hf == 8
    n_p = batch * seq
    n_s = db * t_new
    layer = 0

    wt = w_in[layer].T
    o_alr = 2 * wq + 2 * wv
    o_fox = o_alr + rank
    o_fl = o_fox + 3 * wf
    assert o_fl - o_fox == o_alr
    w_main = _pack_w_in(wt, o_alr, o_fox)
    w_small = jnp.concatenate([wt[o_alr:o_fox], wt[o_fl:], jnp.zeros((LANES - rank - n_hf, d), F32)], axis=0)
    b_small = jnp.zeros((1, LANES), F32).at[0, rank:rank + n_hf].set(b_fgate[layer])
    w_gk2_pad = jnp.zeros((LANES, wq), F32).at[:rank].set(w_gk2[layer])
    b_gk2 = b_gk[layer].reshape(1, wq)
    w_out_bf = w_out[layer].astype(BF16)
    w_r = jnp.concatenate([w_rg[layer], w_re[layer],
                           jnp.zeros((d, LANES - N_GROUPS - n_experts), F32)], axis=1)
    wr_hi = w_r.astype(BF16)
    wr_lo = (w_r - wr_hi.astype(F32)).astype(BF16)
    b_r = jnp.concatenate([b_rg[layer], b_re[layer], jnp.zeros((LANES - N_GROUPS - n_experts,), F32)]).reshape(1, LANES)

    n_c = batch + db
    c_all = jnp.concatenate([c_prompt, c_sample, jnp.zeros((-n_c % 8, d), F32)], axis=0)
    mod = _adaln(c_all, w_ada[layer], b_ada[layer])
    sh1, sc1, g1, sh2, sc2, g2 = (mod[:, i * d:(i + 1) * d] for i in range(6))
    p_rows = slice(0, batch)
    rep = lambda a: jnp.repeat(a[batch:n_c], t_new, axis=0)

    q_scale = dh ** -0.5
    xp = x_prompt.reshape(n_p, d)
    xs = x_sample.reshape(n_s, d)

    tm_p = 512
    gla_p, qf_p, kf_p, vf_p, kfb_p, vfb_p, small_p = _inproj(
        xp, sh1[p_rows], sc1[p_rows], norm1_w[layer], w_main, w_small, b_small, tm_p, seq, q_scale * LOG2E)
    s0_p = jnp.zeros((batch, n_hg, dk, dv), F32)
    o_gla_p, gla_state_p = _gla(gla_p, small_p, w_gk2_pad, b_gk2, s0_p, batch, seq, 128, n_hg, dk, dv)
    logf_p = small_p[:, rank:rank + n_hf]
    key_bias = _key_bias(small_p, batch, seq, rank, n_hf)
    o_fox_p = _fox_prompt(qf_p, kfb_p, vfb_p, key_bias, batch, seq, n_hf, dh, 512)
    n_all = n_p + n_s
    x1_p, h2_p, route_p = _merge(xp, o_gla_p, gla_p, o_fox_p, gla_onorm_w[layer], fox_onorm_w[layer], w_out_bf,
                                 g1[p_rows], sh2[p_rows], sc2[p_rows], norm2_w[layer], wr_hi, wr_lo, b_r,
                                 256, seq, n_hg, dv, n_all, 0)

    gla_s, qf_s, kf_s, vf_s, _, _, small_s = _inproj(
        xs, rep(sh1), rep(sc1), norm1_w[layer], w_main, w_small, b_small, n_s, 1, q_scale)
    t_pad = GLA_CHUNK
    pad_t = lambda a: jnp.pad(a.reshape(db, t_new, -1), ((0, 0), (0, t_pad - t_new), (0, 0))).reshape(db * t_pad, -1)
    o_gla_s_pad, gla_state_s = _gla(pad_t(gla_s), pad_t(small_s), w_gk2_pad, b_gk2, state_gla[layer],
                                    db, t_pad, t_pad, n_hg, dk, dv, t_valid=t_new)
    o_gla_s = o_gla_s_pad.reshape(db, t_pad, wv)[:, :t_new].reshape(n_s, wv)

    logf_s = small_s[:, rank:rank + n_hf]
    o_fox_s = _fox_sample_call(page_table, qf_s, kf_s, vf_s, logf_s, cache_k, cache_v, cache_logf,
                               layer, db, t_new, n_hf, dh, 8)
    x1_s, h2, route_s = _merge(xs, o_gla_s, gla_s, o_fox_s, gla_onorm_w[layer], fox_onorm_w[layer], w_out_bf,
                               rep(g1), rep(sh2), rep(sc2), norm2_w[layer], wr_hi, wr_lo, b_r,
                               n_s, 1, n_hg, dv, n_all, n_p, into=h2_p)

    tm_e = 256
    route = jnp.concatenate([route_p, route_s], axis=0)
    tile_expert, tile_first, n_tiles, tile_group, next_expert, pos, where = _plan(route, n_experts, tm_e)
    sorted_tok = _sorted_tokens(where, where.shape[0] + tm_e)
    y_slabs = _experts(tile_expert, tile_first, sorted_tok, n_tiles, tile_group, next_expert, h2,
                       w_gate_e[layer], w_up_e[layer], w_down_e[layer], tm_e)
    y_p = _combine(pos, y_slabs, x1_p, route_p, g2[p_rows], final_norm_w, 0, n_p, 256, seq)
    y_s = _combine(pos, y_slabs, x1_s, route_s, rep(g2), final_norm_w, n_p, n_s, n_s, 1)

    y_prompt = y_p.reshape(batch, seq, d)
    y_sample = y_s.reshape(db, t_new, d)
    k_prompt = kf_p.reshape(1, batch, seq, n_hf, dh)
    v_prompt = vf_p.reshape(1, batch, seq, n_hf, dh)
    logf_prompt = logf_p.reshape(1, batch, seq, n_hf)
    k_sample = kf_s.reshape(1, db, t_new, n_hf, dh)
    v_sample = vf_s.reshape(1, db, t_new, n_hf, dh)
    logf_sample = logf_s.reshape(1, db, t_new, n_hf)
    return (y_prompt, y_sample, k_prompt, v_prompt, logf_prompt, gla_state_p[None],
            k_sample, v_sample, logf_sample, gla_state_s[None])
```

```python
import functools

import jax
import jax.numpy as jnp
from jax import lax
from jax.experimental import pallas as pl
from jax.experimental.pallas import tpu as pltpu

F32 = jnp.float32
BF16 = jnp.bfloat16
I32 = jnp.int32
EPS = 1e-6
NEG_INF = float("-inf")

LANES = 128
GLA_CHUNK = 16
GLA_GATE_NORM = 16.0
N_GROUPS = 4
E_PER_GROUP = 8
VMEM_LIMIT = 56 * 1024 * 1024


def _cparams(sem):
    return pltpu.CompilerParams(dimension_semantics=sem, vmem_limit_bytes=VMEM_LIMIT)


def _log_sigmoid(z):
    return jnp.minimum(z, 0.0) - jnp.log1p(jnp.exp(-jnp.abs(z)))


def _silu(z):
    return z * jax.nn.sigmoid(z)


def _split3(a):
    hi = a.astype(BF16)
    r = a - hi.astype(F32)
    mid = r.astype(BF16)
    lo = (r - mid.astype(F32)).astype(BF16)
    return hi, mid, lo


def _dot_nn(a, b):
    return jnp.dot(a, b, preferred_element_type=F32)


def _dot_nt(a, b):
    return lax.dot_general(a, b, (((1,), (1,)), ((), ())), preferred_element_type=F32)


def _dot_f32(a, b):
    ah, al, _ = _split3(a)
    bh, bl, _ = _split3(b)
    return _dot_nn(ah, bh) + _dot_nn(al, bh) + _dot_nn(ah, bl)


def _dot_exact_rhs01(a, ones_bf16):
    hi, mid, lo = _split3(a)
    return _dot_nn(hi, ones_bf16) + _dot_nn(mid, ones_bf16) + _dot_nn(lo, ones_bf16)


def _dot_exact_lhs01(ones_bf16, b):
    hi, mid, lo = _split3(b)
    return _dot_nn(ones_bf16, hi) + _dot_nn(ones_bf16, mid) + _dot_nn(ones_bf16, lo)


def _adaln_body(c_ref, w_ref, b_ref, o_ref):
    s = _silu(c_ref[...]).astype(BF16)
    o_ref[...] = _dot_nn(s, w_ref[...].astype(BF16)) + b_ref[...]


def _adaln(c_all, w_ada, b_ada, tn=1024):
    rows, d = c_all.shape
    n6 = w_ada.shape[1]
    return pl.pallas_call(
        _adaln_body,
        grid=(n6 // tn,),
        in_specs=[pl.BlockSpec((rows, d), lambda j: (0, 0)),
                  pl.BlockSpec((d, tn), lambda j: (0, j)),
                  pl.BlockSpec((1, tn), lambda j: (0, j))],
        out_specs=pl.BlockSpec((rows, tn), lambda j: (0, j)),
        out_shape=jax.ShapeDtypeStruct((rows, n6), F32),
        compiler_params=_cparams(("arbitrary",)),
        name="adaln",
    )(c_all, w_ada, b_ada.reshape(1, n6))


def _pack_w_in_body(a_ref, b_ref, c_ref, o_ref, *, shift, n_first):
    j = pl.program_id(0)

    @pl.when(j < n_first)
    def _():
        o_ref[...] = a_ref[...].astype(BF16)

    @pl.when(j >= n_first)
    def _():
        both = jnp.concatenate([b_ref[...], c_ref[...]], axis=0)
        o_ref[...] = both[shift:shift + o_ref.shape[0], :].astype(BF16)


def _pack_w_in(wt, rows, second_start):
    d = wt.shape[1]
    tr = 512
    shift = second_start % tr
    base = second_start - shift
    assert rows % tr == 0 and shift % 16 == 0 and 0 < shift and tr % shift == 0
    n_first = rows // tr
    second = lambda j: jnp.maximum(j - n_first, 0)
    return pl.pallas_call(
        functools.partial(_pack_w_in_body, shift=shift, n_first=n_first),
        grid=(2 * n_first,),
        in_specs=[pl.BlockSpec((tr, d), lambda j: (jnp.minimum(j, n_first - 1), 0)),
                  pl.BlockSpec((tr, d), lambda j: (base // tr + second(j), 0)),
                  pl.BlockSpec((shift, d), lambda j: ((base + tr) // shift + second(j) * (tr // shift), 0))],
        out_specs=pl.BlockSpec((tr, d), lambda j: (j, 0)),
        out_shape=jax.ShapeDtypeStruct((2 * rows, d), BF16),
        compiler_params=_cparams(("arbitrary",)),
        name="pack_w_in",
    )(wt, wt, wt)


def _inproj_body(x_ref, sh_ref, sc_ref, nw_ref, w_ref, ws_ref, bs_ref,
                 gla_ref, qf_ref, kf_ref, vf_ref, kfb_ref, vfb_ref, small_ref, h_scr, *, q_scale):
    j = pl.program_id(1)

    @pl.when(j == 0)
    def _():
        x = x_ref[...]
        y = x * lax.rsqrt(jnp.mean(x * x, axis=-1, keepdims=True) + EPS) * nw_ref[...]
        hb = (y * (1.0 + sc_ref[...]) + sh_ref[...]).astype(BF16)
        h_scr[...] = hb
        sm = _dot_nt(hb, ws_ref[...].astype(BF16))
        lane = lax.broadcasted_iota(I32, sm.shape, 1)
        small_ref[...] = jnp.where((lane >= 16) & (lane < 24), _log_sigmoid(sm + bs_ref[...]), sm)

    acc = _dot_nt(h_scr[...], w_ref[...])

    @pl.when(j < 3)
    def _():
        gla_ref[...] = acc

    @pl.when(j == 3)
    def _():
        qf_ref[...] = (acc * q_scale).astype(BF16)

    @pl.when(j == 4)
    def _():
        kf_ref[...] = acc
        kfb_ref[...] = acc.astype(BF16)

    @pl.when(j == 5)
    def _():
        vf_ref[...] = acc
        vfb_ref[...] = acc.astype(BF16)


def _inproj(x, sh, sc, norm_w, w_main, w_small, b_small, tm, rows_per_mod, q_scale):
    n, d = x.shape
    tn = w_main.shape[0] // 6
    per_token = rows_per_mod == 1
    if per_token:
        mod_spec = pl.BlockSpec((tm, d), lambda i, j: (i, 0))
    else:
        tiles_per_mod = rows_per_mod // tm
        sh = sh.reshape(sh.shape[0], 1, d)
        sc = sc.reshape(sc.shape[0], 1, d)
        mod_spec = pl.BlockSpec((None, 1, d), lambda i, j: (i // tiles_per_mod, 0, 0))
    row = lambda i, j: (i, 0)
    out_shapes = (jax.ShapeDtypeStruct((n, 3 * tn), F32),
                  jax.ShapeDtypeStruct((n, tn), BF16),
                  jax.ShapeDtypeStruct((n, tn), F32),
                  jax.ShapeDtypeStruct((n, tn), F32),
                  jax.ShapeDtypeStruct((n, tn), BF16),
                  jax.ShapeDtypeStruct((n, tn), BF16),
                  jax.ShapeDtypeStruct((n, LANES), F32))
    return pl.pallas_call(
        functools.partial(_inproj_body, q_scale=q_scale),
        grid=(n // tm, 6),
        in_specs=[pl.BlockSpec((tm, d), row), mod_spec, mod_spec,
                  pl.BlockSpec((1, d), lambda i, j: (0, 0)),
                  pl.BlockSpec((tn, d), lambda i, j: (j, 0)),
                  pl.BlockSpec((LANES, d), lambda i, j: (0, 0)),
                  pl.BlockSpec((1, LANES), lambda i, j: (0, 0))],
        out_specs=(pl.BlockSpec((tm, tn), lambda i, j: (i, jnp.minimum(j, 2))),
                   pl.BlockSpec((tm, tn), row), pl.BlockSpec((tm, tn), row), pl.BlockSpec((tm, tn), row),
                   pl.BlockSpec((tm, tn), row), pl.BlockSpec((tm, tn), row),
                   pl.BlockSpec((tm, LANES), row)),
        out_shape=out_shapes,
        scratch_shapes=[pltpu.VMEM((tm, d), BF16)],
        compiler_params=_cparams(("arbitrary", "arbitrary")),
        name="inproj",
    )(x, sh, sc, norm_w.reshape(1, d), w_main, w_small, b_small)


def _level_ref(b, half, rows_i):
    tt, n = b.shape
    if half >= 4:
        size = 2 * half
        blocks = [jnp.broadcast_to(b[g * size + half - 1:g * size + half, :], (size, n)) for g in range(tt // size)]
        return blocks[0] if len(blocks) == 1 else jnp.concatenate(blocks, axis=0)
    if half == 2:
        lo = jnp.concatenate([jnp.broadcast_to(b[8 * g + 1:8 * g + 2, :], (8, n)) for g in range(tt // 8)], axis=0)
        hi = jnp.concatenate([jnp.broadcast_to(b[8 * g + 5:8 * g + 6, :], (8, n)) for g in range(tt // 8)], axis=0)
        return jnp.where(rows_i % 8 < 4, lo, hi)
    return jnp.where(rows_i % 2 == 1, pltpu.roll(b, 1, 0), b)


def _gla_body(q_ref, k_ref, v_ref, sm_ref, w2_ref, b2_ref, s0_ref, o_ref, sout_ref, st_scr,
              *, tt, dk, dv, n_pairs, t_valid):
    t = pl.program_id(1)
    zero_blk = jnp.zeros((dv, dk), F32)

    @pl.when(t == 0)
    def _():
        for p in range(n_pairs):
            top = jnp.concatenate([s0_ref[2 * p].T, zero_blk], axis=1)
            bot = jnp.concatenate([zero_blk, s0_ref[2 * p + 1].T], axis=1)
            st_scr[p] = jnp.concatenate([top, bot], axis=0)

    gk = _log_sigmoid(_dot_f32(sm_ref[...], w2_ref[...]) + b2_ref[...]) / GLA_GATE_NORM
    rows_i = lax.broadcasted_iota(I32, (tt, 1), 0)
    if t_valid < tt:
        gk = jnp.where(rows_i < t_valid, gk, 0.0)
    r_i = lax.broadcasted_iota(I32, (tt, tt), 0)
    c_i = lax.broadcasted_iota(I32, (tt, tt), 1)
    b_all = _dot_exact_lhs01((c_i <= r_i).astype(BF16), gk)

    halves = []
    half = tt // 2
    while half >= 1:
        halves.append(half)
        half //= 2
    refs = [_level_ref(b_all, hf, rows_i) for hf in halves]
    valid = [(r_i // (2 * hf) == c_i // (2 * hf)) & (r_i % (2 * hf) >= hf) & (c_i % (2 * hf) < hf) for hf in halves]
    on_diag = r_i == c_i
    head0 = lax.broadcasted_iota(I32, (tt, 2 * dk), 1) < dk
    rr = lax.broadcasted_iota(I32, (2 * dv, 2 * dk), 0) // dv
    cc = lax.broadcasted_iota(I32, (2 * dv, 2 * dk), 1) // dk
    diag = (rr == cc).astype(F32)

    def head_scores(qx, kx):
        stacked = jnp.concatenate([jnp.where(head0, qx, 0.0), jnp.where(head0, 0.0, qx)], axis=0).astype(BF16)
        return _dot_nt(stacked, kx.astype(BF16))

    for p in range(n_pairs):
        cs = slice(p * 2 * dk, (p + 1) * 2 * dk)
        vs = slice(p * 2 * dv, (p + 1) * 2 * dv)
        q = q_ref[:, cs] * (dk ** -0.5)
        k = k_ref[:, cs]
        v = v_ref[:, vs]
        b = b_all[:, cs]
        sc = head_scores(q, k)
        a0 = jnp.where(on_diag, sc[:tt], 0.0)
        a1 = jnp.where(on_diag, sc[tt:], 0.0)
        for ref_all, ok in zip(refs, valid):
            ref = ref_all[:, cs]
            sc = head_scores(q * jnp.exp(jnp.minimum(b - ref, 0.0)), k * jnp.exp(jnp.minimum(ref - b, 0.0)))
            a0 = jnp.where(ok, sc[:tt], a0)
            a1 = jnp.where(ok, sc[tt:], a1)
        vb = v.astype(BF16)
        o_intra = jnp.concatenate([_dot_nn(a0.astype(BF16), vb[:, :dv]), _dot_nn(a1.astype(BF16), vb[:, dv:])], axis=1)

        st = st_scr[p]
        bend = b[tt - 1:tt]
        o_inter = _dot_nt((q * jnp.exp(b)).astype(BF16), st.astype(BF16))
        kdec = (k * jnp.exp(bend - b)).astype(BF16)
        st = st * jnp.exp(bend) + _dot_nn(v.T.astype(BF16), kdec) * diag
        st_scr[p] = st
        o_ref[:, vs] = o_intra + o_inter

    @pl.when(t == pl.num_programs(1) - 1)
    def _():
        for p in range(n_pairs):
            st = st_scr[p]
            sout_ref[2 * p] = st[:dv, :dk].T
            sout_ref[2 * p + 1] = st[dv:, dk:].T


def _gla(gla_in, small, w_gk2_pad, b_gk, s0, batch, seq, tt, n_heads, dk, dv, t_valid=None):
    n = batch * seq
    nt = seq // tt
    wq = n_heads * dk
    wv = n_heads * dv
    assert wv == 2 * wq
    row = lambda b, t: b * nt + t
    return pl.pallas_call(
        functools.partial(_gla_body, tt=tt, dk=dk, dv=dv, n_pairs=n_heads // 2,
                          t_valid=tt if t_valid is None else t_valid),
        grid=(batch, nt),
        in_specs=[pl.BlockSpec((tt, wq), lambda b, t: (row(b, t), 0)),
                  pl.BlockSpec((tt, wq), lambda b, t: (row(b, t), 1)),
                  pl.BlockSpec((tt, wv), lambda b, t: (row(b, t), 1)),
                  pl.BlockSpec((tt, LANES), lambda b, t: (row(b, t), 0)),
                  pl.BlockSpec((LANES, wq), lambda b, t: (0, 0)),
                  pl.BlockSpec((1, wq), lambda b, t: (0, 0)),
                  pl.BlockSpec((None, n_heads, dk, dv), lambda b, t: (b, 0, 0, 0))],
        out_specs=(pl.BlockSpec((tt, wv), lambda b, t: (row(b, t), 0)),
                   pl.BlockSpec((None, n_heads, dk, dv), lambda b, t: (b, 0, 0, 0))),
        out_shape=(jax.ShapeDtypeStruct((n, wv), F32),
                   jax.ShapeDtypeStruct((batch, n_heads, dk, dv), F32)),
        scratch_shapes=[pltpu.VMEM((n_heads // 2, 2 * dv, 2 * dk), F32)],
        compiler_params=_cparams(("arbitrary", "arbitrary")),
        name="gla",
    )(gla_in, gla_in, gla_in, small, w_gk2_pad, b_gk, s0)


LOG2E = 1.4426950408889634
BIAS_PARTS = 3


def _key_bias_body(sm_ref, o_ref, *, lane0, n_heads, blk):
    s = sm_ref.shape[0]
    r_i = lax.broadcasted_iota(I32, (blk, blk), 0)
    c_i = lax.broadcasted_iota(I32, (blk, blk), 1)
    lower = (c_i <= r_i).astype(BF16)
    head = r_i - lane0
    sel = [((head >= 0) & (head < n_heads) & (c_i == BIAS_PARTS * head + part)).astype(BF16)
           for part in range(BIAS_PARTS)]
    carry = jnp.zeros((1, sm_ref.shape[1]), F32)
    for i in range(s // blk):
        f = _dot_exact_lhs01(lower, sm_ref[i * blk:(i + 1) * blk, :]) + carry
        carry = f[blk - 1:blk, :]
        pieces = _split3(f * (-LOG2E))
        placed = sum(_dot_nn(piece, sel_p) for piece, sel_p in zip(pieces, sel))
        o_ref[i * blk:(i + 1) * blk, :] = placed.astype(BF16)


def _key_bias(small, batch, seq, lane0, n_heads):
    return pl.pallas_call(
        functools.partial(_key_bias_body, lane0=lane0, n_heads=n_heads, blk=LANES),
        grid=(batch,),
        in_specs=[pl.BlockSpec((seq, LANES), lambda i: (i, 0))],
        out_specs=pl.BlockSpec((seq, LANES), lambda i: (i, 0)),
        out_shape=jax.ShapeDtypeStruct((batch * seq, LANES), BF16),
        compiler_params=_cparams(("arbitrary",)),
        name="fox_key_bias",
    )(small)


def _fox_prompt_body(q_ref, k_ref, v_ref, kb_ref, o_ref, m_scr, l_scr, acc_scr, *, n_heads, dh):
    qi = pl.program_id(1)
    ki = pl.program_id(2)
    tq = q_ref.shape[0]
    tk = k_ref.shape[0]

    @pl.when(ki == 0)
    def _():
        m_scr[...] = jnp.full(m_scr.shape, NEG_INF, F32)
        l_scr[...] = jnp.zeros(l_scr.shape, F32)
        acc_scr[...] = jnp.zeros(acc_scr.shape, F32)

    def step(masked):
        lane = lax.broadcasted_iota(I32, (tq, LANES), 1)
        kb = kb_ref[...]
        if masked:
            keep = lax.broadcasted_iota(I32, (tk, tq), 0) <= lax.broadcasted_iota(I32, (tk, tq), 1)
        for h in range(n_heads):
            hs = slice(h * dh, (h + 1) * dh)
            ones = ((lane >= BIAS_PARTS * h) & (lane < BIAS_PARTS * (h + 1))).astype(BF16)
            q_aug = jnp.concatenate([q_ref[:, hs], ones], axis=1)
            k_aug = jnp.concatenate([k_ref[:, hs], kb], axis=1)
            s_t = _dot_nt(k_aug, q_aug)
            if masked:
                s_t = jnp.where(keep, s_t, NEG_INF)
            m_old = m_scr[h]
            m_new = jnp.maximum(m_old, jnp.max(s_t, axis=0, keepdims=True))
            alpha = jnp.exp2(m_old - m_new)
            p_t = jnp.exp2(s_t - m_new)
            l_scr[h] = alpha * l_scr[h] + jnp.sum(p_t, axis=0, keepdims=True)
            acc_scr[h] = alpha * acc_scr[h] + _dot_nn(v_ref[:, hs].T, p_t.astype(BF16))
            m_scr[h] = m_new

    @pl.when(ki < qi)
    def _():
        step(False)

    @pl.when(ki == qi)
    def _():
        step(True)

    @pl.when(ki == pl.num_programs(2) - 1)
    def _():
        for h in range(n_heads):
            o_ref[:, h * dh:(h + 1) * dh] = (acc_scr[h] / l_scr[h]).T


def _fox_prompt(qf, kfb, vfb, key_bias, batch, seq, n_heads, dh, tq):
    nq = seq // tq
    width = n_heads * dh
    kv_map = lambda b, qi, ki: (b * nq + jnp.minimum(ki, qi), 0)
    return pl.pallas_call(
        functools.partial(_fox_prompt_body, n_heads=n_heads, dh=dh),
        grid=(batch, nq, nq),
        in_specs=[pl.BlockSpec((tq, width), lambda b, qi, ki: (b * nq + qi, 0)),
                  pl.BlockSpec((tq, width), kv_map),
                  pl.BlockSpec((tq, width), kv_map),
                  pl.BlockSpec((tq, LANES), kv_map)],
        out_specs=pl.BlockSpec((tq, width), lambda b, qi, ki: (b * nq + qi, 0)),
        out_shape=jax.ShapeDtypeStruct((batch * seq, width), F32),
        scratch_shapes=[pltpu.VMEM((n_heads, 1, tq), F32), pltpu.VMEM((n_heads, 1, tq), F32),
                        pltpu.VMEM((n_heads, dh, tq), F32)],
        compiler_params=_cparams(("arbitrary", "arbitrary", "arbitrary")),
        name="fox_prompt",
    )(qf, kfb, vfb, key_bias)


PAGE_SLOTS = 3


def _fox_sample_body(pt_ref, q_ref, kn_ref, vn_ref, lfn_ref, ck_hbm, cv_hbm, lf_hbm, o_ref,
                     kbuf, vbuf, lfbuf, sem, m_scr, l_scr, acc_scr, carry_scr,
                     *, n_heads, pages_per_step, layer, n_pages):
    P = pages_per_step
    b = pl.program_id(0)
    j = pl.program_id(1)
    steps = pl.num_programs(1)
    rows = q_ref.shape[0]
    tiles = lfbuf.shape[2]

    def fetch(seq, stp, slot):
        for i in range(P):
            pg = pt_ref[seq, n_pages - 1 - (stp * P + i)]
            pltpu.make_async_copy(ck_hbm.at[layer, pg], kbuf.at[slot, i], sem.at[slot]).start()
            pltpu.make_async_copy(cv_hbm.at[layer, pg], vbuf.at[slot, i], sem.at[slot]).start()
            pltpu.make_async_copy(lf_hbm.at[layer, pg], lfbuf.at[slot, i], sem.at[slot]).start()

    n_slots = kbuf.shape[0]
    step = b * steps + j
    total = pl.num_programs(0) * steps
    slot = step % n_slots

    def fetch_step(s):
        fetch(s // steps, s % steps, s % n_slots)

    @pl.when(step == 0)
    def _():
        for ahead in range(n_slots - 1):
            @pl.when(ahead < total)
            def _():
                fetch_step(jnp.int32(ahead))

    @pl.when(step + n_slots - 1 < total)
    def _():
        fetch_step(step + n_slots - 1)

    q = q_ref[...]
    lane = lax.broadcasted_iota(I32, (rows, LANES), 1)
    rowi = lax.broadcasted_iota(I32, (rows, LANES), 0)
    same_head = (lane % n_heads) == (rowi % n_heads)
    l_i = lax.broadcasted_iota(I32, (LANES, LANES), 0)
    l_j = lax.broadcasted_iota(I32, (LANES, LANES), 1)
    head_eq = (l_i % n_heads) == (l_j % n_heads)

    @pl.when(j == 0)
    def _():
        fn = _dot_exact_rhs01(lfn_ref[...], (head_eq & (l_i <= l_j)).astype(BF16))[0:1, :]
        s = _dot_nt(q, kn_ref[...]) - fn
        keep = same_head & (lane // n_heads <= rowi // n_heads) & (lane < rows)
        s = jnp.where(keep, s, NEG_INF)
        m = jnp.max(s, axis=-1, keepdims=True)
        p = jnp.exp(s - m)
        m_scr[...] = m
        l_scr[...] = jnp.sum(p, axis=-1, keepdims=True)
        acc_scr[...] = _dot_nn(p.astype(BF16), vn_ref[...])
        carry_scr[...] = jnp.zeros(carry_scr.shape, F32)

    pltpu.make_async_copy(ck_hbm.at[layer, pl.ds(0, P)], kbuf.at[slot], sem.at[slot]).wait()
    pltpu.make_async_copy(cv_hbm.at[layer, pl.ds(0, P)], vbuf.at[slot], sem.at[slot]).wait()
    pltpu.make_async_copy(lf_hbm.at[layer, pl.ds(0, P)], lfbuf.at[slot], sem.at[slot]).wait()

    x = lfbuf[slot].reshape(P * tiles, LANES)
    within = _dot_exact_rhs01(x, (head_eq & (l_i > l_j)).astype(BF16))
    tot = _dot_exact_rhs01(x, head_eq.astype(BF16))
    nr = P * tiles
    r_i = lax.broadcasted_iota(I32, (nr, nr), 0)
    c_i = lax.broadcasted_iota(I32, (nr, nr), 1)
    later_rows = ((c_i // tiles < r_i // tiles) | ((c_i // tiles == r_i // tiles) & (c_i > r_i))).astype(BF16)
    carry = carry_scr[...]
    suf = within + _dot_exact_lhs01(later_rows, tot) + carry
    carry_scr[...] = carry + jnp.sum(tot, axis=0, keepdims=True)

    k2 = kbuf[slot].reshape(-1, LANES).astype(BF16)
    v2 = vbuf[slot].reshape(-1, LANES).astype(BF16)
    s = _dot_nt(q, k2)
    blocks = [jnp.where(same_head, s[:, c * LANES:(c + 1) * LANES] + suf[c:c + 1, :], NEG_INF) for c in range(nr)]
    blk_max = blocks[0]
    for blk in blocks[1:]:
        blk_max = jnp.maximum(blk_max, blk)
    m_old = m_scr[...]
    m_new = jnp.maximum(m_old, jnp.max(blk_max, axis=-1, keepdims=True))
    alpha = jnp.exp(m_old - m_new)
    probs = [jnp.exp(blk - m_new) for blk in blocks]
    psum = probs[0]
    for pb in probs[1:]:
        psum = psum + pb
    l_scr[...] = alpha * l_scr[...] + jnp.sum(psum, axis=-1, keepdims=True)
    p_all = jnp.concatenate([pb.astype(BF16) for pb in probs], axis=1)
    acc_scr[...] = alpha * acc_scr[...] + _dot_nn(p_all, v2)
    m_scr[...] = m_new

    @pl.when(j == pl.num_programs(1) - 1)
    def _():
        o_ref[...] = acc_scr[...] / l_scr[...]


def _fox_sample_call(page_table, qf, kf, vf, logf, cache_k, cache_v, cache_logf, layer, db, t_new, n_heads, dh,
                     pages_per_step):
    assert dh == LANES
    rows = t_new * n_heads
    n_pool, page = cache_k.shape[1:3]
    n_pages = page_table.shape[1]
    tiles = page * n_heads // LANES
    P = pages_per_step
    steps = n_pages // P
    pad_rows = lambda a: jnp.pad(a.reshape(db, rows, dh), ((0, 0), (0, LANES - rows), (0, 0))).astype(BF16)
    q2 = qf.reshape(db, rows, dh).astype(BF16)
    lfn = jnp.pad(logf.reshape(db, 1, rows), ((0, 0), (0, 7), (0, LANES - rows)))
    lf_pages = cache_logf.reshape(cache_logf.shape[0], n_pool, tiles, LANES)

    per_seq = lambda r: pl.BlockSpec((None, r, LANES), lambda b, j, pt: (b, 0, 0))
    hbm = pl.BlockSpec(memory_space=pl.ANY)
    out = pl.pallas_call(
        functools.partial(_fox_sample_body, n_heads=n_heads, pages_per_step=P, layer=layer, n_pages=n_pages),
        grid_spec=pltpu.PrefetchScalarGridSpec(
            num_scalar_prefetch=1,
            grid=(db, steps),
            in_specs=[per_seq(rows), per_seq(LANES), per_seq(LANES), per_seq(8), hbm, hbm, hbm],
            out_specs=per_seq(rows),
            scratch_shapes=[pltpu.VMEM((PAGE_SLOTS, P, page, n_heads, dh), F32),
                            pltpu.VMEM((PAGE_SLOTS, P, page, n_heads, dh), F32),
                            pltpu.VMEM((PAGE_SLOTS, P, tiles, LANES), F32), pltpu.SemaphoreType.DMA((PAGE_SLOTS,)),
                            pltpu.VMEM((rows, 1), F32), pltpu.VMEM((rows, 1), F32),
                            pltpu.VMEM((rows, LANES), F32), pltpu.VMEM((1, LANES), F32)]),
        out_shape=jax.ShapeDtypeStruct((db, rows, LANES), F32),
        compiler_params=_cparams(("arbitrary", "arbitrary")),
        name="fox_sample",
    )(page_table, q2, pad_rows(kf), pad_rows(vf), lfn, cache_k, cache_v, lf_pages)
    return out.reshape(db * t_new, n_heads * dh)


def _rms_heads(x, n_heads, dh):
    outs = []
    for h in range(n_heads):
        xs = x[:, h * dh:(h + 1) * dh]
        outs.append(xs * lax.rsqrt(jnp.mean(xs * xs, axis=-1, keepdims=True) + EPS))
    return jnp.concatenate(outs, axis=1)


def _route(logits):
    lane = lax.broadcasted_iota(I32, logits.shape, 1)
    big = jnp.int32(LANES)
    gl = jnp.where(lane < N_GROUPS, logits, NEG_INF)
    gmax = jnp.max(gl, axis=-1, keepdims=True)
    g_sel = jnp.min(jnp.where(gl == gmax, lane, big), axis=-1, keepdims=True)
    p_g = 1.0 / jnp.sum(jnp.exp(gl - gmax), axis=-1, keepdims=True)
    lo = N_GROUPS + E_PER_GROUP * g_sel
    ev = jnp.where((lane >= lo) & (lane < lo + E_PER_GROUP), logits, NEG_INF)
    v1 = jnp.max(ev, axis=-1, keepdims=True)
    i1 = jnp.min(jnp.where(ev == v1, lane, big), axis=-1, keepdims=True)
    ev2 = jnp.where(lane == i1, NEG_INF, ev)
    v2 = jnp.max(ev2, axis=-1, keepdims=True)
    i2 = jnp.min(jnp.where(ev2 == v2, lane, big), axis=-1, keepdims=True)
    e21 = jnp.exp(v2 - v1)
    w1 = p_g / (1.0 + e21)
    w2 = p_g * e21 / (1.0 + e21)
    out = jnp.where(lane == 0, (i1 - N_GROUPS).astype(F32), 0.0)
    out = jnp.where(lane == 1, (i2 - N_GROUPS).astype(F32), out)
    out = jnp.where(lane == 2, w1, out)
    out = jnp.where(lane == 3, w2, out)
    return out


def _store_slabs(ref, x):
    tm, d = x.shape
    per = d // LANES
    for s in range(per):
        ref[pl.ds(s, tm, stride=per), :] = x[:, s * LANES:(s + 1) * LANES]


def _load_slabs(ref, tm, per):
    return jnp.concatenate([ref[pl.ds(s, tm, stride=per), :] for s in range(per)], axis=1)


def _merge_body(x_ref, og_ref, gg_ref, of_ref, gw_ref, fw_ref, wo_ref, g1_ref, sh2_ref, sc2_ref, n2_ref,
                wr_hi_ref, wr_lo_ref, br_ref, *rest, n_heads, dh, n_tiles):
    x1_ref, h2_ref, route_ref = rest[-3:]
    i = pl.program_id(0)

    @pl.when(i < n_tiles)
    def _():
        og = _rms_heads(og_ref[...], n_heads, dh) * gw_ref[...] * _silu(gg_ref[...])
        of = _rms_heads(of_ref[...], n_heads, dh) * fw_ref[...]
        merged = jnp.concatenate([og, of], axis=1).astype(BF16)
        x1 = x_ref[...] + g1_ref[...] * _dot_nn(merged, wo_ref[...])
        x1_ref[...] = x1
        y = x1 * lax.rsqrt(jnp.mean(x1 * x1, axis=-1, keepdims=True) + EPS) * n2_ref[...]
        h2 = y * (1.0 + sc2_ref[...]) + sh2_ref[...]
        _store_slabs(h2_ref, h2)
        hh, hl, _ = _split3(h2)
        logits = (_dot_nn(hh, wr_hi_ref[...]) + _dot_nn(hl, wr_hi_ref[...]) + _dot_nn(hh, wr_lo_ref[...])
                  + br_ref[...])
        route_ref[...] = _route(logits)

    @pl.when(i >= n_tiles)
    def _():
        h2_ref[...] = jnp.zeros(h2_ref.shape, F32)


def _merge(x, o_gla, gla_in, o_fox, gla_w, fox_w, w_out_bf, g1, sh2, sc2, norm2_w, wr_hi, wr_lo, b_r,
           tm, rows_per_mod, n_heads, dh, slab_rows, row0, into=None):
    n, d = x.shape
    half = n_heads * dh
    blk0 = row0 // tm
    per = d // LANES
    n_tiles = n // tm
    first = into is None
    extra = -(-(slab_rows - n) // tm) if first else 0
    assert extra <= 1
    aliased = [] if first else [into]
    n_in = 14
    last = n_tiles - 1
    per_token = rows_per_mod == 1
    if per_token:
        mod_spec = pl.BlockSpec((tm, d), lambda i: (jnp.minimum(i, last), 0))
    else:
        tiles_per_mod = rows_per_mod // tm
        g1, sh2, sc2 = (a.reshape(a.shape[0], 1, d) for a in (g1, sh2, sc2))
        mod_spec = pl.BlockSpec((None, 1, d), lambda i: (jnp.minimum(i, last) // tiles_per_mod, 0, 0))
    row = lambda i: (jnp.minimum(i, last), 0)
    fixed = lambda i: (0, 0)
    return pl.pallas_call(
        functools.partial(_merge_body, n_heads=n_heads, dh=dh, n_tiles=n_tiles),
        grid=(n_tiles + extra,),
        in_specs=[pl.BlockSpec((tm, d), row),
                  pl.BlockSpec((tm, half), row),
                  pl.BlockSpec((tm, half), lambda i: (jnp.minimum(i, last), 2)),
                  pl.BlockSpec((tm, half), row),
                  pl.BlockSpec((1, half), fixed), pl.BlockSpec((1, half), fixed),
                  pl.BlockSpec((d, d), fixed),
                  mod_spec, mod_spec, mod_spec,
                  pl.BlockSpec((1, d), fixed),
                  pl.BlockSpec((d, LANES), fixed), pl.BlockSpec((d, LANES), fixed),
                  pl.BlockSpec((1, LANES), fixed)] + [pl.BlockSpec(memory_space=pl.ANY)] * len(aliased),
        out_specs=(pl.BlockSpec((tm, d), row), pl.BlockSpec((tm * per, LANES), lambda i: (blk0 + i, 0)),
                   pl.BlockSpec((tm, LANES), row)),
        out_shape=(jax.ShapeDtypeStruct((n, d), F32), jax.ShapeDtypeStruct((slab_rows * per, LANES), F32),
                   jax.ShapeDtypeStruct((n, LANES), F32)),
        input_output_aliases={n_in: 1} if aliased else {},
        compiler_params=_cparams(("arbitrary",)),
        name="merge_outproj",
    )(x, o_gla, gla_in, o_fox, gla_w.reshape(1, half), fox_w.reshape(1, half), w_out_bf, g1, sh2, sc2,
      norm2_w.reshape(1, d), wr_hi, wr_lo, b_r, *aliased)


GATHER_UNROLL = 8


def _experts_body(te_ref, first_ref, tok_ref, nt_ref, grp_ref, nxt_ref, h_hbm, wg_hbm, wu_hbm, wd_hbm, y_ref,
                  xbuf, wg_f, wu_f, wd_f, wg_bf, wu_bf, wd_bf, sem, wsem, *, tm, per):
    i = pl.program_id(0)
    n_active = nt_ref[0]

    def weight_copies(expert, wslot):
        return (pltpu.make_async_copy(wg_hbm.at[expert], wg_f.at[wslot], wsem.at[wslot]),
                pltpu.make_async_copy(wu_hbm.at[expert], wu_f.at[wslot], wsem.at[wslot]),
                pltpu.make_async_copy(wd_hbm.at[expert], wd_f.at[wslot], wsem.at[wslot]))

    def gather(tile, slot):
        first = first_ref[tile]

        def issue(g, carry):
            for u in range(GATHER_UNROLL):
                r = g * GATHER_UNROLL + u
                src = pl.multiple_of(tok_ref[first + r] * per, per)
                pltpu.make_async_copy(h_hbm.at[pl.ds(src, per), :], xbuf.at[slot, pl.ds(r * per, per), :],
                                      sem.at[slot]).start()
            return carry
        lax.fori_loop(0, tm // GATHER_UNROLL, issue, 0)

    @pl.when((i == 0) & (n_active > 0))
    def _():
        gather(0, 0)
        for cp in weight_copies(te_ref[0], 0):
            cp.start()

    @pl.when(i + 1 < n_active)
    def _():
        gather(i + 1, (i + 1) % 2)

    @pl.when(i < n_active)
    def _():
        slot = i % 2
        prev = te_ref[jnp.maximum(i - 1, 0)]

        @pl.when((i == 0) | (te_ref[i] != prev))
        def _():
            wslot = grp_ref[i] % 2

            @pl.when(nxt_ref[i] >= 0)
            def _():
                for cp in weight_copies(nxt_ref[i], 1 - wslot):
                    cp.start()

            for cp in weight_copies(te_ref[i], wslot):
                cp.wait()
            wg_bf[...] = wg_f[wslot].astype(BF16)
            wu_bf[...] = wu_f[wslot].astype(BF16)
            wd_bf[...] = wd_f[wslot].astype(BF16)

        pltpu.make_async_copy(h_hbm.at[pl.ds(0, tm * per), :], xbuf.at[slot], sem.at[slot]).wait()
        x = _load_slabs(xbuf.at[slot], tm, per).astype(BF16)
        a = (_silu(_dot_nn(x, wg_bf[...])) * _dot_nn(x, wu_bf[...])).astype(BF16)
        _store_slabs(y_ref, _dot_nn(a, wd_bf[...]))

    @pl.when(i >= n_active)
    def _():
        y_ref[...] = jnp.zeros(y_ref.shape, F32)


def _experts(tile_expert, tile_first, sorted_tok, n_tiles, tile_group, next_expert, h2_slabs,
             w_gate, w_up, w_down, tm):
    max_tiles = tile_expert.shape[0]
    _, d, dff = w_gate.shape
    per = d // LANES
    hbm = pl.BlockSpec(memory_space=pl.ANY)
    return pl.pallas_call(
        functools.partial(_experts_body, tm=tm, per=per),
        grid_spec=pltpu.PrefetchScalarGridSpec(
            num_scalar_prefetch=6,
            grid=(max_tiles,),
            in_specs=[hbm, hbm, hbm, hbm],
            out_specs=pl.BlockSpec((tm * per, LANES), lambda i, *prefetch: (i, 0)),
            scratch_shapes=[pltpu.VMEM((2, tm * per, LANES), F32),
                            pltpu.VMEM((2, d, dff), F32), pltpu.VMEM((2, d, dff), F32), pltpu.VMEM((2, dff, d), F32),
                            pltpu.VMEM((d, dff), BF16), pltpu.VMEM((d, dff), BF16), pltpu.VMEM((dff, d), BF16),
                            pltpu.SemaphoreType.DMA((2,)), pltpu.SemaphoreType.DMA((2,))]),
        out_shape=jax.ShapeDtypeStruct((max_tiles * tm * per, LANES), F32),
        compiler_params=_cparams(("arbitrary",)),
        name="experts",
    )(tile_expert, tile_first, sorted_tok, n_tiles, tile_group, next_expert, h2_slabs, w_gate, w_up, w_down)


def _combine_body(pos_ref, y_hbm, x1_ref, route_ref, g2_ref, fw_ref, o_ref, buf, sem, *, tm, row0, per):
    i = pl.program_id(0)

    def gather(tile, slot):
        def issue(g, carry):
            for u in range(GATHER_UNROLL // 2):
                r = g * (GATHER_UNROLL // 2) + u
                tok = row0 + tile * tm + r
                for c in range(2):
                    src = pl.multiple_of(pos_ref[2 * tok + c] * per, per)
                    pltpu.make_async_copy(y_hbm.at[pl.ds(src, per), :],
                                          buf.at[slot, c, pl.ds(r * per, per), :], sem.at[slot]).start(priority=1)
            return carry
        lax.fori_loop(0, tm // (GATHER_UNROLL // 2), issue, 0)

    @pl.when(i == 0)
    def _():
        gather(0, 0)

    @pl.when(i + 1 < pl.num_programs(0))
    def _():
        gather(i + 1, (i + 1) % 2)

    slot = i % 2
    for c in range(2):
        pltpu.make_async_copy(y_hbm.at[pl.ds(0, tm * per), :], buf.at[slot, c], sem.at[slot]).wait()
    route = route_ref[...]
    moe = (route[:, 2:3] * _load_slabs(buf.at[slot, 0], tm, per)
           + route[:, 3:4] * _load_slabs(buf.at[slot, 1], tm, per))
    x2 = x1_ref[...] + g2_ref[...] * moe
    o_ref[...] = x2 * lax.rsqrt(jnp.mean(x2 * x2, axis=-1, keepdims=True) + EPS) * fw_ref[...]


def _combine(pos, y_slabs, x1, route, g2, final_w, row0, n_rows, tm, rows_per_mod):
    d = x1.shape[1]
    per = d // LANES
    per_token = rows_per_mod == 1
    if per_token:
        mod_spec = pl.BlockSpec((tm, d), lambda i, pos: (i, 0))
    else:
        tiles_per_mod = rows_per_mod // tm
        g2 = g2.reshape(g2.shape[0], 1, d)
        mod_spec = pl.BlockSpec((None, 1, d), lambda i, pos: (i // tiles_per_mod, 0, 0))
    return pl.pallas_call(
        functools.partial(_combine_body, tm=tm, row0=row0, per=per),
        grid_spec=pltpu.PrefetchScalarGridSpec(
            num_scalar_prefetch=1,
            grid=(n_rows // tm,),
            in_specs=[pl.BlockSpec(memory_space=pl.ANY),
                      pl.BlockSpec((tm, d), lambda i, pos: (i, 0)),
                      pl.BlockSpec((tm, LANES), lambda i, pos: (i, 0)),
                      mod_spec,
                      pl.BlockSpec((1, d), lambda i, pos: (0, 0))],
            out_specs=pl.BlockSpec((tm, d), lambda i, pos: (i, 0)),
            scratch_shapes=[pltpu.VMEM((2, 2, tm * per, LANES), F32), pltpu.SemaphoreType.DMA((2,))]),
        out_shape=jax.ShapeDtypeStruct((n_rows, d), F32),
        compiler_params=_cparams(("arbitrary",)),
        name="combine_norm",
    )(pos, y_slabs, x1, route, g2, final_w.reshape(1, d))


def _invert_body(where_ref, out_ref, *, n_pairs, n_out):
    def fill(g, carry):
        for u in range(GATHER_UNROLL):
            a = g * GATHER_UNROLL + u
            out_ref[where_ref[a]] = lax.shift_right_logical(a, 1)
        return carry
    lax.fori_loop(0, n_pairs // GATHER_UNROLL, fill, 0)

    def tail(t, carry):
        out_ref[n_pairs + t] = 0
        return carry
    lax.fori_loop(0, n_out - n_pairs, tail, 0)


def _sorted_tokens(where, n_out):
    n_pairs = where.shape[0]
    assert n_pairs % GATHER_UNROLL == 0
    smem = pl.BlockSpec(memory_space=pltpu.SMEM)
    return pl.pallas_call(
        functools.partial(_invert_body, n_pairs=n_pairs, n_out=n_out),
        in_specs=[smem], out_specs=smem,
        out_shape=jax.ShapeDtypeStruct((n_out,), I32),
        name="sorted_tokens",
    )(where)


def _plan(route, n_experts, tm):
    n = route.shape[0]
    e_flat = route[:, 0:2].astype(I32).reshape(-1)
    onehot = (e_flat[:, None] == jnp.arange(n_experts, dtype=I32)[None, :]).astype(I32)
    csum = jnp.cumsum(onehot, axis=0)
    counts = csum[-1]
    rank = jnp.sum(onehot * csum, axis=1) - 1
    tiles_e = (counts + tm - 1) // tm
    tile_end = jnp.cumsum(tiles_e)
    tile_off = tile_end - tiles_e
    cnt_off = jnp.cumsum(counts) - counts
    max_tiles = (2 * n) // tm + n_experts
    pos = jnp.sum(onehot * (tile_off * tm)[None, :], axis=1) + rank
    where = jnp.sum(onehot * cnt_off[None, :], axis=1) + rank
    tile_ids = jnp.arange(max_tiles, dtype=I32)
    live_ids = jnp.minimum(tile_ids, tile_end[-1] - 1)
    tile_expert = jnp.minimum(jnp.sum((tile_end[None, :] <= live_ids[:, None]).astype(I32), axis=1), n_experts - 1)
    is_e = (tile_expert[:, None] == jnp.arange(n_experts, dtype=I32)[None, :]).astype(I32)
    tile_first = jnp.sum(is_e * (cnt_off - tile_off * tm)[None, :], axis=1) + tile_ids * tm
    tile_first = jnp.clip(tile_first, 0, 2 * n)
    e_ids = jnp.arange(n_experts, dtype=I32)
    used = tiles_e > 0
    run_of_e = jnp.cumsum(used.astype(I32)) - 1
    later_used = used[None, :] & (e_ids[None, :] > e_ids[:, None])
    next_of_e = jnp.min(jnp.where(later_used, e_ids[None, :], n_experts), axis=1)
    next_of_e = jnp.where(next_of_e < n_experts, next_of_e, -1)
    tile_group = jnp.sum(is_e * run_of_e[None, :], axis=1)
    next_expert = jnp.sum(is_e * next_of_e[None, :], axis=1)
    return (tile_expert, tile_first.astype(I32), tile_end[-1].reshape(1).astype(I32), tile_group.astype(I32),
            next_expert.astype(I32), pos.astype(I32), where.astype(I32))


def kernel(x_prompt, x_sample, cache_k, cache_v, cache_logf, state_gla, page_table, c_prompt, c_sample,
           norm1_w, norm2_w, w_ada, b_ada, w_in, w_gk2, b_gk, b_fgate, gla_onorm_w, fox_onorm_w, w_out,
           w_rg, b_rg, w_re, b_re, w_gate_e, w_up_e, w_down_e, final_norm_w):
    depth = w_in.shape[0]
    assert depth == 1, "single-layer trunk"
    batch, seq, d = x_prompt.shape
    db, t_new, _ = x_sample.shape
    n_hg, dk, dv = state_gla.shape[2:]
    n_hf, dh = cache_k.shape[3:]
    rank = w_gk2.shape[1]
    n_experts = w_gate_e.shape[1]
    wq = n_hg * dk
    wv = n_hg * dv
    wf = n_hf * dh
    assert wv == 1024 and wf == 1024 and 2 * wq == 1024 and rank == 16 and n_hf == 8
    n_p = batch * seq
    n_s = db * t_new
    layer = 0

    wt = w_in[layer].T
    o_alr = 2 * wq + 2 * wv
    o_fox = o_alr + rank
    o_fl = o_fox + 3 * wf
    assert o_fl - o_fox == o_alr
    w_main = _pack_w_in(wt, o_alr, o_fox)
    w_small = jnp.concatenate([wt[o_alr:o_fox], wt[o_fl:], jnp.zeros((LANES - rank - n_hf, d), F32)], axis=0)
    b_small = jnp.zeros((1, LANES), F32).at[0, rank:rank + n_hf].set(b_fgate[layer])
    w_gk2_pad = jnp.zeros((LANES, wq), F32).at[:rank].set(w_gk2[layer])
    b_gk2 = b_gk[layer].reshape(1, wq)
    w_out_bf = w_out[layer].astype(BF16)
    w_r = jnp.concatenate([w_rg[layer], w_re[layer],
                           jnp.zeros((d, LANES - N_GROUPS - n_experts), F32)], axis=1)
    wr_hi = w_r.astype(BF16)
    wr_lo = (w_r - wr_hi.astype(F32)).astype(BF16)
    b_r = jnp.concatenate([b_rg[layer], b_re[layer], jnp.zeros((LANES - N_GROUPS - n_experts,), F32)]).reshape(1, LANES)

    n_c = batch + db
    c_all = jnp.concatenate([c_prompt, c_sample, jnp.zeros((-n_c % 8, d), F32)], axis=0)
    mod = _adaln(c_all, w_ada[layer], b_ada[layer])
    sh1, sc1, g1, sh2, sc2, g2 = (mod[:, i * d:(i + 1) * d] for i in range(6))
    p_rows = slice(0, batch)
    rep = lambda a: jnp.repeat(a[batch:n_c], t_new, axis=0)

    q_scale = dh ** -0.5
    xp = x_prompt.reshape(n_p, d)
    xs = x_sample.reshape(n_s, d)

    tm_p = 512
    gla_p, qf_p, kf_p, vf_p, kfb_p, vfb_p, small_p = _inproj(
        xp, sh1[p_rows], sc1[p_rows], norm1_w[layer], w_main, w_small, b_small, tm_p, seq, q_scale * LOG2E)
    s0_p = jnp.zeros((batch, n_hg, dk, dv), F32)
    o_gla_p, gla_state_p = _gla(gla_p, small_p, w_gk2_pad, b_gk2, s0_p, batch, seq, 128, n_hg, dk, dv)
    logf_p = small_p[:, rank:rank + n_hf]
    key_bias = _key_bias(small_p, batch, seq, rank, n_hf)
    o_fox_p = _fox_prompt(qf_p, kfb_p, vfb_p, key_bias, batch, seq, n_hf, dh, 512)
    n_all = n_p + n_s
    x1_p, h2_p, route_p = _merge(xp, o_gla_p, gla_p, o_fox_p, gla_onorm_w[layer], fox_onorm_w[layer], w_out_bf,
                                 g1[p_rows], sh2[p_rows], sc2[p_rows], norm2_w[layer], wr_hi, wr_lo, b_r,
                                 256, seq, n_hg, dv, n_all, 0)

    gla_s, qf_s, kf_s, vf_s, _, _, small_s = _inproj(
        xs, rep(sh1), rep(sc1), norm1_w[layer], w_main, w_small, b_small, n_s, 1, q_scale)
    t_pad = GLA_CHUNK
    pad_t = lambda a: jnp.pad(a.reshape(db, t_new, -1), ((0, 0), (0, t_pad - t_new), (0, 0))).reshape(db * t_pad, -1)
    o_gla_s_pad, gla_state_s = _gla(pad_t(gla_s), pad_t(small_s), w_gk2_pad, b_gk2, state_gla[layer],
                                    db, t_pad, t_pad, n_hg, dk, dv, t_valid=t_new)
    o_gla_s = o_gla_s_pad.reshape(db, t_pad, wv)[:, :t_new].reshape(n_s, wv)

    logf_s = small_s[:, rank:rank + n_hf]
    o_fox_s = _fox_sample_call(page_table, qf_s, kf_s, vf_s, logf_s, cache_k, cache_v, cache_logf,
                               layer, db, t_new, n_hf, dh, 8)
    x1_s, h2, route_s = _merge(xs, o_gla_s, gla_s, o_fox_s, gla_onorm_w[layer], fox_onorm_w[layer], w_out_bf,
                               rep(g1), rep(sh2), rep(sc2), norm2_w[layer], wr_hi, wr_lo, b_r,
                               n_s, 1, n_hg, dv, n_all, n_p, into=h2_p)

    tm_e = 256
    route = jnp.concatenate([route_p, route_s], axis=0)
    tile_expert, tile_first, n_tiles, tile_group, next_expert, pos, where = _plan(route, n_experts, tm_e)
    sorted_tok = _sorted_tokens(where, where.shape[0] + tm_e)
    y_slabs = _experts(tile_expert, tile_first, sorted_tok, n_tiles, tile_group, next_expert, h2,
                       w_gate_e[layer], w_up_e[layer], w_down_e[layer], tm_e)
    y_p = _combine(pos, y_slabs, x1_p, route_p, g2[p_rows], final_norm_w, 0, n_p, 256, seq)
    y_s = _combine(pos, y_slabs, x1_s, route_s, rep(g2), final_norm_w, n_p, n_s, n_s, 1)

    y_prompt = y_p.reshape(batch, seq, d)
    y_sample = y_s.reshape(db, t_new, d)
    k_prompt = kf_p.reshape(1, batch, seq, n_hf, dh)
    v_prompt = vf_p.reshape(1, batch, seq, n_hf, dh)
    logf_prompt = logf_p.reshape(1, batch, seq, n_hf)
    k_sample = kf_s.reshape(1, db, t_new, n_hf, dh)
    v_sample = vf_s.reshape(1, db, t_new, n_hf, dh)
    logf_sample = logf_s.reshape(1, db, t_new, n_hf)
    return (y_prompt, y_sample, k_prompt, v_prompt, logf_prompt, gla_state_p[None],
            k_sample, v_sample, logf_sample, gla_state_s[None])
```

```python
import functools

import jax
import jax.numpy as jnp
from jax import lax
from jax.experimental import pallas as pl
from jax.experimental.pallas import tpu as pltpu

F32 = jnp.float32
BF16 = jnp.bfloat16
I32 = jnp.int32
EPS = 1e-6
NEG_INF = float("-inf")

LANES = 128
GLA_CHUNK = 16
GLA_GATE_NORM = 16.0
N_GROUPS = 4
E_PER_GROUP = 8
VMEM_LIMIT = 56 * 1024 * 1024


def _cparams(sem):
    return pltpu.CompilerParams(dimension_semantics=sem, vmem_limit_bytes=VMEM_LIMIT)


def _log_sigmoid(z):
    return jnp.minimum(z, 0.0) - jnp.log1p(jnp.exp(-jnp.abs(z)))


def _silu(z):
    return z * jax.nn.sigmoid(z)


def _split3(a):
    hi = a.astype(BF16)
    r = a - hi.astype(F32)
    mid = r.astype(BF16)
    lo = (r - mid.astype(F32)).astype(BF16)
    return hi, mid, lo


def _dot_nn(a, b):
    return jnp.dot(a, b, preferred_element_type=F32)


def _dot_nt(a, b):
    return lax.dot_general(a, b, (((1,), (1,)), ((), ())), preferred_element_type=F32)


def _dot_f32(a, b):
    ah, al, _ = _split3(a)
    bh, bl, _ = _split3(b)
    return _dot_nn(ah, bh) + _dot_nn(al, bh) + _dot_nn(ah, bl)


def _dot_exact_rhs01(a, ones_bf16):
    hi, mid, lo = _split3(a)
    return _dot_nn(hi, ones_bf16) + _dot_nn(mid, ones_bf16) + _dot_nn(lo, ones_bf16)


def _dot_exact_lhs01(ones_bf16, b):
    hi, mid, lo = _split3(b)
    return _dot_nn(ones_bf16, hi) + _dot_nn(ones_bf16, mid) + _dot_nn(ones_bf16, lo)


def _adaln_body(c_ref, w_ref, b_ref, o_ref):
    s = _silu(c_ref[...]).astype(BF16)
    o_ref[...] = _dot_nn(s, w_ref[...].astype(BF16)) + b_ref[...]


def _adaln(c_all, w_ada, b_ada, tn=1024):
    rows, d = c_all.shape
    n6 = w_ada.shape[1]
    return pl.pallas_call(
        _adaln_body,
        grid=(n6 // tn,),
        in_specs=[pl.BlockSpec((rows, d), lambda j: (0, 0)),
                  pl.BlockSpec((d, tn), lambda j: (0, j)),
                  pl.BlockSpec((1, tn), lambda j: (0, j))],
        out_specs=pl.BlockSpec((rows, tn), lambda j: (0, j)),
        out_shape=jax.ShapeDtypeStruct((rows, n6), F32),
        compiler_params=_cparams(("arbitrary",)),
        name="adaln",
    )(c_all, w_ada, b_ada.reshape(1, n6))


def _pack_w_in_body(a_ref, b_ref, c_ref, o_ref, *, shift, n_first):
    j = pl.program_id(0)

    @pl.when(j < n_first)
    def _():
        o_ref[...] = a_ref[...].astype(BF16)

    @pl.when(j >= n_first)
    def _():
        both = jnp.concatenate([b_ref[...], c_ref[...]], axis=0)
        o_ref[...] = both[shift:shift + o_ref.shape[0], :].astype(BF16)


def _pack_w_in(wt, rows, second_start):
    d = wt.shape[1]
    tr = 512
    shift = second_start % tr
    base = second_start - shift
    assert rows % tr == 0 and shift % 16 == 0 and 0 < shift and tr % shift == 0
    n_first = rows // tr
    second = lambda j: jnp.maximum(j - n_first, 0)
    return pl.pallas_call(
        functools.partial(_pack_w_in_body, shift=shift, n_first=n_first),
        grid=(2 * n_first,),
        in_specs=[pl.BlockSpec((tr, d), lambda j: (jnp.minimum(j, n_first - 1), 0)),
                  pl.BlockSpec((tr, d), lambda j: (base // tr + second(j), 0)),
                  pl.BlockSpec((shift, d), lambda j: ((base + tr) // shift + second(j) * (tr // shift), 0))],
        out_specs=pl.BlockSpec((tr, d), lambda j: (j, 0)),
        out_shape=jax.ShapeDtypeStruct((2 * rows, d), BF16),
        compiler_params=_cparams(("arbitrary",)),
        name="pack_w_in",
    )(wt, wt, wt)


def _inproj_body(x_ref, sh_ref, sc_ref, nw_ref, w_ref, ws_ref, bs_ref,
                 gla_ref, qf_ref, kf_ref, vf_ref, kfb_ref, vfb_ref, small_ref, h_scr, *, q_scale):
    j = pl.program_id(1)

    @pl.when(j == 0)
    def _():
        x = x_ref[...]
        y = x * lax.rsqrt(jnp.mean(x * x, axis=-1, keepdims=True) + EPS) * nw_ref[...]
        hb = (y * (1.0 + sc_ref[...]) + sh_ref[...]).astype(BF16)
        h_scr[...] = hb
        sm = _dot_nt(hb, ws_ref[...].astype(BF16))
        lane = lax.broadcasted_iota(I32, sm.shape, 1)
        small_ref[...] = jnp.where((lane >= 16) & (lane < 24), _log_sigmoid(sm + bs_ref[...]), sm)

    acc = _dot_nt(h_scr[...], w_ref[...])

    @pl.when(j < 3)
    def _():
        gla_ref[...] = acc

    @pl.when(j == 3)
    def _():
        qf_ref[...] = (acc * q_scale).astype(BF16)

    @pl.when(j == 4)
    def _():
        kf_ref[...] = acc
        kfb_ref[...] = acc.astype(BF16)

    @pl.when(j == 5)
    def _():
        vf_ref[...] = acc
        vfb_ref[...] = acc.astype(BF16)


def _inproj(x, sh, sc, norm_w, w_main, w_small, b_small, tm, rows_per_mod, q_scale):
    n, d = x.shape
    tn = w_main.shape[0] // 6
    per_token = rows_per_mod == 1
    if per_token:
        mod_spec = pl.BlockSpec((tm, d), lambda i, j: (i, 0))
    else:
        tiles_per_mod = rows_per_mod // tm
        sh = sh.reshape(sh.shape[0], 1, d)
        sc = sc.reshape(sc.shape[0], 1, d)
        mod_spec = pl.BlockSpec((None, 1, d), lambda i, j: (i // tiles_per_mod, 0, 0))
    row = lambda i, j: (i, 0)
    out_shapes = (jax.ShapeDtypeStruct((n, 3 * tn), F32),
                  jax.ShapeDtypeStruct((n, tn), BF16),
                  jax.ShapeDtypeStruct((n, tn), F32),
                  jax.ShapeDtypeStruct((n, tn), F32),
                  jax.ShapeDtypeStruct((n, tn), BF16),
                  jax.ShapeDtypeStruct((n, tn), BF16),
                  jax.ShapeDtypeStruct((n, LANES), F32))
    return pl.pallas_call(
        functools.partial(_inproj_body, q_scale=q_scale),
        grid=(n // tm, 6),
        in_specs=[pl.BlockSpec((tm, d), row), mod_spec, mod_spec,
                  pl.BlockSpec((1, d), lambda i, j: (0, 0)),
                  pl.BlockSpec((tn, d), lambda i, j: (j, 0)),
                  pl.BlockSpec((LANES, d), lambda i, j: (0, 0)),
                  pl.BlockSpec((1, LANES), lambda i, j: (0, 0))],
        out_specs=(pl.BlockSpec((tm, tn), lambda i, j: (i, jnp.minimum(j, 2))),
                   pl.BlockSpec((tm, tn), row), pl.BlockSpec((tm, tn), row), pl.BlockSpec((tm, tn), row),
                   pl.BlockSpec((tm, tn), row), pl.BlockSpec((tm, tn), row),
                   pl.BlockSpec((tm, LANES), row)),
        out_shape=out_shapes,
        scratch_shapes=[pltpu.VMEM((tm, d), BF16)],
        compiler_params=_cparams(("arbitrary", "arbitrary")),
        name="inproj",
    )(x, sh, sc, norm_w.reshape(1, d), w_main, w_small, b_small)


def _level_ref(b, half, rows_i):
    tt, n = b.shape
    if half >= 4:
        size = 2 * half
        blocks = [jnp.broadcast_to(b[g * size + half - 1:g * size + half, :], (size, n)) for g in range(tt // size)]
        return blocks[0] if len(blocks) == 1 else jnp.concatenate(blocks, axis=0)
    if half == 2:
        lo = jnp.concatenate([jnp.broadcast_to(b[8 * g + 1:8 * g + 2, :], (8, n)) for g in range(tt // 8)], axis=0)
        hi = jnp.concatenate([jnp.broadcast_to(b[8 * g + 5:8 * g + 6, :], (8, n)) for g in range(tt // 8)], axis=0)
        return jnp.where(rows_i % 8 < 4, lo, hi)
    return jnp.where(rows_i % 2 == 1, pltpu.roll(b, 1, 0), b)


def _gla_body(q_ref, k_ref, v_ref, sm_ref, w2_ref, b2_ref, s0_ref, o_ref, sout_ref, st_scr,
              *, tt, dk, dv, n_pairs, t_valid):
    t = pl.program_id(1)
    zero_blk = jnp.zeros((dv, dk), F32)

    @pl.when(t == 0)
    def _():
        for p in range(n_pairs):
            top = jnp.concatenate([s0_ref[2 * p].T, zero_blk], axis=1)
            bot = jnp.concatenate([zero_blk, s0_ref[2 * p + 1].T], axis=1)
            st_scr[p] = jnp.concatenate([top, bot], axis=0)

    gk = _log_sigmoid(_dot_f32(sm_ref[...], w2_ref[...]) + b2_ref[...]) / GLA_GATE_NORM
    rows_i = lax.broadcasted_iota(I32, (tt, 1), 0)
    if t_valid < tt:
        gk = jnp.where(rows_i < t_valid, gk, 0.0)
    r_i = lax.broadcasted_iota(I32, (tt, tt), 0)
    c_i = lax.broadcasted_iota(I32, (tt, tt), 1)
    b_all = _dot_exact_lhs01((c_i <= r_i).astype(BF16), gk)

    halves = []
    half = tt // 2
    while half >= 1:
        halves.append(half)
        half //= 2
    refs = [_level_ref(b_all, hf, rows_i) for hf in halves]
    valid = [(r_i // (2 * hf) == c_i // (2 * hf)) & (r_i % (2 * hf) >= hf) & (c_i % (2 * hf) < hf) for hf in halves]
    on_diag = r_i == c_i
    head0 = lax.broadcasted_iota(I32, (tt, 2 * dk), 1) < dk
    rr = lax.broadcasted_iota(I32, (2 * dv, 2 * dk), 0) // dv
    cc = lax.broadcasted_iota(I32, (2 * dv, 2 * dk), 1) // dk
    diag = (rr == cc).astype(F32)

    def head_scores(qx, kx):
        stacked = jnp.concatenate([jnp.where(head0, qx, 0.0), jnp.where(head0, 0.0, qx)], axis=0).astype(BF16)
        return _dot_nt(stacked, kx.astype(BF16))

    for p in range(n_pairs):
        cs = slice(p * 2 * dk, (p + 1) * 2 * dk)
        vs = slice(p * 2 * dv, (p + 1) * 2 * dv)
        q = q_ref[:, cs] * (dk ** -0.5)
        k = k_ref[:, cs]
        v = v_ref[:, vs]
        b = b_all[:, cs]
        sc = head_scores(q, k)
        a0 = jnp.where(on_diag, sc[:tt], 0.0)
        a1 = jnp.where(on_diag, sc[tt:], 0.0)
        for ref_all, ok in zip(refs, valid):
            ref = ref_all[:, cs]
            sc = head_scores(q * jnp.exp(jnp.minimum(b - ref, 0.0)), k * jnp.exp(jnp.minimum(ref - b, 0.0)))
            a0 = jnp.where(ok, sc[:tt], a0)
            a1 = jnp.where(ok, sc[tt:], a1)
        vb = v.astype(BF16)
        o_intra = jnp.concatenate([_dot_nn(a0.astype(BF16), vb[:, :dv]), _dot_nn(a1.astype(BF16), vb[:, dv:])], axis=1)

        st = st_scr[p]
        bend = b[tt - 1:tt]
        o_inter = _dot_nt((q * jnp.exp(b)).astype(BF16), st.astype(BF16))
        kdec = (k * jnp.exp(bend - b)).astype(BF16)
        st = st * jnp.exp(bend) + _dot_nn(v.T.astype(BF16), kdec) * diag
        st_scr[p] = st
        o_ref[:, vs] = o_intra + o_inter

    @pl.when(t == pl.num_programs(1) - 1)
    def _():
        for p in range(n_pairs):
            st = st_scr[p]
            sout_ref[2 * p] = st[:dv, :dk].T
            sout_ref[2 * p + 1] = st[dv:, dk:].T


def _gla(gla_in, small, w_gk2_pad, b_gk, s0, batch, seq, tt, n_heads, dk, dv, t_valid=None):
    n = batch * seq
    nt = seq // tt
    wq = n_heads * dk
    wv = n_heads * dv
    assert wv == 2 * wq
    row = lambda b, t: b * nt + t
    return pl.pallas_call(
        functools.partial(_gla_body, tt=tt, dk=dk, dv=dv, n_pairs=n_heads // 2,
                          t_valid=tt if t_valid is None else t_valid),
        grid=(batch, nt),
        in_specs=[pl.BlockSpec((tt, wq), lambda b, t: (row(b, t), 0)),
                  pl.BlockSpec((tt, wq), lambda b, t: (row(b, t), 1)),
                  pl.BlockSpec((tt, wv), lambda b, t: (row(b, t), 1)),
                  pl.BlockSpec((tt, LANES), lambda b, t: (row(b, t), 0)),
                  pl.BlockSpec((LANES, wq), lambda b, t: (0, 0)),
                  pl.BlockSpec((1, wq), lambda b, t: (0, 0)),
                  pl.BlockSpec((None, n_heads, dk, dv), lambda b, t: (b, 0, 0, 0))],
        out_specs=(pl.BlockSpec((tt, wv), lambda b, t: (row(b, t), 0)),
                   pl.BlockSpec((None, n_heads, dk, dv), lambda b, t: (b, 0, 0, 0))),
        out_shape=(jax.ShapeDtypeStruct((n, wv), F32),
                   jax.ShapeDtypeStruct((batch, n_heads, dk, dv), F32)),
        scratch_shapes=[pltpu.VMEM((n_heads // 2, 2 * dv, 2 * dk), F32)],
        compiler_params=_cparams(("arbitrary", "arbitrary")),
        name="gla",
    )(gla_in, gla_in, gla_in, small, w_gk2_pad, b_gk, s0)


LOG2E = 1.4426950408889634
BIAS_PARTS = 3


def _key_bias_body(sm_ref, o_ref, *, lane0, n_heads, blk):
    s = sm_ref.shape[0]
    r_i = lax.broadcasted_iota(I32, (blk, blk), 0)
    c_i = lax.broadcasted_iota(I32, (blk, blk), 1)
    lower = (c_i <= r_i).astype(BF16)
    head = r_i - lane0
    sel = [((head >= 0) & (head < n_heads) & (c_i == BIAS_PARTS * head + part)).astype(BF16)
           for part in range(BIAS_PARTS)]
    carry = jnp.zeros((1, sm_ref.shape[1]), F32)
    for i in range(s // blk):
        f = _dot_exact_lhs01(lower, sm_ref[i * blk:(i + 1) * blk, :]) + carry
        carry = f[blk - 1:blk, :]
        pieces = _split3(f * (-LOG2E))
        placed = sum(_dot_nn(piece, sel_p) for piece, sel_p in zip(pieces, sel))
        o_ref[i * blk:(i + 1) * blk, :] = placed.astype(BF16)


def _key_bias(small, batch, seq, lane0, n_heads):
    return pl.pallas_call(
        functools.partial(_key_bias_body, lane0=lane0, n_heads=n_heads, blk=LANES),
        grid=(batch,),
        in_specs=[pl.BlockSpec((seq, LANES), lambda i: (i, 0))],
        out_specs=pl.BlockSpec((seq, LANES), lambda i: (i, 0)),
        out_shape=jax.ShapeDtypeStruct((batch * seq, LANES), BF16),
        compiler_params=_cparams(("arbitrary",)),
        name="fox_key_bias",
    )(small)


def _fox_prompt_body(q_ref, k_ref, v_ref, kb_ref, o_ref, m_scr, l_scr, acc_scr, *, n_heads, dh):
    qi = pl.program_id(1)
    ki = pl.program_id(2)
    tq = q_ref.shape[0]
    tk = k_ref.shape[0]

    @pl.when(ki == 0)
    def _():
        m_scr[...] = jnp.full(m_scr.shape, NEG_INF, F32)
        l_scr[...] = jnp.zeros(l_scr.shape, F32)
        acc_scr[...] = jnp.zeros(acc_scr.shape, F32)

    def step(masked):
        lane = lax.broadcasted_iota(I32, (tq, LANES), 1)
        kb = kb_ref[...]
        if masked:
            keep = lax.broadcasted_iota(I32, (tk, tq), 0) <= lax.broadcasted_iota(I32, (tk, tq), 1)
        for h in range(n_heads):
            hs = slice(h * dh, (h + 1) * dh)
            ones = ((lane >= BIAS_PARTS * h) & (lane < BIAS_PARTS * (h + 1))).astype(BF16)
            q_aug = jnp.concatenate([q_ref[:, hs], ones], axis=1)
            k_aug = jnp.concatenate([k_ref[:, hs], kb], axis=1)
            s_t = _dot_nt(k_aug, q_aug)
            if masked:
                s_t = jnp.where(keep, s_t, NEG_INF)
            m_old = m_scr[h]
            m_new = jnp.maximum(m_old, jnp.max(s_t, axis=0, keepdims=True))
            alpha = jnp.exp2(m_old - m_new)
            p_t = jnp.exp2(s_t - m_new)
            l_scr[h] = alpha * l_scr[h] + jnp.sum(p_t, axis=0, keepdims=True)
            acc_scr[h] = alpha * acc_scr[h] + _dot_nn(v_ref[:, hs].T, p_t.astype(BF16))
            m_scr[h] = m_new

    @pl.when(ki < qi)
    def _():
        step(False)

    @pl.when(ki == qi)
    def _():
        step(True)

    @pl.when(ki == pl.num_programs(2) - 1)
    def _():
        for h in range(n_heads):
            o_ref[:, h * dh:(h + 1) * dh] = (acc_scr[h] / l_scr[h]).T


def _fox_prompt(qf, kfb, vfb, key_bias, batch, seq, n_heads, dh, tq):
    nq = seq // tq
    width = n_heads * dh
    kv_map = lambda b, qi, ki: (b * nq + jnp.minimum(ki, qi), 0)
    return pl.pallas_call(
        functools.partial(_fox_prompt_body, n_heads=n_heads, dh=dh),
        grid=(batch, nq, nq),
        in_specs=[pl.BlockSpec((tq, width), lambda b, qi, ki: (b * nq + qi, 0)),
                  pl.BlockSpec((tq, width), kv_map),
                  pl.BlockSpec((tq, width), kv_map),
                  pl.BlockSpec((tq, LANES), kv_map)],
        out_specs=pl.BlockSpec((tq, width), lambda b, qi, ki: (b * nq + qi, 0)),
        out_shape=jax.ShapeDtypeStruct((batch * seq, width), F32),
        scratch_shapes=[pltpu.VMEM((n_heads, 1, tq), F32), pltpu.VMEM((n_heads, 1, tq), F32),
                        pltpu.VMEM((n_heads, dh, tq), F32)],
        compiler_params=_cparams(("arbitrary", "arbitrary", "arbitrary")),
        name="fox_prompt",
    )(qf, kfb, vfb, key_bias)


PAGE_SLOTS = 4


def _fox_sample_body(pt_ref, q_ref, kn_ref, vn_ref, lfn_ref, ck_hbm, cv_hbm, lf_hbm, o_ref,
                     kbuf, vbuf, lfbuf, sem, m_scr, l_scr, acc_scr, carry_scr,
                     *, n_heads, pages_per_step, layer, n_pages):
    P = pages_per_step
    b = pl.program_id(0)
    j = pl.program_id(1)
    steps = pl.num_programs(1)
    rows = q_ref.shape[0]
    tiles = lfbuf.shape[2]

    def fetch(seq, stp, slot):
        for i in range(P):
            pg = pt_ref[seq, n_pages - 1 - (stp * P + i)]
            pltpu.make_async_copy(ck_hbm.at[layer, pg], kbuf.at[slot, i], sem.at[slot]).start()
            pltpu.make_async_copy(cv_hbm.at[layer, pg], vbuf.at[slot, i], sem.at[slot]).start()
            pltpu.make_async_copy(lf_hbm.at[layer, pg], lfbuf.at[slot, i], sem.at[slot]).start()

    n_slots = kbuf.shape[0]
    step = b * steps + j
    total = pl.num_programs(0) * steps
    slot = step % n_slots

    def fetch_step(s):
        fetch(s // steps, s % steps, s % n_slots)

    @pl.when(step == 0)
    def _():
        for ahead in range(n_slots - 1):
            @pl.when(ahead < total)
            def _():
                fetch_step(jnp.int32(ahead))

    @pl.when(step + n_slots - 1 < total)
    def _():
        fetch_step(step + n_slots - 1)

    q = q_ref[...]
    lane = lax.broadcasted_iota(I32, (rows, LANES), 1)
    rowi = lax.broadcasted_iota(I32, (rows, LANES), 0)
    same_head = (lane % n_heads) == (rowi % n_heads)
    l_i = lax.broadcasted_iota(I32, (LANES, LANES), 0)
    l_j = lax.broadcasted_iota(I32, (LANES, LANES), 1)
    head_eq = (l_i % n_heads) == (l_j % n_heads)

    @pl.when(j == 0)
    def _():
        fn = _dot_exact_rhs01(lfn_ref[...], (head_eq & (l_i <= l_j)).astype(BF16))[0:1, :]
        s = _dot_nt(q, kn_ref[...]) - fn
        keep = same_head & (lane // n_heads <= rowi // n_heads) & (lane < rows)
        s = jnp.where(keep, s, NEG_INF)
        m = jnp.max(s, axis=-1, keepdims=True)
        p = jnp.exp(s - m)
        m_scr[...] = m
        l_scr[...] = jnp.sum(p, axis=-1, keepdims=True)
        acc_scr[...] = _dot_nn(p.astype(BF16), vn_ref[...])
        carry_scr[...] = jnp.zeros(carry_scr.shape, F32)

    pltpu.make_async_copy(ck_hbm.at[layer, pl.ds(0, P)], kbuf.at[slot], sem.at[slot]).wait()
    pltpu.make_async_copy(cv_hbm.at[layer, pl.ds(0, P)], vbuf.at[slot], sem.at[slot]).wait()
    pltpu.make_async_copy(lf_hbm.at[layer, pl.ds(0, P)], lfbuf.at[slot], sem.at[slot]).wait()

    x = lfbuf[slot].reshape(P * tiles, LANES)
    within = _dot_exact_rhs01(x, (head_eq & (l_i > l_j)).astype(BF16))
    tot = _dot_exact_rhs01(x, head_eq.astype(BF16))
    nr = P * tiles
    r_i = lax.broadcasted_iota(I32, (nr, nr), 0)
    c_i = lax.broadcasted_iota(I32, (nr, nr), 1)
    later_rows = ((c_i // tiles < r_i // tiles) | ((c_i // tiles == r_i // tiles) & (c_i > r_i))).astype(BF16)
    carry = carry_scr[...]
    suf = within + _dot_exact_lhs01(later_rows, tot) + carry
    carry_scr[...] = carry + jnp.sum(tot, axis=0, keepdims=True)

    k2 = kbuf[slot].reshape(-1, LANES).astype(BF16)
    v2 = vbuf[slot].reshape(-1, LANES).astype(BF16)
    s = _dot_nt(q, k2)
    blocks = [jnp.where(same_head, s[:, c * LANES:(c + 1) * LANES] + suf[c:c + 1, :], NEG_INF) for c in range(nr)]
    blk_max = blocks[0]
    for blk in blocks[1:]:
        blk_max = jnp.maximum(blk_max, blk)
    m_old = m_scr[...]
    m_new = jnp.maximum(m_old, jnp.max(blk_max, axis=-1, keepdims=True))
    alpha = jnp.exp(m_old - m_new)
    probs = [jnp.exp(blk - m_new) for blk in blocks]
    psum = probs[0]
    for pb in probs[1:]:
        psum = psum + pb
    l_scr[...] = alpha * l_scr[...] + jnp.sum(psum, axis=-1, keepdims=True)
    p_all = jnp.concatenate([pb.astype(BF16) for pb in probs], axis=1)
    acc_scr[...] = alpha * acc_scr[...] + _dot_nn(p_all, v2)
    m_scr[...] = m_new

    @pl.when(j == pl.num_programs(1) - 1)
    def _():
        o_ref[...] = acc_scr[...] / l_scr[...]


def _fox_sample_call(page_table, qf, kf, vf, logf, cache_k, cache_v, cache_logf, layer, db, t_new, n_heads, dh,
                     pages_per_step):
    assert dh == LANES
    rows = t_new * n_heads
    n_pool, page = cache_k.shape[1:3]
    n_pages = page_table.shape[1]
    tiles = page * n_heads // LANES
    P = pages_per_step
    steps = n_pages // P
    pad_rows = lambda a: jnp.pad(a.reshape(db, rows, dh), ((0, 0), (0, LANES - rows), (0, 0))).astype(BF16)
    q2 = qf.reshape(db, rows, dh).astype(BF16)
    lfn = jnp.pad(logf.reshape(db, 1, rows), ((0, 0), (0, 7), (0, LANES - rows)))
    lf_pages = cache_logf.reshape(cache_logf.shape[0], n_pool, tiles, LANES)

    per_seq = lambda r: pl.BlockSpec((None, r, LANES), lambda b, j, pt: (b, 0, 0))
    hbm = pl.BlockSpec(memory_space=pl.ANY)
    out = pl.pallas_call(
        functools.partial(_fox_sample_body, n_heads=n_heads, pages_per_step=P, layer=layer, n_pages=n_pages),
        grid_spec=pltpu.PrefetchScalarGridSpec(
            num_scalar_prefetch=1,
            grid=(db, steps),
            in_specs=[per_seq(rows), per_seq(LANES), per_seq(LANES), per_seq(8), hbm, hbm, hbm],
            out_specs=per_seq(rows),
            scratch_shapes=[pltpu.VMEM((PAGE_SLOTS, P, page, n_heads, dh), F32),
                            pltpu.VMEM((PAGE_SLOTS, P, page, n_heads, dh), F32),
                            pltpu.VMEM((PAGE_SLOTS, P, tiles, LANES), F32), pltpu.SemaphoreType.DMA((PAGE_SLOTS,)),
                            pltpu.VMEM((rows, 1), F32), pltpu.VMEM((rows, 1), F32),
                            pltpu.VMEM((rows, LANES), F32), pltpu.VMEM((1, LANES), F32)]),
        out_shape=jax.ShapeDtypeStruct((db, rows, LANES), F32),
        compiler_params=_cparams(("arbitrary", "arbitrary")),
        name="fox_sample",
    )(page_table, q2, pad_rows(kf), pad_rows(vf), lfn, cache_k, cache_v, lf_pages)
    return out.reshape(db * t_new, n_heads * dh)


def _rms_heads(x, n_heads, dh):
    outs = []
    for h in range(n_heads):
        xs = x[:, h * dh:(h + 1) * dh]
        outs.append(xs * lax.rsqrt(jnp.mean(xs * xs, axis=-1, keepdims=True) + EPS))
    return jnp.concatenate(outs, axis=1)


def _route(logits):
    lane = lax.broadcasted_iota(I32, logits.shape, 1)
    big = jnp.int32(LANES)
    gl = jnp.where(lane < N_GROUPS, logits, NEG_INF)
    gmax = jnp.max(gl, axis=-1, keepdims=True)
    g_sel = jnp.min(jnp.where(gl == gmax, lane, big), axis=-1, keepdims=True)
    p_g = 1.0 / jnp.sum(jnp.exp(gl - gmax), axis=-1, keepdims=True)
    lo = N_GROUPS + E_PER_GROUP * g_sel
    ev = jnp.where((lane >= lo) & (lane < lo + E_PER_GROUP), logits, NEG_INF)
    v1 = jnp.max(ev, axis=-1, keepdims=True)
    i1 = jnp.min(jnp.where(ev == v1, lane, big), axis=-1, keepdims=True)
    ev2 = jnp.where(lane == i1, NEG_INF, ev)
    v2 = jnp.max(ev2, axis=-1, keepdims=True)
    i2 = jnp.min(jnp.where(ev2 == v2, lane, big), axis=-1, keepdims=True)
    e21 = jnp.exp(v2 - v1)
    w1 = p_g / (1.0 + e21)
    w2 = p_g * e21 / (1.0 + e21)
    out = jnp.where(lane == 0, (i1 - N_GROUPS).astype(F32), 0.0)
    out = jnp.where(lane == 1, (i2 - N_GROUPS).astype(F32), out)
    out = jnp.where(lane == 2, w1, out)
    out = jnp.where(lane == 3, w2, out)
    return out


def _store_slabs(ref, x):
    tm, d = x.shape
    per = d // LANES
    for s in range(per):
        ref[pl.ds(s, tm, stride=per), :] = x[:, s * LANES:(s + 1) * LANES]


def _load_slabs(ref, tm, per):
    return jnp.concatenate([ref[pl.ds(s, tm, stride=per), :] for s in range(per)], axis=1)


def _merge_body(x_ref, og_ref, gg_ref, of_ref, gw_ref, fw_ref, wo_ref, g1_ref, sh2_ref, sc2_ref, n2_ref,
                wr_hi_ref, wr_lo_ref, br_ref, *rest, n_heads, dh, n_tiles):
    x1_ref, h2_ref, route_ref = rest[-3:]
    i = pl.program_id(0)

    @pl.when(i < n_tiles)
    def _():
        og = _rms_heads(og_ref[...], n_heads, dh) * gw_ref[...] * _silu(gg_ref[...])
        of = _rms_heads(of_ref[...], n_heads, dh) * fw_ref[...]
        merged = jnp.concatenate([og, of], axis=1).astype(BF16)
        x1 = x_ref[...] + g1_ref[...] * _dot_nn(merged, wo_ref[...])
        x1_ref[...] = x1
        y = x1 * lax.rsqrt(jnp.mean(x1 * x1, axis=-1, keepdims=True) + EPS) * n2_ref[...]
        h2 = y * (1.0 + sc2_ref[...]) + sh2_ref[...]
        _store_slabs(h2_ref, h2)
        hh, hl, _ = _split3(h2)
        logits = (_dot_nn(hh, wr_hi_ref[...]) + _dot_nn(hl, wr_hi_ref[...]) + _dot_nn(hh, wr_lo_ref[...])
                  + br_ref[...])
        route_ref[...] = _route(logits)

    @pl.when(i >= n_tiles)
    def _():
        h2_ref[...] = jnp.zeros(h2_ref.shape, F32)


def _merge(x, o_gla, gla_in, o_fox, gla_w, fox_w, w_out_bf, g1, sh2, sc2, norm2_w, wr_hi, wr_lo, b_r,
           tm, rows_per_mod, n_heads, dh, slab_rows, row0, into=None):
    n, d = x.shape
    half = n_heads * dh
    blk0 = row0 // tm
    per = d // LANES
    n_tiles = n // tm
    first = into is None
    extra = -(-(slab_rows - n) // tm) if first else 0
    assert extra <= 1
    aliased = [] if first else [into]
    n_in = 14
    last = n_tiles - 1
    per_token = rows_per_mod == 1
    if per_token:
        mod_spec = pl.BlockSpec((tm, d), lambda i: (jnp.minimum(i, last), 0))
    else:
        tiles_per_mod = rows_per_mod // tm
        g1, sh2, sc2 = (a.reshape(a.shape[0], 1, d) for a in (g1, sh2, sc2))
        mod_spec = pl.BlockSpec((None, 1, d), lambda i: (jnp.minimum(i, last) // tiles_per_mod, 0, 0))
    row = lambda i: (jnp.minimum(i, last), 0)
    fixed = lambda i: (0, 0)
    return pl.pallas_call(
        functools.partial(_merge_body, n_heads=n_heads, dh=dh, n_tiles=n_tiles),
        grid=(n_tiles + extra,),
        in_specs=[pl.BlockSpec((tm, d), row),
                  pl.BlockSpec((tm, half), row),
                  pl.BlockSpec((tm, half), lambda i: (jnp.minimum(i, last), 2)),
                  pl.BlockSpec((tm, half), row),
                  pl.BlockSpec((1, half), fixed), pl.BlockSpec((1, half), fixed),
                  pl.BlockSpec((d, d), fixed),
                  mod_spec, mod_spec, mod_spec,
                  pl.BlockSpec((1, d), fixed),
                  pl.BlockSpec((d, LANES), fixed), pl.BlockSpec((d, LANES), fixed),
                  pl.BlockSpec((1, LANES), fixed)] + [pl.BlockSpec(memory_space=pl.ANY)] * len(aliased),
        out_specs=(pl.BlockSpec((tm, d), row), pl.BlockSpec((tm * per, LANES), lambda i: (blk0 + i, 0)),
                   pl.BlockSpec((tm, LANES), row)),
        out_shape=(jax.ShapeDtypeStruct((n, d), F32), jax.ShapeDtypeStruct((slab_rows * per, LANES), F32),
                   jax.ShapeDtypeStruct((n, LANES), F32)),
        input_output_aliases={n_in: 1} if aliased else {},
        compiler_params=_cparams(("arbitrary",)),
        name="merge_outproj",
    )(x, o_gla, gla_in, o_fox, gla_w.reshape(1, half), fox_w.reshape(1, half), w_out_bf, g1, sh2, sc2,
      norm2_w.reshape(1, d), wr_hi, wr_lo, b_r, *aliased)


GATHER_UNROLL = 8
GATHER_SLOTS = 3


def _experts_body(te_ref, first_ref, tok_ref, nt_ref, grp_ref, nxt_ref, h_hbm, wg_hbm, wu_hbm, wd_hbm, y_ref,
                  xbuf, wg_f, wu_f, wd_f, wg_bf, wu_bf, wd_bf, sem, wsem, *, tm, per):
    i = pl.program_id(0)
    n_active = nt_ref[0]

    def weight_copies(expert, wslot):
        return (pltpu.make_async_copy(wg_hbm.at[expert], wg_f.at[wslot], wsem.at[wslot]),
                pltpu.make_async_copy(wu_hbm.at[expert], wu_f.at[wslot], wsem.at[wslot]),
                pltpu.make_async_copy(wd_hbm.at[expert], wd_f.at[wslot], wsem.at[wslot]))

    def gather(tile, slot):
        first = first_ref[tile]

        def issue(g, carry):
            for u in range(GATHER_UNROLL):
                r = g * GATHER_UNROLL + u
                src = pl.multiple_of(tok_ref[first + r] * per, per)
                pltpu.make_async_copy(h_hbm.at[pl.ds(src, per), :], xbuf.at[slot, pl.ds(r * per, per), :],
                                      sem.at[slot]).start()
            return carry
        lax.fori_loop(0, tm // GATHER_UNROLL, issue, 0)

    n_slots = xbuf.shape[0]

    @pl.when((i == 0) & (n_active > 0))
    def _():
        for ahead in range(n_slots - 1):
            @pl.when(ahead < n_active)
            def _():
                gather(ahead, ahead)
        for cp in weight_copies(te_ref[0], 0):
            cp.start()

    @pl.when(i + n_slots - 1 < n_active)
    def _():
        gather(i + n_slots - 1, (i + n_slots - 1) % n_slots)

    @pl.when(i < n_active)
    def _():
        slot = i % n_slots
        prev = te_ref[jnp.maximum(i - 1, 0)]

        @pl.when((i == 0) | (te_ref[i] != prev))
        def _():
            wslot = grp_ref[i] % 2

            @pl.when(nxt_ref[i] >= 0)
            def _():
                for cp in weight_copies(nxt_ref[i], 1 - wslot):
                    cp.start()

            for cp in weight_copies(te_ref[i], wslot):
                cp.wait()
            wg_bf[...] = wg_f[wslot].astype(BF16)
            wu_bf[...] = wu_f[wslot].astype(BF16)
            wd_bf[...] = wd_f[wslot].astype(BF16)

        pltpu.make_async_copy(h_hbm.at[pl.ds(0, tm * per), :], xbuf.at[slot], sem.at[slot]).wait()
        x = _load_slabs(xbuf.at[slot], tm, per).astype(BF16)
        a = (_silu(_dot_nn(x, wg_bf[...])) * _dot_nn(x, wu_bf[...])).astype(BF16)
        _store_slabs(y_ref, _dot_nn(a, wd_bf[...]))

    @pl.when(i >= n_active)
    def _():
        y_ref[...] = jnp.zeros(y_ref.shape, F32)


def _experts(tile_expert, tile_first, sorted_tok, n_tiles, tile_group, next_expert, h2_slabs,
             w_gate, w_up, w_down, tm):
    max_tiles = tile_expert.shape[0]
    _, d, dff = w_gate.shape
    per = d // LANES
    hbm = pl.BlockSpec(memory_space=pl.ANY)
    return pl.pallas_call(
        functools.partial(_experts_body, tm=tm, per=per),
        grid_spec=pltpu.PrefetchScalarGridSpec(
            num_scalar_prefetch=6,
            grid=(max_tiles,),
            in_specs=[hbm, hbm, hbm, hbm],
            out_specs=pl.BlockSpec((tm * per, LANES), lambda i, *prefetch: (i, 0)),
            scratch_shapes=[pltpu.VMEM((GATHER_SLOTS, tm * per, LANES), F32),
                            pltpu.VMEM((2, d, dff), F32), pltpu.VMEM((2, d, dff), F32), pltpu.VMEM((2, dff, d), F32),
                            pltpu.VMEM((d, dff), BF16), pltpu.VMEM((d, dff), BF16), pltpu.VMEM((dff, d), BF16),
                            pltpu.SemaphoreType.DMA((GATHER_SLOTS,)), pltpu.SemaphoreType.DMA((2,))]),
        out_shape=jax.ShapeDtypeStruct((max_tiles * tm * per, LANES), F32),
        compiler_params=_cparams(("arbitrary",)),
        name="experts",
    )(tile_expert, tile_first, sorted_tok, n_tiles, tile_group, next_expert, h2_slabs, w_gate, w_up, w_down)


def _combine_body(pos_ref, y_hbm, x1_ref, route_ref, g2_ref, fw_ref, o_ref, buf, sem, *, tm, row0, per):
    i = pl.program_id(0)

    def gather(tile, slot):
        def issue(g, carry):
            for u in range(GATHER_UNROLL // 2):
                r = g * (GATHER_UNROLL // 2) + u
                tok = row0 + tile * tm + r
                for c in range(2):
                    src = pl.multiple_of(pos_ref[2 * tok + c] * per, per)
                    pltpu.make_async_copy(y_hbm.at[pl.ds(src, per), :],
                                          buf.at[slot, c, pl.ds(r * per, per), :], sem.at[slot]).start(priority=1)
            return carry
        lax.fori_loop(0, tm // (GATHER_UNROLL // 2), issue, 0)

    n_slots = buf.shape[0]
    n_tiles = pl.num_programs(0)

    @pl.when(i == 0)
    def _():
        for ahead in range(n_slots - 1):
            @pl.when(ahead < n_tiles)
            def _():
                gather(ahead, ahead)

    @pl.when(i + n_slots - 1 < n_tiles)
    def _():
        gather(i + n_slots - 1, (i + n_slots - 1) % n_slots)

    slot = i % n_slots
    for c in range(2):
        pltpu.make_async_copy(y_hbm.at[pl.ds(0, tm * per), :], buf.at[slot, c], sem.at[slot]).wait()
    route = route_ref[...]
    moe = (route[:, 2:3] * _load_slabs(buf.at[slot, 0], tm, per)
           + route[:, 3:4] * _load_slabs(buf.at[slot, 1], tm, per))
    x2 = x1_ref[...] + g2_ref[...] * moe
    o_ref[...] = x2 * lax.rsqrt(jnp.mean(x2 * x2, axis=-1, keepdims=True) + EPS) * fw_ref[...]


def _combine(pos, y_slabs, x1, route, g2, final_w, row0, n_rows, tm, rows_per_mod):
    d = x1.shape[1]
    per = d // LANES
    per_token = rows_per_mod == 1
    if per_token:
        mod_spec = pl.BlockSpec((tm, d), lambda i, pos: (i, 0))
    else:
        tiles_per_mod = rows_per_mod // tm
        g2 = g2.reshape(g2.shape[0], 1, d)
        mod_spec = pl.BlockSpec((None, 1, d), lambda i, pos: (i // tiles_per_mod, 0, 0))
    return pl.pallas_call(
        functools.partial(_combine_body, tm=tm, row0=row0, per=per),
        grid_spec=pltpu.PrefetchScalarGridSpec(
            num_scalar_prefetch=1,
            grid=(n_rows // tm,),
            in_specs=[pl.BlockSpec(memory_space=pl.ANY),
                      pl.BlockSpec((tm, d), lambda i, pos: (i, 0)),
                      pl.BlockSpec((tm, LANES), lambda i, pos: (i, 0)),
                      mod_spec,
                      pl.BlockSpec((1, d), lambda i, pos: (0, 0))],
            out_specs=pl.BlockSpec((tm, d), lambda i, pos: (i, 0)),
            scratch_shapes=[pltpu.VMEM((GATHER_SLOTS, 2, tm * per, LANES), F32),
                            pltpu.SemaphoreType.DMA((GATHER_SLOTS,))]),
        out_shape=jax.ShapeDtypeStruct((n_rows, d), F32),
        compiler_params=_cparams(("arbitrary",)),
        name="combine_norm",
    )(pos, y_slabs, x1, route, g2, final_w.reshape(1, d))


def _invert_body(where_ref, out_ref, *, n_pairs, n_out):
    def fill(g, carry):
        for u in range(GATHER_UNROLL):
            a = g * GATHER_UNROLL + u
            out_ref[where_ref[a]] = lax.shift_right_logical(a, 1)
        return carry
    lax.fori_loop(0, n_pairs // GATHER_UNROLL, fill, 0)

    def tail(t, carry):
        out_ref[n_pairs + t] = 0
        return carry
    lax.fori_loop(0, n_out - n_pairs, tail, 0)


def _sorted_tokens(where, n_out):
    n_pairs = where.shape[0]
    assert n_pairs % GATHER_UNROLL == 0
    smem = pl.BlockSpec(memory_space=pltpu.SMEM)
    return pl.pallas_call(
        functools.partial(_invert_body, n_pairs=n_pairs, n_out=n_out),
        in_specs=[smem], out_specs=smem,
        out_shape=jax.ShapeDtypeStruct((n_out,), I32),
        name="sorted_tokens",
    )(where)


def _plan(route, n_experts, tm):
    n = route.shape[0]
    e_flat = route[:, 0:2].astype(I32).reshape(-1)
    onehot = (e_flat[:, None] == jnp.arange(n_experts, dtype=I32)[None, :]).astype(I32)
    csum = jnp.cumsum(onehot, axis=0)
    counts = csum[-1]
    rank = jnp.sum(onehot * csum, axis=1) - 1
    tiles_e = (counts + tm - 1) // tm
    tile_end = jnp.cumsum(tiles_e)
    tile_off = tile_end - tiles_e
    cnt_off = jnp.cumsum(counts) - counts
    max_tiles = (2 * n) // tm + n_experts
    pos = jnp.sum(onehot * (tile_off * tm)[None, :], axis=1) + rank
    where = jnp.sum(onehot * cnt_off[None, :], axis=1) + rank
    tile_ids = jnp.arange(max_tiles, dtype=I32)
    live_ids = jnp.minimum(tile_ids, tile_end[-1] - 1)
    tile_expert = jnp.minimum(jnp.sum((tile_end[None, :] <= live_ids[:, None]).astype(I32), axis=1), n_experts - 1)
    is_e = (tile_expert[:, None] == jnp.arange(n_experts, dtype=I32)[None, :]).astype(I32)
    tile_first = jnp.sum(is_e * (cnt_off - tile_off * tm)[None, :], axis=1) + tile_ids * tm
    tile_first = jnp.clip(tile_first, 0, 2 * n)
    e_ids = jnp.arange(n_experts, dtype=I32)
    used = tiles_e > 0
    run_of_e = jnp.cumsum(used.astype(I32)) - 1
    later_used = used[None, :] & (e_ids[None, :] > e_ids[:, None])
    next_of_e = jnp.min(jnp.where(later_used, e_ids[None, :], n_experts), axis=1)
    next_of_e = jnp.where(next_of_e < n_experts, next_of_e, -1)
    tile_group = jnp.sum(is_e * run_of_e[None, :], axis=1)
    next_expert = jnp.sum(is_e * next_of_e[None, :], axis=1)
    return (tile_expert, tile_first.astype(I32), tile_end[-1].reshape(1).astype(I32), tile_group.astype(I32),
            next_expert.astype(I32), pos.astype(I32), where.astype(I32))


def kernel(x_prompt, x_sample, cache_k, cache_v, cache_logf, state_gla, page_table, c_prompt, c_sample,
           norm1_w, norm2_w, w_ada, b_ada, w_in, w_gk2, b_gk, b_fgate, gla_onorm_w, fox_onorm_w, w_out,
           w_rg, b_rg, w_re, b_re, w_gate_e, w_up_e, w_down_e, final_norm_w):
    depth = w_in.shape[0]
    assert depth == 1, "single-layer trunk"
    batch, seq, d = x_prompt.shape
    db, t_new, _ = x_sample.shape
    n_hg, dk, dv = state_gla.shape[2:]
    n_hf, dh = cache_k.shape[3:]
    rank = w_gk2.shape[1]
    n_experts = w_gate_e.shape[1]
    wq = n_hg * dk
    wv = n_hg * dv
    wf = n_hf * dh
    assert wv == 1024 and wf == 1024 and 2 * wq == 1024 and rank == 16 and n_hf == 8
    n_p = batch * seq
    n_s = db * t_new
    layer = 0

    wt = w_in[layer].T
    o_alr = 2 * wq + 2 * wv
    o_fox = o_alr + rank
    o_fl = o_fox + 3 * wf
    assert o_fl - o_fox == o_alr
    w_main = _pack_w_in(wt, o_alr, o_fox)
    w_small = jnp.concatenate([wt[o_alr:o_fox], wt[o_fl:], jnp.zeros((LANES - rank - n_hf, d), F32)], axis=0)
    b_small = jnp.zeros((1, LANES), F32).at[0, rank:rank + n_hf].set(b_fgate[layer])
    w_gk2_pad = jnp.zeros((LANES, wq), F32).at[:rank].set(w_gk2[layer])
    b_gk2 = b_gk[layer].reshape(1, wq)
    w_out_bf = w_out[layer].astype(BF16)
    w_r = jnp.concatenate([w_rg[layer], w_re[layer],
                           jnp.zeros((d, LANES - N_GROUPS - n_experts), F32)], axis=1)
    wr_hi = w_r.astype(BF16)
    wr_lo = (w_r - wr_hi.astype(F32)).astype(BF16)
    b_r = jnp.concatenate([b_rg[layer], b_re[layer], jnp.zeros((LANES - N_GROUPS - n_experts,), F32)]).reshape(1, LANES)

    n_c = batch + db
    c_all = jnp.concatenate([c_prompt, c_sample, jnp.zeros((-n_c % 8, d), F32)], axis=0)
    mod = _adaln(c_all, w_ada[layer], b_ada[layer])
    sh1, sc1, g1, sh2, sc2, g2 = (mod[:, i * d:(i + 1) * d] for i in range(6))
    p_rows = slice(0, batch)
    rep = lambda a: jnp.repeat(a[batch:n_c], t_new, axis=0)

    q_scale = dh ** -0.5
    xp = x_prompt.reshape(n_p, d)
    xs = x_sample.reshape(n_s, d)

    tm_p = 512
    gla_p, qf_p, kf_p, vf_p, kfb_p, vfb_p, small_p = _inproj(
        xp, sh1[p_rows], sc1[p_rows], norm1_w[layer], w_main, w_small, b_small, tm_p, seq, q_scale * LOG2E)
    s0_p = jnp.zeros((batch, n_hg, dk, dv), F32)
    o_gla_p, gla_state_p = _gla(gla_p, small_p, w_gk2_pad, b_gk2, s0_p, batch, seq, 128, n_hg, dk, dv)
    logf_p = small_p[:, rank:rank + n_hf]
    key_bias = _key_bias(small_p, batch, seq, rank, n_hf)
    o_fox_p = _fox_prompt(qf_p, kfb_p, vfb_p, key_bias, batch, seq, n_hf, dh, 512)
    n_all = n_p + n_s
    x1_p, h2_p, route_p = _merge(xp, o_gla_p, gla_p, o_fox_p, gla_onorm_w[layer], fox_onorm_w[layer], w_out_bf,
                                 g1[p_rows], sh2[p_rows], sc2[p_rows], norm2_w[layer], wr_hi, wr_lo, b_r,
                                 256, seq, n_hg, dv, n_all, 0)

    gla_s, qf_s, kf_s, vf_s, _, _, small_s = _inproj(
        xs, rep(sh1), rep(sc1), norm1_w[layer], w_main, w_small, b_small, n_s, 1, q_scale)
    t_pad = GLA_CHUNK
    pad_t = lambda a: jnp.pad(a.reshape(db, t_new, -1), ((0, 0), (0, t_pad - t_new), (0, 0))).reshape(db * t_pad, -1)
    o_gla_s_pad, gla_state_s = _gla(pad_t(gla_s), pad_t(small_s), w_gk2_pad, b_gk2, state_gla[layer],
                                    db, t_pad, t_pad, n_hg, dk, dv, t_valid=t_new)
    o_gla_s = o_gla_s_pad.reshape(db, t_pad, wv)[:, :t_new].reshape(n_s, wv)

    logf_s = small_s[:, rank:rank + n_hf]
    o_fox_s = _fox_sample_call(page_table, qf_s, kf_s, vf_s, logf_s, cache_k, cache_v, cache_logf,
                               layer, db, t_new, n_hf, dh, 8)
    x1_s, h2, route_s = _merge(xs, o_gla_s, gla_s, o_fox_s, gla_onorm_w[layer], fox_onorm_w[layer], w_out_bf,
                               rep(g1), rep(sh2), rep(sc2), norm2_w[layer], wr_hi, wr_lo, b_r,
                               n_s, 1, n_hg, dv, n_all, n_p, into=h2_p)

    tm_e = 256
    route = jnp.concatenate([route_p, route_s], axis=0)
    tile_expert, tile_first, n_tiles, tile_group, next_expert, pos, where = _plan(route, n_experts, tm_e)
    sorted_tok = _sorted_tokens(where, where.shape[0] + tm_e)
    y_slabs = _experts(tile_expert, tile_first, sorted_tok, n_tiles, tile_group, next_expert, h2,
                       w_gate_e[layer], w_up_e[layer], w_down_e[layer], tm_e)
    y_p = _combine(pos, y_slabs, x1_p, route_p, g2[p_rows], final_norm_w, 0, n_p, 256, seq)
    y_s = _combine(pos, y_slabs, x1_s, route_s, rep(g2), final_norm_w, n_p, n_s, n_s, 1)

    y_prompt = y_p.reshape(batch, seq, d)
    y_sample = y_s.reshape(db, t_new, d)
    k_prompt = kf_p.reshape(1, batch, seq, n_hf, dh)
    v_prompt = vf_p.reshape(1, batch, seq, n_hf, dh)
    logf_prompt = logf_p.reshape(1, batch, seq, n_hf)
    k_sample = kf_s.reshape(1, db, t_new, n_hf, dh)
    v_sample = vf_s.reshape(1, db, t_new, n_hf, dh)
    logf_sample = logf_s.reshape(1, db, t_new, n_hf)
    return (y_prompt, y_sample, k_prompt, v_prompt, logf_prompt, gla_state_p[None],
            k_sample, v_sample, logf_sample, gla_state_s[None])
```
